```python
import jax, jax.numpy as jnp
from jax import lax
import numpy as np

D_MODEL = 1024
BATCH = 8
SEQ = 8192
DEPTH = 1

M_HEADS = 4
M_HEAD_DIM = D_MODEL // M_HEADS
M_WIDTH = M_HEADS * M_HEAD_DIM
M_CHUNK = 64
CONV_WIDTH = 4
F_HEADS = 8
F_HEAD_DIM = D_MODEL // F_HEADS
F_WIDTH = F_HEADS * F_HEAD_DIM
Q_BLOCK = 128
IN_SIZES = (M_WIDTH, M_WIDTH, M_WIDTH, M_WIDTH, M_HEADS, M_HEADS,
            F_WIDTH, F_WIDTH, F_WIDTH, F_HEADS, D_MODEL, D_MODEL)
D_IN = sum(IN_SIZES)
P_HEADS = 8
P_KEYS = 128
N_EXPERTS = P_KEYS * P_KEYS
P_TOPK = 16
P_QDIM = 256
P_HALF = P_QDIM // 2
P_CHUNK = 128
EPS = 1e-6
M_INIT = -1e30

kernel_name = 'hybrid_mlstm_fox_peer_block'


def rmsnorm(x, g):
    xf = x.astype(jnp.float32)
    r = lax.rsqrt(jnp.mean(xf * xf, axis=-1, keepdims=True) + EPS)
    return (xf * r).astype(x.dtype) * g


def to_heads(a, n_heads):
    B, T, _ = a.shape
    return a.reshape(B, T, n_heads, -1).transpose(0, 2, 1, 3)


def from_heads(a):
    B, H, T, dh = a.shape
    return a.transpose(0, 2, 1, 3).reshape(B, T, H * dh)


def causal_conv(x, w):
    T = x.shape[1]
    xp = jnp.pad(x, ((0, 0), (CONV_WIDTH - 1, 0), (0, 0)))
    out = xp[:, 0:T] * w[0]
    for j in range(1, CONV_WIDTH):
        out = out + xp[:, j:j + T] * w[j]
    return out


def mlstm_chunkwise(q, k, v, i_pre, f_pre):
    B, H, T, dh = q.shape
    L = M_CHUNK
    NC = T // L
    q = q.astype(jnp.float32) * (dh ** -0.5)
    k = k.astype(jnp.float32)
    v = v.astype(jnp.float32)
    i_pre = i_pre.astype(jnp.float32)
    logf = jax.nn.log_sigmoid(f_pre.astype(jnp.float32))

    def chunks(a):
        return jnp.moveaxis(a.reshape((B, H, NC, L) + a.shape[3:]), 2, 0)

    causal = jnp.tril(jnp.ones((L, L), dtype=bool))

    def step(carry, inp):
        C, n, m = carry
        q_, k_, v_, i_, lf = inp
        b = jnp.cumsum(lf, axis=-1)
        g = b[..., -1]
        D = b[..., :, None] - b[..., None, :] + i_[..., None, :]
        D = jnp.where(causal, D, -jnp.inf)
        inter = b + m[..., None]
        m_t = jnp.maximum(jnp.max(D, axis=-1), inter)
        S = jnp.einsum('bhtd,bhsd->bhts', q_, k_) * jnp.exp(D - m_t[..., None])
        w_inter = jnp.exp(inter - m_t)
        num = jnp.einsum('bhts,bhse->bhte', S, v_) + w_inter[..., None] * jnp.einsum('bhtd,bhde->bhte', q_, C)
        den = jnp.sum(S, axis=-1) + w_inter * jnp.einsum('bhtd,bhd->bht', q_, n)
        h = num / jnp.maximum(jnp.abs(den), jnp.exp(-m_t))[..., None]
        log_w = g[..., None] - b + i_
        m_new = jnp.maximum(g + m, jnp.max(log_w, axis=-1))
        w_s = jnp.exp(log_w - m_new[..., None])
        decay = jnp.exp(g + m - m_new)
        C_new = decay[..., None, None] * C + jnp.einsum('bhs,bhsd,bhse->bhde', w_s, k_, v_)
        n_new = decay[..., None] * n + jnp.einsum('bhs,bhsd->bhd', w_s, k_)
        return (C_new, n_new, m_new), h

    init = (jnp.zeros((B, H, dh, dh), jnp.float32),
            jnp.zeros((B, H, dh), jnp.float32),
            jnp.full((B, H), M_INIT, jnp.float32))
    _, hs = lax.scan(step, init, (chunks(q), chunks(k), chunks(v), chunks(i_pre), chunks(logf)))
    return jnp.moveaxis(hs, 0, 2).reshape(B, H, T, dh)


def forgetting_attention(q, k, v, logf):
    B, H, T, dh = q.shape
    NQ = T // Q_BLOCK
    c = jnp.cumsum(logf, axis=-1)
    kpos = jnp.arange(T)
    scale = dh ** -0.5

    def block(i):
        start = i * Q_BLOCK
        qb = lax.dynamic_slice_in_dim(q, start, Q_BLOCK, axis=2)
        cb = lax.dynamic_slice_in_dim(c, start, Q_BLOCK, axis=2)
        s = jnp.einsum('bhqd,bhkd->bhqk', qb, k).astype(jnp.float32) * scale
        s = s + cb[..., :, None] - c[..., None, :]
        qpos = start + jnp.arange(Q_BLOCK)
        s = jnp.where(qpos[:, None] >= kpos[None, :], s, -jnp.inf)
        p = jax.nn.softmax(s, axis=-1)
        return jnp.einsum('bhqk,bhkd->bhqd', p.astype(v.dtype), v)

    out = lax.map(block, jnp.arange(NQ))
    return jnp.moveaxis(out, 0, 2).reshape(B, H, T, dh)


def peer(xn, w_pq, sub_keys, u_tab, v_tab):
    B, T, D = xn.shape
    qh = jnp.einsum('btd,de->bte', xn, w_pq).reshape(B, T, P_HEADS, 2, P_HALF)
    s = jnp.einsum('bthcd,hcnd->bthcn', qh, sub_keys).astype(jnp.float32)
    vals, idx = lax.top_k(s, P_TOPK)
    cand = vals[..., 0, :, None] + vals[..., 1, None, :]
    cand_id = idx[..., 0, :, None] * P_KEYS + idx[..., 1, None, :]
    cand = cand.reshape(B, T, P_HEADS, P_TOPK * P_TOPK)
    cand_id = cand_id.reshape(B, T, P_HEADS, P_TOPK * P_TOPK)
    top_s, pos = lax.top_k(cand, P_TOPK)
    ids = jnp.take_along_axis(cand_id, pos, axis=-1)
    gate = jax.nn.softmax(top_s, axis=-1)
    NC = T // P_CHUNK
    HK = P_HEADS * P_TOPK
    xc = jnp.moveaxis(xn.reshape(B, NC, P_CHUNK, D), 1, 0)
    ic = jnp.moveaxis(ids.reshape(B, NC, P_CHUNK, HK), 1, 0)
    gc = jnp.moveaxis(gate.reshape(B, NC, P_CHUNK, HK), 1, 0)

    def chunk(args):
        xb, ib, gb = args
        u = u_tab[ib]
        a = jax.nn.gelu(jnp.einsum('bcd,bced->bce', xb, u), approximate=False)
        v = v_tab[ib]
        return jnp.einsum('bce,bced->bcd', (gb * a).astype(v.dtype), v)

    out = lax.map(chunk, (xc, ic, gc))
    return jnp.moveaxis(out, 0, 1).reshape(B, T, D)


def setup_inputs(seed: int = 0) -> dict:
    key = jax.random.key(seed)
    ks = jax.random.split(key, 20)

    def nrm(k, shape, s):
        return s * jax.random.normal(k, shape, jnp.float32)

    L = DEPTH
    offs = [int(o) for o in np.cumsum((0,) + IN_SIZES)]
    x = nrm(ks[0], (BATCH, SEQ, D_MODEL), 1.0)
    norm1_g = 1.0 + nrm(ks[1], (L, D_MODEL), 0.02)
    w_in = nrm(ks[2], (L, D_MODEL, D_IN), D_MODEL ** -0.5)
    b_in = nrm(ks[3], (L, D_IN), 0.02)
    b_in = b_in.at[:, offs[4]:offs[5]].add(nrm(ks[4], (L, M_HEADS), 0.1))
    b_in = b_in.at[:, offs[5]:offs[6]].add(jnp.linspace(3.0, 6.0, M_HEADS))
    b_in = b_in.at[:, offs[9]:offs[10]].add(jnp.linspace(1.0, 5.0, F_HEADS))
    conv_w = nrm(ks[5], (L, CONV_WIDTH, 2 * M_WIDTH), CONV_WIDTH ** -0.5)
    m_norm_g = 1.0 + nrm(ks[6], (L, M_WIDTH), 0.02)
    qn_g = 1.0 + nrm(ks[7], (L, F_HEAD_DIM), 0.02)
    kn_g = 1.0 + nrm(ks[8], (L, F_HEAD_DIM), 0.02)
    w_m_out = nrm(ks[9], (L, M_WIDTH, D_MODEL), M_WIDTH ** -0.5)
    w_f_out = nrm(ks[10], (L, F_WIDTH, D_MODEL), F_WIDTH ** -0.5)
    w_out = nrm(ks[11], (L, D_MODEL, D_MODEL), D_MODEL ** -0.5)
    norm2_g = 1.0 + nrm(ks[12], (L, D_MODEL), 0.02)
    w_pq = nrm(ks[13], (L, D_MODEL, P_HEADS * P_QDIM), D_MODEL ** -0.5)
    sub_keys = nrm(ks[14], (L, P_HEADS, 2, P_KEYS, P_HALF), P_HALF ** -0.5)
    u_tab = nrm(ks[15], (L, N_EXPERTS, D_MODEL), D_MODEL ** -0.5)
    v_tab = nrm(ks[16], (L, N_EXPERTS, D_MODEL), P_HEADS ** -0.5)
    return {'x': x, 'norm1_g': norm1_g, 'w_in': w_in, 'b_in': b_in, 'conv_w': conv_w,
            'm_norm_g': m_norm_g, 'qn_g': qn_g, 'kn_g': kn_g, 'w_m_out': w_m_out,
            'w_f_out': w_f_out, 'w_out': w_out, 'norm2_g': norm2_g, 'w_pq': w_pq,
            'sub_keys': sub_keys, 'u_tab': u_tab, 'v_tab': v_tab}


def reference(x, norm1_g, w_in, b_in, conv_w, m_norm_g, qn_g, kn_g, w_m_out,
              w_f_out, w_out, norm2_g, w_pq, sub_keys, u_tab, v_tab):
    split_at = np.cumsum(IN_SIZES)[:-1].tolist()
    for l in range(DEPTH):
        h = rmsnorm(x, norm1_g[l])
        z = jnp.einsum('btd,de->bte', h, w_in[l]) + b_in[l]
        mq, mk, mv, mo, mi, mf, fq, fk, fv, ff, gm, gf = jnp.split(z, split_at, axis=-1)
        qk = jax.nn.silu(causal_conv(jnp.concatenate([mq, mk], axis=-1), conv_w[l]))
        mq, mk = jnp.split(qk, 2, axis=-1)
        hm = mlstm_chunkwise(to_heads(mq, M_HEADS), to_heads(mk, M_HEADS), to_heads(mv, M_HEADS),
                             mi.transpose(0, 2, 1), mf.transpose(0, 2, 1))
        hm = rmsnorm(hm, m_norm_g[l].reshape(M_HEADS, 1, M_HEAD_DIM)).astype(x.dtype)
        hm = from_heads(hm) * jax.nn.sigmoid(mo)
        fq_h = rmsnorm(to_heads(fq, F_HEADS), qn_g[l])
        fk_h = rmsnorm(to_heads(fk, F_HEADS), kn_g[l])
        logf = jax.nn.log_sigmoid(ff.astype(jnp.float32)).transpose(0, 2, 1)
        hf = from_heads(forgetting_attention(fq_h, fk_h, to_heads(fv, F_HEADS), logf))
        y = (jax.nn.sigmoid(gm) * jnp.einsum('bte,ed->btd', hm, w_m_out[l])
             + jax.nn.sigmoid(gf) * jnp.einsum('bte,ed->btd', hf, w_f_out[l]))
        x = x + jnp.einsum('btd,de->bte', y, w_out[l]).astype(x.dtype)
        x = x + peer(rmsnorm(x, norm2_g[l]), w_pq[l], sub_keys[l], u_tab[l], v_tab[l]).astype(x.dtype)
    return x
```

```python
import functools

import jax
import jax.numpy as jnp
from jax import lax
from jax.experimental import pallas as pl
from jax.experimental.pallas import tpu as pltpu

F32 = jnp.float32
BF16 = jnp.bfloat16
I32 = jnp.int32

D_MODEL = 1024
M_HEADS = 4
M_HEAD_DIM = 256
F_HEADS = 8
F_HEAD_DIM = 128
CONV_WIDTH = 4
P_HEADS = 8
P_KEYS = 128
P_TOPK = 16
P_HALF = 128
N_EXPERTS = P_KEYS * P_KEYS
EPS = 1e-6
M_INIT = -1e30
LANES = 128
SUBLANES = 8
VMEM_LIMIT = 56 * 1024 * 1024

_SIZES = (1024, 1024, 1024, 1024, M_HEADS, M_HEADS, 1024, 1024, 1024, F_HEADS, 1024, 1024)
_OFFS = [0]
for _s in _SIZES:
    _OFFS.append(_OFFS[-1] + _s)
Z_COLS = 9 * 1024
ZB_MQ, ZB_MK, ZB_MV, ZB_MO, ZB_FQ, ZB_FK, ZB_FV, ZB_GM, ZB_GF = range(9)
G_MI, G_MF, G_FF = 0, M_HEADS, 2 * M_HEADS


def _cparams(sem):
    return pltpu.CompilerParams(dimension_semantics=sem, vmem_limit_bytes=VMEM_LIMIT)


def _log_sigmoid(x):
    return jnp.minimum(x, 0.0) - jnp.log1p(jnp.exp(-jnp.abs(x)))


def _sigmoid(x):
    return 1.0 / (1.0 + jnp.exp(-x))


def _inproj_kernel(x_ref, g_ref, w_ref, b_ref, ws_ref, wst_ref, bs_ref, bst_ref,
                   z_ref, gc_ref, gr_ref, h_sc):
    j = pl.program_id(1)

    @pl.when(j == 0)
    def _():
        x = x_ref[...]
        r = lax.rsqrt(jnp.mean(x * x, axis=-1, keepdims=True) + EPS)
        h = ((x * r) * g_ref[...]).astype(BF16)
        h_sc[...] = h
        gc_ref[...] = jnp.dot(h, ws_ref[...], preferred_element_type=F32) + bs_ref[...]
        gr_ref[...] = lax.dot_general(wst_ref[...], h, (((1,), (1,)), ((), ())),
                                      preferred_element_type=F32) + bst_ref[...]

    z = jnp.dot(h_sc[...], w_ref[...], preferred_element_type=F32) + b_ref[...]
    z_ref[...] = z.astype(BF16)


def _inproj(x2, g1, w_main, b_main, w_small, w_small_t, b_small, b_small_t, tm):
    n = x2.shape[0]
    tn = 1024
    grid = (n // tm, Z_COLS // tn)
    return pl.pallas_call(
        _inproj_kernel,
        grid=grid,
        in_specs=[
            pl.BlockSpec((tm, D_MODEL), lambda i, j: (i, 0)),
            pl.BlockSpec((1, D_MODEL), lambda i, j: (0, 0)),
            pl.BlockSpec((D_MODEL, tn), lambda i, j: (0, j)),
            pl.BlockSpec((1, tn), lambda i, j: (0, j)),
            pl.BlockSpec((D_MODEL, LANES), lambda i, j: (0, 0)),
            pl.BlockSpec((LANES, D_MODEL), lambda i, j: (0, 0)),
            pl.BlockSpec((1, LANES), lambda i, j: (0, 0)),
            pl.BlockSpec((LANES, 1), lambda i, j: (0, 0)),
        ],
        out_specs=[
            pl.BlockSpec((tm, tn), lambda i, j: (i, j)),
            pl.BlockSpec((tm, LANES), lambda i, j: (i, 0)),
            pl.BlockSpec((LANES, tm), lambda i, j: (0, i)),
        ],
        out_shape=[
            jax.ShapeDtypeStruct((n, Z_COLS), BF16),
            jax.ShapeDtypeStruct((n, LANES), F32),
            jax.ShapeDtypeStruct((LANES, n), F32),
        ],
        scratch_shapes=[pltpu.VMEM((tm, D_MODEL), BF16)],
        compiler_params=_cparams(("parallel", "arbitrary")),
        name="inproj",
    )(x2, g1, w_main, b_main, w_small, w_small_t, b_small, b_small_t)


def _cumsum_kernel(gr_ref, crow_ref, ccol_ref):
    x = _log_sigmoid(gr_ref[...])
    t = x.shape[1]
    lane = lax.broadcasted_iota(I32, x.shape, 1)
    s = 1
    while s < t:
        x = x + jnp.where(lane >= s, pltpu.roll(x, s, axis=1), 0.0)
        s *= 2
    crow_ref[...] = x
    ccol_ref[...] = x.T


def _forget_cumsum(gates_row, batch, seq):
    n = batch * seq
    return pl.pallas_call(
        _cumsum_kernel,
        grid=(batch,),
        in_specs=[pl.BlockSpec((LANES, seq), lambda b: (0, b))],
        out_specs=[pl.BlockSpec((LANES, seq), lambda b: (0, b)),
                   pl.BlockSpec((seq, LANES), lambda b: (b, 0))],
        out_shape=[jax.ShapeDtypeStruct((LANES, n), F32),
                   jax.ShapeDtypeStruct((n, LANES), F32)],
        compiler_params=_cparams(("parallel",)),
        name="forget_cumsum",
    )(gates_row)


def _mlstm_kernel(q_ref, k_ref, v_ref, o_ref, gc_ref, gr_ref, cw_ref, mg_ref, out_ref,
                  c_sc, n_sc, m_sc, ext_sc, *, chunk):
    L = chunk
    dh = M_HEAD_DIM
    t = pl.program_id(1)

    @pl.when(t == 0)
    def _():
        c_sc[...] = jnp.zeros_like(c_sc)
        n_sc[...] = jnp.zeros_like(n_sc)
        m_sc[...] = jnp.full_like(m_sc, M_INIT)
        ext_sc[0:SUBLANES, :] = jnp.zeros((SUBLANES, 2 * D_MODEL), F32)

    ext_sc[SUBLANES:SUBLANES + L, 0:D_MODEL] = q_ref[...].astype(F32)
    ext_sc[SUBLANES:SUBLANES + L, D_MODEL:2 * D_MODEL] = k_ref[...].astype(F32)

    gc = gc_ref[...]
    gr = gr_ref[...]
    row_i = lax.broadcasted_iota(I32, (L, L), 0)
    col_i = lax.broadcasted_iota(I32, (L, L), 1)
    tril = col_i <= row_i

    def conv_silu(c0):
        base = SUBLANES - (CONV_WIDTH - 1)
        acc = ext_sc[base:base + L, c0:c0 + dh] * cw_ref[0:1, c0:c0 + dh]
        for j in range(1, CONV_WIDTH):
            acc = acc + ext_sc[base + j:base + j + L, c0:c0 + dh] * cw_ref[j:j + 1, c0:c0 + dh]
        return acc * _sigmoid(acc)

    for h in range(M_HEADS):
        q = conv_silu(h * dh) * (dh ** -0.5)
        k = conv_silu(D_MODEL + h * dh)
        v = v_ref[:, h * dh:(h + 1) * dh]
        qb = q.astype(BF16)
        kb = k.astype(BF16)

        i_col = gc[:, G_MI + h:G_MI + h + 1]
        lf_col = _log_sigmoid(gc[:, G_MF + h:G_MF + h + 1])
        i_row = gr[G_MI + h:G_MI + h + 1, :]
        lf_row = _log_sigmoid(gr[G_MF + h:G_MF + h + 1, :])

        b_col = jnp.sum(jnp.where(tril, lf_row, 0.0), axis=1, keepdims=True)
        b_row = jnp.sum(jnp.where(row_i <= col_i, lf_col, 0.0), axis=0, keepdims=True)
        g = jnp.sum(lf_row, axis=1, keepdims=True)
        m_prev = m_sc[h][0:1, 0:1]

        dmat = jnp.where(tril, b_col - b_row + i_row, -jnp.inf)
        inter = b_col + m_prev
        m_t = jnp.maximum(jnp.max(dmat, axis=1, keepdims=True), inter)
        p = jnp.exp(dmat - m_t)
        s = lax.dot_general(qb, kb, (((1,), (1,)), ((), ())), preferred_element_type=F32) * p
        w_inter = jnp.exp(inter - m_t)
        c_old = c_sc[h]
        n_old = n_sc[h]
        num = (jnp.dot(s.astype(BF16), v, preferred_element_type=F32)
               + w_inter * jnp.dot(qb, c_old.astype(BF16), preferred_element_type=F32))
        den = (jnp.sum(s, axis=1, keepdims=True)
               + w_inter * jnp.sum(q * n_old, axis=1, keepdims=True))
        hh = num / jnp.maximum(jnp.abs(den), jnp.exp(-m_t))

        log_w = g - b_col + i_col
        m_new = jnp.maximum(g + m_prev, jnp.max(log_w, axis=0, keepdims=True))
        w_s = jnp.exp(log_w - m_new)
        decay = jnp.exp(g + m_prev - m_new)
        wv = (w_s * v.astype(F32)).astype(BF16)
        c_sc[h] = decay * c_old + lax.dot_general(kb, wv, (((0,), (0,)), ((), ())),
                                                  preferred_element_type=F32)
        n_sc[h] = decay * n_old + jnp.sum(w_s * k, axis=0, keepdims=True)
        m_sc[h] = jnp.broadcast_to(m_new, (SUBLANES, LANES))

        r = lax.rsqrt(jnp.mean(hh * hh, axis=-1, keepdims=True) + EPS)
        hn = (hh * r) * mg_ref[0:1, h * dh:(h + 1) * dh]
        og = _sigmoid(o_ref[:, h * dh:(h + 1) * dh].astype(F32))
        out_ref[:, h * dh:(h + 1) * dh] = (hn * og).astype(BF16)

    ext_sc[0:SUBLANES, :] = ext_sc[L:L + SUBLANES, :]


def _mlstm(z, gates_col, gates_row, conv_w, m_norm_g, batch, seq, chunk):
    n = batch * seq
    nt = seq // chunk
    zspec = lambda cb: pl.BlockSpec((chunk, D_MODEL), lambda b, t, cb=cb: (b * nt + t, cb))
    return pl.pallas_call(
        functools.partial(_mlstm_kernel, chunk=chunk),
        grid=(batch, nt),
        in_specs=[
            zspec(ZB_MQ), zspec(ZB_MK), zspec(ZB_MV), zspec(ZB_MO),
            pl.BlockSpec((chunk, LANES), lambda b, t: (b * nt + t, 0)),
            pl.BlockSpec((LANES, chunk), lambda b, t: (0, b * nt + t)),
            pl.BlockSpec((CONV_WIDTH, 2 * D_MODEL), lambda b, t: (0, 0)),
            pl.BlockSpec((1, D_MODEL), lambda b, t: (0, 0)),
        ],
        out_specs=pl.BlockSpec((chunk, D_MODEL), lambda b, t: (b * nt + t, 0)),
        out_shape=jax.ShapeDtypeStruct((n, D_MODEL), BF16),
        scratch_shapes=[
            pltpu.VMEM((M_HEADS, M_HEAD_DIM, M_HEAD_DIM), F32),
            pltpu.VMEM((M_HEADS, 1, M_HEAD_DIM), F32),
            pltpu.VMEM((M_HEADS, SUBLANES, LANES), F32),
            pltpu.VMEM((chunk + SUBLANES, 2 * D_MODEL), F32),
        ],
        compiler_params=_cparams(("parallel", "arbitrary")),
        name="mlstm",
    )(z, z, z, z, gates_col, gates_row, conv_w, m_norm_g)


def _fox_kernel(q_ref, k_ref, v_ref, ccol_ref, crow_ref, qg_ref, kg_ref, out_ref,
                kn_sc, m_sc, l_sc, acc_sc, *, tq, seq):
    h = pl.program_id(1)
    i = pl.program_id(2)
    dh = F_HEAD_DIM

    @pl.when(i == 0)
    def _():
        def norm_chunk(c, carry):
            r0 = pl.multiple_of(c * tq, tq)
            kk = k_ref[pl.ds(r0, tq), :].astype(F32)
            r = lax.rsqrt(jnp.mean(kk * kk, axis=-1, keepdims=True) + EPS)
            kn_sc[pl.ds(r0, tq), :] = ((kk * r) * kg_ref[...]).astype(BF16)
            return carry
        lax.fori_loop(0, seq // tq, norm_chunk, 0)

    q = q_ref[...].astype(F32)
    r = lax.rsqrt(jnp.mean(q * q, axis=-1, keepdims=True) + EPS)
    qn = (((q * r) * qg_ref[...]) * (dh ** -0.5)).astype(BF16)

    lane = lax.broadcasted_iota(I32, (tq, LANES), 1)
    cq = jnp.sum(jnp.where(lane == G_FF + h, ccol_ref[...], 0.0), axis=1, keepdims=True)

    m_sc[...] = jnp.full_like(m_sc, -jnp.inf)
    l_sc[...] = jnp.zeros_like(l_sc)
    acc_sc[...] = jnp.zeros_like(acc_sc)

    def step(j, masked):
        r0 = pl.multiple_of(j * tq, tq)
        kn = kn_sc[pl.ds(r0, tq), :]
        vv = v_ref[pl.ds(r0, tq), :]
        ck = crow_ref[pl.ds(h, 1), pl.ds(r0, tq)]
        s = lax.dot_general(qn, kn, (((1,), (1,)), ((), ())), preferred_element_type=F32)
        s = s + cq - ck
        if masked:
            rr = lax.broadcasted_iota(I32, (tq, tq), 0)
            cc = lax.broadcasted_iota(I32, (tq, tq), 1)
            s = jnp.where(rr >= cc, s, -jnp.inf)
        m_old = m_sc[...]
        m_new = jnp.maximum(m_old, jnp.max(s, axis=1, keepdims=True))
        p = jnp.exp(s - m_new)
        alpha = jnp.exp(m_old - m_new)
        l_sc[...] = alpha * l_sc[...] + jnp.sum(p, axis=1, keepdims=True)
        acc_sc[...] = alpha * acc_sc[...] + jnp.dot(p.astype(BF16), vv, preferred_element_type=F32)
        m_sc[...] = m_new

    def body(j, carry):
        step(j, False)
        return carry

    lax.fori_loop(0, i, body, 0)
    step(i, True)
    out_ref[...] = (acc_sc[...] / l_sc[...]).astype(BF16)


def _fox(z, ccol, crow, qn_g, kn_g, batch, seq, tq):
    n = batch * seq
    nq = seq // tq
    fq0 = ZB_FQ * (D_MODEL // F_HEAD_DIM)
    fk0 = ZB_FK * (D_MODEL // F_HEAD_DIM)
    fv0 = ZB_FV * (D_MODEL // F_HEAD_DIM)
    return pl.pallas_call(
        functools.partial(_fox_kernel, tq=tq, seq=seq),
        grid=(batch, F_HEADS, nq),
        in_specs=[
            pl.BlockSpec((tq, F_HEAD_DIM), lambda b, h, i: (b * nq + i, fq0 + h)),
            pl.BlockSpec((seq, F_HEAD_DIM), lambda b, h, i: (b, fk0 + h)),
            pl.BlockSpec((seq, F_HEAD_DIM), lambda b, h, i: (b, fv0 + h)),
            pl.BlockSpec((tq, LANES), lambda b, h, i: (b * nq + i, 0)),
            pl.BlockSpec((SUBLANES, seq), lambda b, h, i: (G_FF // SUBLANES, b)),
            pl.BlockSpec((1, F_HEAD_DIM), lambda b, h, i: (0, 0)),
            pl.BlockSpec((1, F_HEAD_DIM), lambda b, h, i: (0, 0)),
        ],
        out_specs=pl.BlockSpec((tq, F_HEAD_DIM), lambda b, h, i: (b * nq + i, h)),
        out_shape=jax.ShapeDtypeStruct((n, D_MODEL), BF16),
        scratch_shapes=[
            pltpu.VMEM((seq, F_HEAD_DIM), BF16),
            pltpu.VMEM((tq, 1), F32),
            pltpu.VMEM((tq, 1), F32),
            pltpu.VMEM((tq, F_HEAD_DIM), F32),
        ],
        compiler_params=_cparams(("parallel", "parallel", "arbitrary")),
        name="fox",
    )(z, z, z, ccol, crow, qn_g, kn_g)


def _topk_rows(s, k, store):
    n_rows = s.shape[0]
    iota = lax.broadcasted_iota(I32, s.shape, 0)
    for r in range(k):
        mx = jnp.max(s, axis=0, keepdims=True)
        ix = jnp.min(jnp.where(s == mx, iota, n_rows), axis=0, keepdims=True)
        s = jnp.where(iota == ix, -jnp.inf, s)
        store(r, mx, ix)


def _merge_kernel(hm_ref, hf_ref, gm_ref, gf_ref, x_ref, wm_ref, wf_ref, wo_ref, g2_ref,
                  wpq_ref, sk_ref, x1_ref, xn_ref, ids_i_ref, ids_j_ref, gate_ref,
                  sc_sc, val_sc, idx_sc, top_sc, pos_sc, oi_sc, oj_sc, og_sc, *, tm, tc):
    ym = jnp.dot(hm_ref[...], wm_ref[...], preferred_element_type=F32)
    yf = jnp.dot(hf_ref[...], wf_ref[...], preferred_element_type=F32)
    y = (_sigmoid(gm_ref[...].astype(F32)) * ym + _sigmoid(gf_ref[...].astype(F32)) * yf)
    x1 = x_ref[...] + jnp.dot(y.astype(BF16), wo_ref[...], preferred_element_type=F32)
    x1_ref[...] = x1
    r = lax.rsqrt(jnp.mean(x1 * x1, axis=-1, keepdims=True) + EPS)
    xn = ((x1 * r) * g2_ref[...]).astype(BF16)
    xn_ref[...] = xn
    qh = jnp.dot(xn, wpq_ref[...], preferred_element_type=F32).astype(BF16)

    for hc in range(2 * P_HEADS):
        sc_sc[hc] = lax.dot_general(sk_ref[hc], qh[:, hc * P_HALF:(hc + 1) * P_HALF],
                                    (((1,), (1,)), ((), ())), preferred_element_type=F32)

    def chunk_body(c, carry):
        c0 = pl.multiple_of(c * tc, tc)

        def first_level(hc, carry2):
            s = sc_sc[hc, :, pl.ds(c0, tc)]

            def store(r, val, idx):
                val_sc[hc, r:r + 1, :] = val
                idx_sc[hc, r:r + 1, :] = idx
            _topk_rows(s, P_TOPK, store)
            return carry2
        lax.fori_loop(0, 2 * P_HEADS, first_level, 0)

        def second_level(hd, carry2):
            v0 = val_sc[2 * hd]
            v1 = val_sc[2 * hd + 1]
            i0 = idx_sc[2 * hd]
            i1 = idx_sc[2 * hd + 1]
            cand = jnp.concatenate([v0[a:a + 1, :] + v1 for a in range(P_TOPK)], axis=0)

            def store(r, val, idx):
                top_sc[r:r + 1, :] = val
                pos_sc[r:r + 1, :] = idx
            _topk_rows(cand, P_TOPK, store)
            top = top_sc[...]
            pos = pos_sc[...]
            pa = pos >> 4
            pb = pos & (P_TOPK - 1)
            sel_i = jnp.zeros(pos.shape, I32)
            sel_j = jnp.zeros(pos.shape, I32)
            for a in range(P_TOPK):
                sel_i = jnp.where(pa == a, i0[a:a + 1, :], sel_i)
                sel_j = jnp.where(pb == a, i1[a:a + 1, :], sel_j)
            e = jnp.exp(top - top[0:1, :])
            gate = e / jnp.sum(e, axis=0, keepdims=True)
            r0 = pl.multiple_of(hd * P_TOPK, P_TOPK)
            oi_sc[pl.ds(r0, P_TOPK), :] = sel_i
            oj_sc[pl.ds(r0, P_TOPK), :] = sel_j
            og_sc[pl.ds(r0, P_TOPK), :] = gate
            return carry2
        lax.fori_loop(0, P_HEADS, second_level, 0)

        ids_i_ref[pl.ds(c0, tc), :] = oi_sc[...].T
        ids_j_ref[pl.ds(c0, tc), :] = oj_sc[...].T
        gate_ref[pl.ds(c0, tc), :] = og_sc[...].T
        return carry
    lax.fori_loop(0, tm // tc, chunk_body, 0)


def _merge(hm, hf, z, x2, w_m, w_f, w_o, g2, w_pq, sk, tm, tc):
    n = x2.shape[0]
    hk = P_HEADS * P_TOPK
    row = lambda i: (i, 0)
    const = lambda i: (0, 0)
    return pl.pallas_call(
        functools.partial(_merge_kernel, tm=tm, tc=tc),
        grid=(n // tm,),
        in_specs=[
            pl.BlockSpec((tm, D_MODEL), row),
            pl.BlockSpec((tm, D_MODEL), row),
            pl.BlockSpec((tm, D_MODEL), lambda i: (i, ZB_GM)),
            pl.BlockSpec((tm, D_MODEL), lambda i: (i, ZB_GF)),
            pl.BlockSpec((tm, D_MODEL), row),
            pl.BlockSpec((D_MODEL, D_MODEL), const),
            pl.BlockSpec((D_MODEL, D_MODEL), const),
            pl.BlockSpec((D_MODEL, D_MODEL), const),
            pl.BlockSpec((1, D_MODEL), const),
            pl.BlockSpec((D_MODEL, 2 * P_HEADS * P_HALF), const),
            pl.BlockSpec((2 * P_HEADS, P_KEYS, P_HALF), lambda i: (0, 0, 0)),
        ],
        out_specs=[
            pl.BlockSpec((tm, D_MODEL), row),
            pl.BlockSpec((tm, D_MODEL), row),
            pl.BlockSpec((tm, hk), row),
            pl.BlockSpec((tm, hk), row),
            pl.BlockSpec((tm, hk), row),
        ],
        out_shape=[
            jax.ShapeDtypeStruct((n, D_MODEL), F32),
            jax.ShapeDtypeStruct((n, D_MODEL), BF16),
            jax.ShapeDtypeStruct((n, hk), I32),
            jax.ShapeDtypeStruct((n, hk), I32),
            jax.ShapeDtypeStruct((n, hk), F32),
        ],
        scratch_shapes=[
            pltpu.VMEM((2 * P_HEADS, P_KEYS, tm), F32),
            pltpu.VMEM((2 * P_HEADS, P_TOPK, tc), F32),
            pltpu.VMEM((2 * P_HEADS, P_TOPK, tc), I32),
            pltpu.VMEM((P_TOPK, tc), F32),
            pltpu.VMEM((P_TOPK, tc), I32),
            pltpu.VMEM((hk, tc), I32),
            pltpu.VMEM((hk, tc), I32),
            pltpu.VMEM((hk, tc), F32),
        ],
        compiler_params=_cparams(("parallel",)),
        name="merge_retrieve",
    )(hm, hf, z, z, x2, w_m, w_f, w_o, g2, w_pq, sk)


def _peer_u_kernel(xn_ref, u_ref, ii_ref, jj_ref, gate_ref, hw_ref, ids_ref, acc_sc, *, eb):
    e = pl.program_id(1)
    ne = pl.num_programs(1)

    @pl.when(e == 0)
    def _():
        acc_sc[...] = jnp.zeros_like(acc_sc)

    a = lax.dot_general(xn_ref[...], u_ref[...], (((1,), (1,)), ((), ())),
                        preferred_element_type=F32)
    ii = ii_ref[...]
    jj = jj_ref[...]
    acc = acc_sc[...]
    for c in range(eb // P_KEYS):
        blk = e * (eb // P_KEYS) + c
        g = jnp.take_along_axis(a[:, c * P_KEYS:(c + 1) * P_KEYS], jj, axis=1,
                                mode="promise_in_bounds")
        acc = jnp.where(ii == blk, g, acc)
    acc_sc[...] = acc

    @pl.when(e == ne - 1)
    def _():
        av = acc_sc[...]
        gelu = 0.5 * av * (1.0 + lax.erf(av * (2.0 ** -0.5)))
        hw_ref[...] = gate_ref[...] * gelu
        ids_ref[...] = (ii * P_KEYS + jj) * (D_MODEL // (2 * LANES))


def _peer_u(xn, u_bf, ids_i, ids_j, gate, tm, eb):
    n = xn.shape[0]
    hk = P_HEADS * P_TOPK
    row = lambda i, e: (i, 0)
    return pl.pallas_call(
        functools.partial(_peer_u_kernel, eb=eb),
        grid=(n // tm, N_EXPERTS // eb),
        in_specs=[
            pl.BlockSpec((tm, D_MODEL), row),
            pl.BlockSpec((eb, D_MODEL), lambda i, e: (e, 0)),
            pl.BlockSpec((tm, hk), row),
            pl.BlockSpec((tm, hk), row),
            pl.BlockSpec((tm, hk), row),
        ],
        out_specs=[pl.BlockSpec((tm, hk), row), pl.BlockSpec((tm, hk), row)],
        out_shape=[jax.ShapeDtypeStruct((n, hk), F32), jax.ShapeDtypeStruct((n, hk), I32)],
        scratch_shapes=[pltpu.VMEM((tm, hk), F32)],
        compiler_params=_cparams(("parallel", "arbitrary")),
        name="peer_u",
    )(xn, u_bf, ids_i, ids_j, gate)


V_ROW_WORDS = D_MODEL // (2 * LANES)


def _peer_v_kernel(ids_ref, w_ref, x1_ref, tab_ref, out_ref, *, tt):
    hk = P_HEADS * P_TOPK

    def token(t, carry):
        acc_lo = [jnp.zeros((V_ROW_WORDS, LANES), F32) for _ in range(2)]
        acc_hi = [jnp.zeros((V_ROW_WORDS, LANES), F32) for _ in range(2)]
        for m in range(hk):
            r0 = pl.multiple_of(ids_ref[t, m], V_ROW_WORDS)
            w = w_ref[t, m]
            row = tab_ref[pl.ds(r0, V_ROW_WORDS), :]
            lo = lax.bitcast_convert_type(row << 16, F32)
            hi = lax.bitcast_convert_type(row & jnp.int32(-65536), F32)
            acc_lo[m % 2] = acc_lo[m % 2] + w * lo
            acc_hi[m % 2] = acc_hi[m % 2] + w * hi
        out_ref[t, 0:V_ROW_WORDS, :] = x1_ref[t, 0:V_ROW_WORDS, :] + (acc_lo[0] + acc_lo[1])
        out_ref[t, V_ROW_WORDS:2 * V_ROW_WORDS, :] = (x1_ref[t, V_ROW_WORDS:2 * V_ROW_WORDS, :]
                                                      + (acc_hi[0] + acc_hi[1]))
        return carry
    lax.fori_loop(0, tt, token, 0)


def _peer_v(ids, hw, x1_3d, v_packed, tt):
    n = ids.shape[0]
    hk = P_HEADS * P_TOPK
    return pl.pallas_call(
        functools.partial(_peer_v_kernel, tt=tt),
        grid=(n // tt,),
        in_specs=[
            pl.BlockSpec((tt, hk), lambda i: (i, 0), memory_space=pltpu.SMEM),
            pl.BlockSpec((tt, hk), lambda i: (i, 0), memory_space=pltpu.SMEM),
            pl.BlockSpec((tt, 2 * V_ROW_WORDS, LANES), lambda i: (i, 0, 0)),
            pl.BlockSpec((N_EXPERTS * V_ROW_WORDS, LANES), lambda i: (0, 0),
                         pipeline_mode=pl.Buffered(1)),
        ],
        out_specs=pl.BlockSpec((tt, 2 * V_ROW_WORDS, LANES), lambda i: (i, 0, 0)),
        out_shape=jax.ShapeDtypeStruct((n, 2 * V_ROW_WORDS, LANES), F32),
        compiler_params=_cparams(("arbitrary",)),
        name="peer_v",
    )(ids, hw, x1_3d, v_packed)


def _pack_v(v_tab):
    vb = lax.bitcast_convert_type(v_tab.astype(BF16), jnp.uint16).astype(jnp.uint32)
    half = D_MODEL // 2
    packed = vb[:, :half] | (vb[:, half:] << 16)
    return lax.bitcast_convert_type(packed, I32).reshape(N_EXPERTS * V_ROW_WORDS, LANES)


def _layer(x2, batch, seq, norm1_g, w_in, b_in, conv_w, m_norm_g, qn_g, kn_g, w_m_out, w_f_out,
           w_out, norm2_g, w_pq, sub_keys, u_tab, v_tab):
    o = _OFFS
    cols = lambda a: jnp.concatenate([a[..., o[0]:o[4]], a[..., o[6]:o[9]], a[..., o[10]:o[12]]], axis=-1)
    small = lambda a: jnp.concatenate([a[..., o[4]:o[6]], a[..., o[9]:o[10]]], axis=-1)
    w_main = cols(w_in).astype(BF16)
    b_main = cols(b_in)[None, :]
    ws = small(w_in)
    bs = small(b_in)
    pad = LANES - ws.shape[1]
    ws = jnp.pad(ws, ((0, 0), (0, pad))).astype(BF16)
    bs = jnp.pad(bs, (0, pad))

    tm_a = min(1024, x2.shape[0])
    z, gates_col, gates_row = _inproj(x2, norm1_g[None, :], w_main, b_main, ws, ws.T,
                                      bs[None, :], bs[:, None], tm_a)
    crow, ccol = _forget_cumsum(gates_row, batch, seq)
    chunk = min(256, seq)
    hm = _mlstm(z, gates_col, gates_row, conv_w, m_norm_g[None, :], batch, seq, chunk)
    tq = min(512, seq)
    hf = _fox(z, ccol, crow, qn_g[None, :], kn_g[None, :], batch, seq, tq)

    sk = sub_keys.reshape(2 * P_HEADS, P_KEYS, P_HALF).astype(BF16)
    tm_d = min(512, x2.shape[0])
    x1, xn, ids_i, ids_j, gate = _merge(hm, hf, z, x2, w_m_out.astype(BF16), w_f_out.astype(BF16),
                                        w_out.astype(BF16), norm2_g[None, :], w_pq.astype(BF16), sk,
                                        tm_d, 256)
    tm_e = min(1024, x2.shape[0])
    hw, ids = _peer_u(xn, u_tab.astype(BF16), ids_i, ids_j, gate, tm_e, 512)
    n = x2.shape[0]
    out = _peer_v(ids, hw, x1.reshape(n, 2 * V_ROW_WORDS, LANES), _pack_v(v_tab), min(64, n))
    return out.reshape(n, D_MODEL)


def kernel(x, norm1_g, w_in, b_in, conv_w, m_norm_g, qn_g, kn_g, w_m_out, w_f_out, w_out,
           norm2_g, w_pq, sub_keys, u_tab, v_tab):
    batch, seq, d = x.shape
    x2 = x.reshape(batch * seq, d)
    for l in range(w_in.shape[0]):
        x2 = _layer(x2, batch, seq, norm1_g[l], w_in[l], b_in[l], conv_w[l], m_norm_g[l], qn_g[l],
                    kn_g[l], w_m_out[l], w_f_out[l], w_out[l], norm2_g[l], w_pq[l], sub_keys[l],
                    u_tab[l], v_tab[l])
    return x2.reshape(batch, seq, d)
```

```python
import functools

import jax
import jax.numpy as jnp
from jax import lax
from jax.experimental import pallas as pl
from jax.experimental.pallas import tpu as pltpu

F32 = jnp.float32
BF16 = jnp.bfloat16
I32 = jnp.int32

D_MODEL = 1024
M_HEADS = 4
M_HEAD_DIM = 256
F_HEADS = 8
F_HEAD_DIM = 128
CONV_WIDTH = 4
P_HEADS = 8
P_KEYS = 128
P_TOPK = 16
P_HALF = 128
N_EXPERTS = P_KEYS * P_KEYS
EPS = 1e-6
M_INIT = -1e30
LANES = 128
SUBLANES = 8
VMEM_LIMIT = 56 * 1024 * 1024

_SIZES = (1024, 1024, 1024, 1024, M_HEADS, M_HEADS, 1024, 1024, 1024, F_HEADS, 1024, 1024)
_OFFS = [0]
for _s in _SIZES:
    _OFFS.append(_OFFS[-1] + _s)
Z_COLS = 9 * 1024
ZB_MQ, ZB_MK, ZB_MV, ZB_MO, ZB_FQ, ZB_FK, ZB_FV, ZB_GM, ZB_GF = range(9)
G_MI, G_MF, G_FF = 0, M_HEADS, 2 * M_HEADS


def _cparams(sem):
    return pltpu.CompilerParams(dimension_semantics=sem, vmem_limit_bytes=VMEM_LIMIT)


def _log_sigmoid(x):
    return jnp.minimum(x, 0.0) - jnp.log1p(jnp.exp(-jnp.abs(x)))


def _sigmoid(x):
    return 1.0 / (1.0 + jnp.exp(-x))


def _inproj_kernel(x_ref, g_ref, w_ref, b_ref, ws_ref, wst_ref, bs_ref, bst_ref,
                   z_ref, gc_ref, gr_ref, h_sc):
    j = pl.program_id(1)

    @pl.when(j == 0)
    def _():
        x = x_ref[...]
        r = lax.rsqrt(jnp.mean(x * x, axis=-1, keepdims=True) + EPS)
        h = ((x * r) * g_ref[...]).astype(BF16)
        h_sc[...] = h
        gc_ref[...] = jnp.dot(h, ws_ref[...], preferred_element_type=F32) + bs_ref[...]
        gr_ref[...] = lax.dot_general(wst_ref[...], h, (((1,), (1,)), ((), ())),
                                      preferred_element_type=F32) + bst_ref[...]

    z = jnp.dot(h_sc[...], w_ref[...], preferred_element_type=F32) + b_ref[...]
    z_ref[...] = z.astype(BF16)


def _inproj(x2, g1, w_main, b_main, w_small, w_small_t, b_small, b_small_t, tm):
    n = x2.shape[0]
    tn = 1024
    grid = (n // tm, Z_COLS // tn)
    return pl.pallas_call(
        _inproj_kernel,
        grid=grid,
        in_specs=[
            pl.BlockSpec((tm, D_MODEL), lambda i, j: (i, 0)),
            pl.BlockSpec((1, D_MODEL), lambda i, j: (0, 0)),
            pl.BlockSpec((D_MODEL, tn), lambda i, j: (0, j)),
            pl.BlockSpec((1, tn), lambda i, j: (0, j)),
            pl.BlockSpec((D_MODEL, LANES), lambda i, j: (0, 0)),
            pl.BlockSpec((LANES, D_MODEL), lambda i, j: (0, 0)),
            pl.BlockSpec((1, LANES), lambda i, j: (0, 0)),
            pl.BlockSpec((LANES, 1), lambda i, j: (0, 0)),
        ],
        out_specs=[
            pl.BlockSpec((tm, tn), lambda i, j: (i, j)),
            pl.BlockSpec((tm, LANES), lambda i, j: (i, 0)),
            pl.BlockSpec((LANES, tm), lambda i, j: (0, i)),
        ],
        out_shape=[
            jax.ShapeDtypeStruct((n, Z_COLS), BF16),
            jax.ShapeDtypeStruct((n, LANES), F32),
            jax.ShapeDtypeStruct((LANES, n), F32),
        ],
        scratch_shapes=[pltpu.VMEM((tm, D_MODEL), BF16)],
        compiler_params=_cparams(("parallel", "arbitrary")),
        name="inproj",
    )(x2, g1, w_main, b_main, w_small, w_small_t, b_small, b_small_t)


def _cumsum_kernel(gr_ref, ccol_ref):
    x = _log_sigmoid(gr_ref[...])
    t = x.shape[1]
    lane = lax.broadcasted_iota(I32, x.shape, 1)
    s = 1
    while s < t:
        x = x + jnp.where(lane >= s, pltpu.roll(x, s, axis=1), 0.0)
        s *= 2
    ccol_ref[...] = x.T


def _forget_cumsum(gates_row, batch, seq):
    n = batch * seq
    return pl.pallas_call(
        _cumsum_kernel,
        grid=(batch,),
        in_specs=[pl.BlockSpec((LANES, seq), lambda b: (0, b))],
        out_specs=pl.BlockSpec((seq, LANES), lambda b: (b, 0)),
        out_shape=jax.ShapeDtypeStruct((n, LANES), F32),
        compiler_params=_cparams(("parallel",)),
        name="forget_cumsum",
    )(gates_row)


def _mlstm_kernel(q_ref, k_ref, v_ref, o_ref, gc_ref, gr_ref, cw_ref, mg_ref, out_ref,
                  c_sc, n_sc, m_sc, ext_sc, *, chunk):
    L = chunk
    dh = M_HEAD_DIM
    t = pl.program_id(1)

    @pl.when(t == 0)
    def _():
        c_sc[...] = jnp.zeros_like(c_sc)
        n_sc[...] = jnp.zeros_like(n_sc)
        m_sc[...] = jnp.full_like(m_sc, M_INIT)
        ext_sc[0:SUBLANES, :] = jnp.zeros((SUBLANES, 2 * D_MODEL), F32)

    ext_sc[SUBLANES:SUBLANES + L, 0:D_MODEL] = q_ref[...].astype(F32)
    ext_sc[SUBLANES:SUBLANES + L, D_MODEL:2 * D_MODEL] = k_ref[...].astype(F32)

    gc = gc_ref[...]
    gr = gr_ref[...]
    row_i = lax.broadcasted_iota(I32, (L, L), 0)
    col_i = lax.broadcasted_iota(I32, (L, L), 1)
    tril = col_i <= row_i

    def conv_silu(c0):
        base = SUBLANES - (CONV_WIDTH - 1)
        acc = ext_sc[base:base + L, c0:c0 + dh] * cw_ref[0:1, c0:c0 + dh]
        for j in range(1, CONV_WIDTH):
            acc = acc + ext_sc[base + j:base + j + L, c0:c0 + dh] * cw_ref[j:j + 1, c0:c0 + dh]
        return acc * _sigmoid(acc)

    for h in range(M_HEADS):
        q = conv_silu(h * dh) * (dh ** -0.5)
        k = conv_silu(D_MODEL + h * dh)
        v = v_ref[:, h * dh:(h + 1) * dh]
        qb = q.astype(BF16)
        kb = k.astype(BF16)

        i_col = gc[:, G_MI + h:G_MI + h + 1]
        lf_col = _log_sigmoid(gc[:, G_MF + h:G_MF + h + 1])
        i_row = gr[G_MI + h:G_MI + h + 1, :]
        lf_row = _log_sigmoid(gr[G_MF + h:G_MF + h + 1, :])

        b_col = jnp.sum(jnp.where(tril, lf_row, 0.0), axis=1, keepdims=True)
        b_row = jnp.sum(jnp.where(row_i <= col_i, lf_col, 0.0), axis=0, keepdims=True)
        g = jnp.sum(lf_row, axis=1, keepdims=True)
        m_prev = m_sc[h][0:1, 0:1]

        dmat = jnp.where(tril, b_col - b_row + i_row, -jnp.inf)
        inter = b_col + m_prev
        m_t = jnp.maximum(jnp.max(dmat, axis=1, keepdims=True), inter)
        p = jnp.exp(dmat - m_t)
        s = lax.dot_general(qb, kb, (((1,), (1,)), ((), ())), preferred_element_type=F32) * p
        w_inter = jnp.exp(inter - m_t)
        c_old = c_sc[h]
        n_old = n_sc[h]
        num = (jnp.dot(s.astype(BF16), v, preferred_element_type=F32)
               + w_inter * jnp.dot(qb, c_old.astype(BF16), preferred_element_type=F32))
        den = (jnp.sum(s, axis=1, keepdims=True)
               + w_inter * jnp.sum(q * n_old, axis=1, keepdims=True))
        hh = num / jnp.maximum(jnp.abs(den), jnp.exp(-m_t))

        log_w = g - b_col + i_col
        m_new = jnp.maximum(g + m_prev, jnp.max(log_w, axis=0, keepdims=True))
        w_s = jnp.exp(log_w - m_new)
        decay = jnp.exp(g + m_prev - m_new)
        wv = (w_s * v.astype(F32)).astype(BF16)
        c_sc[h] = decay * c_old + lax.dot_general(kb, wv, (((0,), (0,)), ((), ())),
                                                  preferred_element_type=F32)
        n_sc[h] = decay * n_old + jnp.sum(w_s * k, axis=0, keepdims=True)
        m_sc[h] = jnp.broadcast_to(m_new, (SUBLANES, LANES))

        r = lax.rsqrt(jnp.mean(hh * hh, axis=-1, keepdims=True) + EPS)
        hn = (hh * r) * mg_ref[0:1, h * dh:(h + 1) * dh]
        og = _sigmoid(o_ref[:, h * dh:(h + 1) * dh].astype(F32))
        out_ref[:, h * dh:(h + 1) * dh] = (hn * og).astype(BF16)

    ext_sc[0:SUBLANES, :] = ext_sc[L:L + SUBLANES, :]


def _mlstm(z, gates_col, gates_row, conv_w, m_norm_g, batch, seq, chunk):
    n = batch * seq
    nt = seq // chunk
    zspec = lambda cb: pl.BlockSpec((chunk, D_MODEL), lambda b, t, cb=cb: (b * nt + t, cb))
    return pl.pallas_call(
        functools.partial(_mlstm_kernel, chunk=chunk),
        grid=(batch, nt),
        in_specs=[
            zspec(ZB_MQ), zspec(ZB_MK), zspec(ZB_MV), zspec(ZB_MO),
            pl.BlockSpec((chunk, LANES), lambda b, t: (b * nt + t, 0)),
            pl.BlockSpec((LANES, chunk), lambda b, t: (0, b * nt + t)),
            pl.BlockSpec((CONV_WIDTH, 2 * D_MODEL), lambda b, t: (0, 0)),
            pl.BlockSpec((1, D_MODEL), lambda b, t: (0, 0)),
        ],
        out_specs=pl.BlockSpec((chunk, D_MODEL), lambda b, t: (b * nt + t, 0)),
        out_shape=jax.ShapeDtypeStruct((n, D_MODEL), BF16),
        scratch_shapes=[
            pltpu.VMEM((M_HEADS, M_HEAD_DIM, M_HEAD_DIM), F32),
            pltpu.VMEM((M_HEADS, 1, M_HEAD_DIM), F32),
            pltpu.VMEM((M_HEADS, SUBLANES, LANES), F32),
            pltpu.VMEM((chunk + SUBLANES, 2 * D_MODEL), F32),
        ],
        compiler_params=_cparams(("parallel", "arbitrary")),
        name="mlstm",
    )(z, z, z, z, gates_col, gates_row, conv_w, m_norm_g)


LOG2E = 1.4426950408889634
FOX_SAFE_LOG2_RANGE = 100.0


def _split3(c):
    hi = c.astype(BF16).astype(F32)
    r1 = c - hi
    mid = r1.astype(BF16).astype(F32)
    lo = (r1 - mid).astype(BF16).astype(F32)
    return hi, mid, lo


def _fox_aug(c, lane, is_query, shift):
    hi, mid, lo = _split3(c if is_query else -c)
    off = 0 if is_query else 3
    ex = jnp.where(lane == off, hi, jnp.where(lane == off + 1, mid, jnp.where(lane == off + 2, lo, 0.0)))
    ones_at = (lane >= 3) & (lane < 6) if is_query else lane < 3
    ex = jnp.where(ones_at, 1.0, ex)
    return jnp.where(lane == 6, -shift if is_query else 1.0, ex).astype(BF16)


def _fox_kernel(bnd_ref, q_ref, k_ref, v_ref, ccol_ref, qg_ref, kg_ref, out_ref,
                ka_sc, va_sc, m_sc, acc_sc, *, tq, seq, hp):
    g = pl.program_id(1)
    i = pl.program_id(2)
    dh = F_HEAD_DIM
    lane = lax.broadcasted_iota(I32, (tq, LANES), 1)
    shift = bnd_ref[0]
    safe = bnd_ref[1] > 0.5

    def c_of(r0, hh):
        cc = ccol_ref[pl.ds(r0, tq), :]
        return jnp.sum(jnp.where(lane == G_FF + g * hp + hh, cc, 0.0), axis=1, keepdims=True) * LOG2E

    @pl.when(i == 0)
    def _():
        def prep_chunk(c, carry):
            r0 = pl.multiple_of(c * tq, tq)
            for hh in range(hp):
                kk = k_ref[pl.ds(r0, tq), hh * dh:(hh + 1) * dh].astype(F32)
                r = lax.rsqrt(jnp.mean(kk * kk, axis=-1, keepdims=True) + EPS)
                ka_sc[hh, pl.ds(r0, tq), 0:dh] = ((kk * r) * kg_ref[...]).astype(BF16)
                ka_sc[hh, pl.ds(r0, tq), dh:2 * dh] = _fox_aug(c_of(r0, hh), lane, False, shift)
                va_sc[hh, pl.ds(r0, tq), 0:dh] = v_ref[pl.ds(r0, tq), hh * dh:(hh + 1) * dh]
                va_sc[hh, pl.ds(r0, tq), dh:2 * dh] = jnp.ones((tq, dh), BF16)
            return carry
        lax.fori_loop(0, seq // tq, prep_chunk, 0)

    q0 = pl.multiple_of(i * tq, tq)
    qa = []
    for hh in range(hp):
        q = q_ref[:, hh * dh:(hh + 1) * dh].astype(F32)
        r = lax.rsqrt(jnp.mean(q * q, axis=-1, keepdims=True) + EPS)
        qn = (((q * r) * qg_ref[...]) * (dh ** -0.5 * LOG2E)).astype(BF16)
        qa.append(jnp.concatenate([qn, _fox_aug(c_of(q0, hh), lane, True, shift)], axis=1))

    acc_sc[...] = jnp.zeros_like(acc_sc)

    def scores(j, masked):
        r0 = pl.multiple_of(j * tq, tq)
        s = [lax.dot_general(qa[hh], ka_sc[hh, pl.ds(r0, tq), :], (((1,), (1,)), ((), ())),
                             preferred_element_type=F32) for hh in range(hp)]
        if masked:
            rr = lax.broadcasted_iota(I32, (tq, tq), 0)
            cc = lax.broadcasted_iota(I32, (tq, tq), 1)
            s = [jnp.where(rr >= cc, sh, -jnp.inf) for sh in s]
        return r0, s

    def fixed_step(j, masked):
        r0, s = scores(j, masked)
        for hh in range(hp):
            p = jnp.exp2(s[hh]).astype(BF16)
            acc_sc[hh] += jnp.dot(p, va_sc[hh, pl.ds(r0, tq), :], preferred_element_type=F32)

    def running_step(j, masked):
        r0, s = scores(j, masked)
        for hh in range(hp):
            m_old = m_sc[hh]
            m_new = jnp.maximum(m_old, jnp.max(s[hh], axis=1, keepdims=True))
            p = jnp.exp2(s[hh] - m_new).astype(BF16)
            alpha = jnp.exp2(m_old - m_new)
            acc_sc[hh] = alpha * acc_sc[hh] + jnp.dot(p, va_sc[hh, pl.ds(r0, tq), :],
                                                      preferred_element_type=F32)
            m_sc[hh] = m_new

    def run(step):
        def body(j, carry):
            step(j, False)
            return carry
        lax.fori_loop(0, i, body, 0)
        step(i, True)

    @pl.when(safe)
    def _():
        run(fixed_step)

    @pl.when(jnp.logical_not(safe))
    def _():
        m_sc[...] = jnp.full_like(m_sc, -jnp.inf)
        run(running_step)

    for hh in range(hp):
        out_ref[:, hh * dh:(hh + 1) * dh] = (acc_sc[hh, :, 0:dh] / acc_sc[hh, :, dh:2 * dh]).astype(BF16)


def _fox(z, ccol, qn_g, kn_g, batch, seq, tq, hp):
    n = batch * seq
    nq = seq // tq
    w = hp * F_HEAD_DIM
    fq0 = ZB_FQ * (D_MODEL // w)
    fk0 = ZB_FK * (D_MODEL // w)
    fv0 = ZB_FV * (D_MODEL // w)
    shift = jnp.ceil(1.01 * LOG2E * F_HEAD_DIM ** 0.5 * jnp.max(jnp.abs(qn_g)) * jnp.max(jnp.abs(kn_g)))
    safe = (2.0 * shift <= FOX_SAFE_LOG2_RANGE).astype(F32)
    bnd = jnp.stack([shift, safe]).astype(F32)
    return pl.pallas_call(
        functools.partial(_fox_kernel, tq=tq, seq=seq, hp=hp),
        grid=(batch, F_HEADS // hp, nq),
        in_specs=[
            pl.BlockSpec(memory_space=pltpu.SMEM),
            pl.BlockSpec((tq, w), lambda b, g, i: (b * nq + i, fq0 + g)),
            pl.BlockSpec((seq, w), lambda b, g, i: (b, fk0 + g)),
            pl.BlockSpec((seq, w), lambda b, g, i: (b, fv0 + g)),
            pl.BlockSpec((seq, LANES), lambda b, g, i: (b, 0)),
            pl.BlockSpec((1, F_HEAD_DIM), lambda b, g, i: (0, 0)),
            pl.BlockSpec((1, F_HEAD_DIM), lambda b, g, i: (0, 0)),
        ],
        out_specs=pl.BlockSpec((tq, w), lambda b, g, i: (b * nq + i, g)),
        out_shape=jax.ShapeDtypeStruct((n, D_MODEL), BF16),
        scratch_shapes=[
            pltpu.VMEM((hp, seq, 2 * F_HEAD_DIM), BF16),
            pltpu.VMEM((hp, seq, 2 * F_HEAD_DIM), BF16),
            pltpu.VMEM((hp, tq, 1), F32),
            pltpu.VMEM((hp, tq, 2 * F_HEAD_DIM), F32),
        ],
        compiler_params=_cparams(("parallel", "parallel", "arbitrary")),
        name="fox",
    )(bnd, z, z, z, ccol, qn_g, kn_g)


def _topk_rows(s, k, store, ids=None):
    if ids is None:
        ids = lax.broadcasted_iota(I32, s.shape, 0)
    big = jnp.int32(2 ** 30)
    for r in range(k):
        mx = jnp.max(s, axis=0, keepdims=True)
        ix = jnp.min(jnp.where(s == mx, ids, big), axis=0, keepdims=True)
        s = jnp.where(ids == ix, -jnp.inf, s)
        store(r, mx, ix)


def _pair_candidates(v0, v1):
    k = P_TOPK
    w = v0.shape[1]
    row8 = lax.broadcasted_iota(I32, (SUBLANES, w), 0)
    sums = [v0[0:1, :] + v1]
    ids = [lax.broadcasted_iota(I32, (k, w), 0)]
    for a in range(1, SUBLANES):
        nb = k // (a + 1)
        sums.append(jnp.where(row8 < nb, v0[a:a + 1, :] + v1[0:SUBLANES, :], -jnp.inf))
        ids.append(row8 + a * k)
    sums.append(v0[SUBLANES:k, :] + v1[0:1, :])
    ids.append((row8 + SUBLANES) * k)
    return jnp.concatenate(sums, axis=0), jnp.concatenate(ids, axis=0)


def _merge_kernel(hm_ref, hf_ref, gm_ref, gf_ref, x_ref, wm_ref, wf_ref, wo_ref, g2_ref,
                  wpq_ref, sk_ref, x1_ref, xn_ref, ids_i_ref, ids_j_ref, gate_ref,
                  sc_sc, val_sc, idx_sc, top_sc, pos_sc, oi_sc, oj_sc, og_sc, *, tm, tc):
    ym = jnp.dot(hm_ref[...], wm_ref[...], preferred_element_type=F32)
    yf = jnp.dot(hf_ref[...], wf_ref[...], preferred_element_type=F32)
    y = (_sigmoid(gm_ref[...].astype(F32)) * ym + _sigmoid(gf_ref[...].astype(F32)) * yf)
    x1 = x_ref[...] + jnp.dot(y.astype(BF16), wo_ref[...], preferred_element_type=F32)
    x1_ref[...] = x1
    r = lax.rsqrt(jnp.mean(x1 * x1, axis=-1, keepdims=True) + EPS)
    xn = ((x1 * r) * g2_ref[...]).astype(BF16)
    xn_ref[...] = xn
    qh = jnp.dot(xn, wpq_ref[...], preferred_element_type=F32).astype(BF16)

    for hc in range(2 * P_HEADS):
        sc_sc[hc] = lax.dot_general(sk_ref[hc], qh[:, hc * P_HALF:(hc + 1) * P_HALF],
                                    (((1,), (1,)), ((), ())), preferred_element_type=F32)

    def chunk_body(c, carry):
        c0 = pl.multiple_of(c * tc, tc)

        def first_level(hc, carry2):
            s = sc_sc[hc, :, pl.ds(c0, tc)]

            def store(r, val, idx):
                val_sc[hc, r:r + 1, :] = val
                idx_sc[hc, r:r + 1, :] = idx
            _topk_rows(s, P_TOPK, store)
            return carry2
        lax.fori_loop(0, 2 * P_HEADS, first_level, 0)

        def second_level(hd, carry2):
            v0 = val_sc[2 * hd]
            v1 = val_sc[2 * hd + 1]
            i0 = idx_sc[2 * hd]
            i1 = idx_sc[2 * hd + 1]
            cand, cand_id = _pair_candidates(v0, v1)

            def store(r, val, idx):
                top_sc[r:r + 1, :] = val
                pos_sc[r:r + 1, :] = idx
            _topk_rows(cand, P_TOPK, store, cand_id)
            top = top_sc[...]
            pos = pos_sc[...]
            pa = pos >> 4
            pb = pos & (P_TOPK - 1)
            sel_i = jnp.zeros(pos.shape, I32)
            sel_j = jnp.zeros(pos.shape, I32)
            for a in range(P_TOPK):
                sel_i = jnp.where(pa == a, i0[a:a + 1, :], sel_i)
                sel_j = jnp.where(pb == a, i1[a:a + 1, :], sel_j)
            e = jnp.exp(top - top[0:1, :])
            gate = e / jnp.sum(e, axis=0, keepdims=True)
            r0 = pl.multiple_of(hd * P_TOPK, P_TOPK)
            oi_sc[pl.ds(r0, P_TOPK), :] = sel_i
            oj_sc[pl.ds(r0, P_TOPK), :] = sel_j
            og_sc[pl.ds(r0, P_TOPK), :] = gate
            return carry2
        lax.fori_loop(0, P_HEADS, second_level, 0)

        ids_i_ref[pl.ds(c0, tc), :] = oi_sc[...].T
        ids_j_ref[pl.ds(c0, tc), :] = oj_sc[...].T
        gate_ref[pl.ds(c0, tc), :] = og_sc[...].T
        return carry
    lax.fori_loop(0, tm // tc, chunk_body, 0)


def _merge(hm, hf, z, x2, w_m, w_f, w_o, g2, w_pq, sk, tm, tc):
    n = x2.shape[0]
    hk = P_HEADS * P_TOPK
    row = lambda i: (i, 0)
    const = lambda i: (0, 0)
    return pl.pallas_call(
        functools.partial(_merge_kernel, tm=tm, tc=tc),
        grid=(n // tm,),
        in_specs=[
            pl.BlockSpec((tm, D_MODEL), row),
            pl.BlockSpec((tm, D_MODEL), row),
            pl.BlockSpec((tm, D_MODEL), lambda i: (i, ZB_GM)),
            pl.BlockSpec((tm, D_MODEL), lambda i: (i, ZB_GF)),
            pl.BlockSpec((tm, D_MODEL), row),
            pl.BlockSpec((D_MODEL, D_MODEL), const),
            pl.BlockSpec((D_MODEL, D_MODEL), const),
            pl.BlockSpec((D_MODEL, D_MODEL), const),
            pl.BlockSpec((1, D_MODEL), const),
            pl.BlockSpec((D_MODEL, 2 * P_HEADS * P_HALF), const),
            pl.BlockSpec((2 * P_HEADS, P_KEYS, P_HALF), lambda i: (0, 0, 0)),
        ],
        out_specs=[
            pl.BlockSpec((tm, D_MODEL), row),
            pl.BlockSpec((tm, D_MODEL), row),
            pl.BlockSpec((tm, hk), row),
            pl.BlockSpec((tm, hk), row),
            pl.BlockSpec((tm, hk), row),
        ],
        out_shape=[
            jax.ShapeDtypeStruct((n, D_MODEL), F32),
            jax.ShapeDtypeStruct((n, D_MODEL), BF16),
            jax.ShapeDtypeStruct((n, hk), I32),
            jax.ShapeDtypeStruct((n, hk), I32),
            jax.ShapeDtypeStruct((n, hk), F32),
        ],
        scratch_shapes=[
            pltpu.VMEM((2 * P_HEADS, P_KEYS, tm), F32),
            pltpu.VMEM((2 * P_HEADS, P_TOPK, tc), F32),
            pltpu.VMEM((2 * P_HEADS, P_TOPK, tc), I32),
            pltpu.VMEM((P_TOPK, tc), F32),
            pltpu.VMEM((P_TOPK, tc), I32),
            pltpu.VMEM((hk, tc), I32),
            pltpu.VMEM((hk, tc), I32),
            pltpu.VMEM((hk, tc), F32),
        ],
        compiler_params=_cparams(("parallel",)),
        name="merge_retrieve",
    )(hm, hf, z, z, x2, w_m, w_f, w_o, g2, w_pq, sk)


U_SUB = 512


def _peer_u_kernel(xn_ref, u_ref, ii_ref, jj_ref, gate_ref, hw_ref, ids_ref, acc_sc, *, eb):
    e = pl.program_id(1)
    ne = pl.num_programs(1)

    @pl.when(e == 0)
    def _():
        acc_sc[...] = jnp.zeros_like(acc_sc)

    xn = xn_ref[...]
    ii = ii_ref[...]
    jj = jj_ref[...]
    acc = acc_sc[...]
    for sb in range(eb // U_SUB):
        a = lax.dot_general(xn, u_ref[sb * U_SUB:(sb + 1) * U_SUB, :], (((1,), (1,)), ((), ())),
                            preferred_element_type=F32)
        for c in range(U_SUB // P_KEYS):
            blk = e * (eb // P_KEYS) + sb * (U_SUB // P_KEYS) + c
            g = jnp.take_along_axis(a[:, c * P_KEYS:(c + 1) * P_KEYS], jj, axis=1,
                                    mode="promise_in_bounds")
            acc = jnp.where(ii == blk, g, acc)
    acc_sc[...] = acc

    @pl.when(e == ne - 1)
    def _():
        av = acc_sc[...]
        gelu = 0.5 * av * (1.0 + lax.erf(av * (2.0 ** -0.5)))
        hw_ref[...] = gate_ref[...] * gelu
        ids_ref[...] = (ii * P_KEYS + jj) * (D_MODEL // (2 * LANES))


def _peer_u(xn, u_bf, ids_i, ids_j, gate, tm, eb):
    n = xn.shape[0]
    hk = P_HEADS * P_TOPK
    row = lambda i, e: (i, 0)
    return pl.pallas_call(
        functools.partial(_peer_u_kernel, eb=eb),
        grid=(n // tm, N_EXPERTS // eb),
        in_specs=[
            pl.BlockSpec((tm, D_MODEL), row),
            pl.BlockSpec((eb, D_MODEL), lambda i, e: (e, 0)),
            pl.BlockSpec((tm, hk), row),
            pl.BlockSpec((tm, hk), row),
            pl.BlockSpec((tm, hk), row),
        ],
        out_specs=[pl.BlockSpec((tm, hk), row), pl.BlockSpec((tm, hk), row)],
        out_shape=[jax.ShapeDtypeStruct((n, hk), F32), jax.ShapeDtypeStruct((n, hk), I32)],
        scratch_shapes=[pltpu.VMEM((tm, hk), F32)],
        compiler_params=_cparams(("parallel", "arbitrary")),
        name="peer_u",
    )(xn, u_bf, ids_i, ids_j, gate)


V_ROW_WORDS = D_MODEL // (2 * LANES)


def _peer_v_kernel(ids_ref, w_ref, x1_ref, tab_ref, out_ref, *, tt):
    hk = P_HEADS * P_TOPK

    def token(t, carry):
        acc_lo = [jnp.zeros((V_ROW_WORDS, LANES), F32) for _ in range(2)]
        acc_hi = [jnp.zeros((V_ROW_WORDS, LANES), F32) for _ in range(2)]
        for m in range(hk):
            r0 = pl.multiple_of(ids_ref[t, m], V_ROW_WORDS)
            w = w_ref[t, m]
            row = tab_ref[pl.ds(r0, V_ROW_WORDS), :]
            lo = lax.bitcast_convert_type(row << 16, F32)
            hi = lax.bitcast_convert_type(row & jnp.int32(-65536), F32)
            acc_lo[m % 2] = acc_lo[m % 2] + w * lo
            acc_hi[m % 2] = acc_hi[m % 2] + w * hi
        out_ref[t, 0:V_ROW_WORDS, :] = x1_ref[t, 0:V_ROW_WORDS, :] + (acc_lo[0] + acc_lo[1])
        out_ref[t, V_ROW_WORDS:2 * V_ROW_WORDS, :] = (x1_ref[t, V_ROW_WORDS:2 * V_ROW_WORDS, :]
                                                      + (acc_hi[0] + acc_hi[1]))
        return carry
    lax.fori_loop(0, tt, token, 0)


def _peer_v(ids, hw, x1_3d, v_packed, tt):
    n = ids.shape[0]
    hk = P_HEADS * P_TOPK
    return pl.pallas_call(
        functools.partial(_peer_v_kernel, tt=tt),
        grid=(n // tt,),
        in_specs=[
            pl.BlockSpec((tt, hk), lambda i: (i, 0), memory_space=pltpu.SMEM),
            pl.BlockSpec((tt, hk), lambda i: (i, 0), memory_space=pltpu.SMEM),
            pl.BlockSpec((tt, 2 * V_ROW_WORDS, LANES), lambda i: (i, 0, 0)),
            pl.BlockSpec((N_EXPERTS * V_ROW_WORDS, LANES), lambda i: (0, 0),
                         pipeline_mode=pl.Buffered(1)),
        ],
        out_specs=pl.BlockSpec((tt, 2 * V_ROW_WORDS, LANES), lambda i: (i, 0, 0)),
        out_shape=jax.ShapeDtypeStruct((n, 2 * V_ROW_WORDS, LANES), F32),
        compiler_params=_cparams(("arbitrary",)),
        name="peer_v",
    )(ids, hw, x1_3d, v_packed)


def _pack_v(v_tab):
    vb = lax.bitcast_convert_type(v_tab.astype(BF16), jnp.uint16).astype(jnp.uint32)
    half = D_MODEL // 2
    packed = vb[:, :half] | (vb[:, half:] << 16)
    return lax.bitcast_convert_type(packed, I32).reshape(N_EXPERTS * V_ROW_WORDS, LANES)


def _layer(x2, batch, seq, norm1_g, w_in, b_in, conv_w, m_norm_g, qn_g, kn_g, w_m_out, w_f_out,
           w_out, norm2_g, w_pq, sub_keys, u_tab, v_tab):
    o = _OFFS
    cols = lambda a: jnp.concatenate([a[..., o[0]:o[4]], a[..., o[6]:o[9]], a[..., o[10]:o[12]]], axis=-1)
    small = lambda a: jnp.concatenate([a[..., o[4]:o[6]], a[..., o[9]:o[10]]], axis=-1)
    w_main = cols(w_in).astype(BF16)
    b_main = cols(b_in)[None, :]
    ws = small(w_in)
    bs = small(b_in)
    pad = LANES - ws.shape[1]
    ws = jnp.pad(ws, ((0, 0), (0, pad))).astype(BF16)
    bs = jnp.pad(bs, (0, pad))

    tm_a = min(1024, x2.shape[0])
    z, gates_col, gates_row = _inproj(x2, norm1_g[None, :], w_main, b_main, ws, ws.T,
                                      bs[None, :], bs[:, None], tm_a)
    ccol = _forget_cumsum(gates_row, batch, seq)
    chunk = min(256, seq)
    hm = _mlstm(z, gates_col, gates_row, conv_w, m_norm_g[None, :], batch, seq, chunk)
    tq = min(512, seq)
    hf = _fox(z, ccol, qn_g[None, :], kn_g[None, :], batch, seq, tq, 2)

    sk = sub_keys.reshape(2 * P_HEADS, P_KEYS, P_HALF).astype(BF16)
    tm_d = min(512, x2.shape[0])
    x1, xn, ids_i, ids_j, gate = _merge(hm, hf, z, x2, w_m_out.astype(BF16), w_f_out.astype(BF16),
                                        w_out.astype(BF16), norm2_g[None, :], w_pq.astype(BF16), sk,
                                        tm_d, 256)
    tm_e = min(1024, x2.shape[0])
    hw, ids = _peer_u(xn, u_tab.astype(BF16), ids_i, ids_j, gate, tm_e, 2048)
    n = x2.shape[0]
    out = _peer_v(ids, hw, x1.reshape(n, 2 * V_ROW_WORDS, LANES), _pack_v(v_tab), min(64, n))
    return out.reshape(n, D_MODEL)


def kernel(x, norm1_g, w_in, b_in, conv_w, m_norm_g, qn_g, kn_g, w_m_out, w_f_out, w_out,
           norm2_g, w_pq, sub_keys, u_tab, v_tab):
    batch, seq, d = x.shape
    x2 = x.reshape(batch * seq, d)
    for l in range(w_in.shape[0]):
        x2 = _layer(x2, batch, seq, norm1_g[l], w_in[l], b_in[l], conv_w[l], m_norm_g[l], qn_g[l],
                    kn_g[l], w_m_out[l], w_f_out[l], w_out[l], norm2_g[l], w_pq[l], sub_keys[l],
                    u_tab[l], v_tab[l])
    return x2.reshape(batch, seq, d)
```

```python
import functools

import jax
import jax.numpy as jnp
from jax import lax
from jax.experimental import pallas as pl
from jax.experimental.pallas import tpu as pltpu

F32 = jnp.float32
BF16 = jnp.bfloat16
I32 = jnp.int32

D_MODEL = 1024
M_HEADS = 4
M_HEAD_DIM = 256
F_HEADS = 8
F_HEAD_DIM = 128
CONV_WIDTH = 4
P_HEADS = 8
P_KEYS = 128
P_TOPK = 16
P_HALF = 128
N_EXPERTS = P_KEYS * P_KEYS
EPS = 1e-6
M_INIT = -1e30
LANES = 128
SUBLANES = 8
VMEM_LIMIT = 56 * 1024 * 1024

_SIZES = (1024, 1024, 1024, 1024, M_HEADS, M_HEADS, 1024, 1024, 1024, F_HEADS, 1024, 1024)
_OFFS = [0]
for _s in _SIZES:
    _OFFS.append(_OFFS[-1] + _s)
Z_COLS = 9 * 1024
ZB_MQ, ZB_MK, ZB_MV, ZB_MO, ZB_FQ, ZB_FK, ZB_FV, ZB_GM, ZB_GF = range(9)
G_MI, G_MF, G_FF = 0, M_HEADS, 2 * M_HEADS


def _cparams(sem):
    return pltpu.CompilerParams(dimension_semantics=sem, vmem_limit_bytes=VMEM_LIMIT)


def _log_sigmoid(x):
    return jnp.minimum(x, 0.0) - jnp.log1p(jnp.exp(-jnp.abs(x)))


def _sigmoid(x):
    return 1.0 / (1.0 + jnp.exp(-x))


def _inproj_kernel(x_ref, g_ref, w_ref, b_ref, ws_ref, wst_ref, bs_ref, bst_ref,
                   z_ref, gc_ref, gr_ref, h_sc):
    j = pl.program_id(1)

    @pl.when(j == 0)
    def _():
        x = x_ref[...]
        r = lax.rsqrt(jnp.mean(x * x, axis=-1, keepdims=True) + EPS)
        h = ((x * r) * g_ref[...]).astype(BF16)
        h_sc[...] = h
        gc_ref[...] = jnp.dot(h, ws_ref[...], preferred_element_type=F32) + bs_ref[...]
        gr_ref[...] = lax.dot_general(wst_ref[...], h, (((1,), (1,)), ((), ())),
                                      preferred_element_type=F32) + bst_ref[...]

    z = jnp.dot(h_sc[...], w_ref[...], preferred_element_type=F32) + b_ref[...]
    z_ref[...] = z.astype(BF16)


def _inproj(x2, g1, w_main, b_main, w_small, w_small_t, b_small, b_small_t, tm):
    n = x2.shape[0]
    tn = 1024
    grid = (n // tm, Z_COLS // tn)
    return pl.pallas_call(
        _inproj_kernel,
        grid=grid,
        in_specs=[
            pl.BlockSpec((tm, D_MODEL), lambda i, j: (i, 0)),
            pl.BlockSpec((1, D_MODEL), lambda i, j: (0, 0)),
            pl.BlockSpec((D_MODEL, tn), lambda i, j: (0, j)),
            pl.BlockSpec((1, tn), lambda i, j: (0, j)),
            pl.BlockSpec((D_MODEL, LANES), lambda i, j: (0, 0)),
            pl.BlockSpec((LANES, D_MODEL), lambda i, j: (0, 0)),
            pl.BlockSpec((1, LANES), lambda i, j: (0, 0)),
            pl.BlockSpec((LANES, 1), lambda i, j: (0, 0)),
        ],
        out_specs=[
            pl.BlockSpec((tm, tn), lambda i, j: (i, j)),
            pl.BlockSpec((tm, LANES), lambda i, j: (i, 0)),
            pl.BlockSpec((LANES, tm), lambda i, j: (0, i)),
        ],
        out_shape=[
            jax.ShapeDtypeStruct((n, Z_COLS), BF16),
            jax.ShapeDtypeStruct((n, LANES), F32),
            jax.ShapeDtypeStruct((LANES, n), F32),
        ],
        scratch_shapes=[pltpu.VMEM((tm, D_MODEL), BF16)],
        compiler_params=_cparams(("parallel", "arbitrary")),
        name="inproj",
    )(x2, g1, w_main, b_main, w_small, w_small_t, b_small, b_small_t)


def _cumsum_kernel(gr_ref, ccol_ref):
    x = _log_sigmoid(gr_ref[...])
    t = x.shape[1]
    lane = lax.broadcasted_iota(I32, x.shape, 1)
    s = 1
    while s < t:
        x = x + jnp.where(lane >= s, pltpu.roll(x, s, axis=1), 0.0)
        s *= 2
    ccol_ref[...] = x.T


def _forget_cumsum(gates_row, batch, seq):
    n = batch * seq
    return pl.pallas_call(
        _cumsum_kernel,
        grid=(batch,),
        in_specs=[pl.BlockSpec((LANES, seq), lambda b: (0, b))],
        out_specs=pl.BlockSpec((seq, LANES), lambda b: (b, 0)),
        out_shape=jax.ShapeDtypeStruct((n, LANES), F32),
        compiler_params=_cparams(("parallel",)),
        name="forget_cumsum",
    )(gates_row)


def _mlstm_kernel(q_ref, k_ref, v_ref, o_ref, gc_ref, gr_ref, cw_ref, mg_ref, out_ref,
                  c_sc, n_sc, m_sc, ext_sc, *, chunk):
    L = chunk
    dh = M_HEAD_DIM
    t = pl.program_id(1)

    @pl.when(t == 0)
    def _():
        c_sc[...] = jnp.zeros_like(c_sc)
        n_sc[...] = jnp.zeros_like(n_sc)
        m_sc[...] = jnp.full_like(m_sc, M_INIT)
        ext_sc[0:SUBLANES, :] = jnp.zeros((SUBLANES, 2 * D_MODEL), F32)

    ext_sc[SUBLANES:SUBLANES + L, 0:D_MODEL] = q_ref[...].astype(F32)
    ext_sc[SUBLANES:SUBLANES + L, D_MODEL:2 * D_MODEL] = k_ref[...].astype(F32)

    gc = gc_ref[...]
    gr = gr_ref[...]
    row_i = lax.broadcasted_iota(I32, (L, L), 0)
    col_i = lax.broadcasted_iota(I32, (L, L), 1)
    tril = col_i <= row_i

    def conv_silu(c0):
        base = SUBLANES - (CONV_WIDTH - 1)
        acc = ext_sc[base:base + L, c0:c0 + dh] * cw_ref[0:1, c0:c0 + dh]
        for j in range(1, CONV_WIDTH):
            acc = acc + ext_sc[base + j:base + j + L, c0:c0 + dh] * cw_ref[j:j + 1, c0:c0 + dh]
        return acc * _sigmoid(acc)

    for h in range(M_HEADS):
        q = conv_silu(h * dh) * (dh ** -0.5)
        k = conv_silu(D_MODEL + h * dh)
        v = v_ref[:, h * dh:(h + 1) * dh]
        qb = q.astype(BF16)
        kb = k.astype(BF16)

        i_col = gc[:, G_MI + h:G_MI + h + 1]
        lf_col = _log_sigmoid(gc[:, G_MF + h:G_MF + h + 1])
        i_row = gr[G_MI + h:G_MI + h + 1, :]
        lf_row = _log_sigmoid(gr[G_MF + h:G_MF + h + 1, :])

        b_col = jnp.sum(jnp.where(tril, lf_row, 0.0), axis=1, keepdims=True)
        b_row = jnp.sum(jnp.where(row_i <= col_i, lf_col, 0.0), axis=0, keepdims=True)
        g = jnp.sum(lf_row, axis=1, keepdims=True)
        m_prev = m_sc[h][0:1, 0:1]

        dmat = jnp.where(tril, b_col - b_row + i_row, -jnp.inf)
        inter = b_col + m_prev
        m_t = jnp.maximum(jnp.max(dmat, axis=1, keepdims=True), inter)
        p = jnp.exp(dmat - m_t)
        s = lax.dot_general(qb, kb, (((1,), (1,)), ((), ())), preferred_element_type=F32) * p
        w_inter = jnp.exp(inter - m_t)
        c_old = c_sc[h]
        n_old = n_sc[h]
        num = (jnp.dot(s.astype(BF16), v, preferred_element_type=F32)
               + w_inter * jnp.dot(qb, c_old.astype(BF16), preferred_element_type=F32))
        den = (jnp.sum(s, axis=1, keepdims=True)
               + w_inter * jnp.sum(q * n_old, axis=1, keepdims=True))
        hh = num / jnp.maximum(jnp.abs(den), jnp.exp(-m_t))

        log_w = g - b_col + i_col
        m_new = jnp.maximum(g + m_prev, jnp.max(log_w, axis=0, keepdims=True))
        w_s = jnp.exp(log_w - m_new)
        decay = jnp.exp(g + m_prev - m_new)
        wv = (w_s * v.astype(F32)).astype(BF16)
        c_sc[h] = decay * c_old + lax.dot_general(kb, wv, (((0,), (0,)), ((), ())),
                                                  preferred_element_type=F32)
        n_sc[h] = decay * n_old + jnp.sum(w_s * k, axis=0, keepdims=True)
        m_sc[h] = jnp.broadcast_to(m_new, (SUBLANES, LANES))

        r = lax.rsqrt(jnp.mean(hh * hh, axis=-1, keepdims=True) + EPS)
        hn = (hh * r) * mg_ref[0:1, h * dh:(h + 1) * dh]
        og = _sigmoid(o_ref[:, h * dh:(h + 1) * dh].astype(F32))
        out_ref[:, h * dh:(h + 1) * dh] = (hn * og).astype(BF16)

    ext_sc[0:SUBLANES, :] = ext_sc[L:L + SUBLANES, :]


def _mlstm(z, gates_col, gates_row, conv_w, m_norm_g, batch, seq, chunk):
    n = batch * seq
    nt = seq // chunk
    zspec = lambda cb: pl.BlockSpec((chunk, D_MODEL), lambda b, t, cb=cb: (b * nt + t, cb))
    return pl.pallas_call(
        functools.partial(_mlstm_kernel, chunk=chunk),
        grid=(batch, nt),
        in_specs=[
            zspec(ZB_MQ), zspec(ZB_MK), zspec(ZB_MV), zspec(ZB_MO),
            pl.BlockSpec((chunk, LANES), lambda b, t: (b * nt + t, 0)),
            pl.BlockSpec((LANES, chunk), lambda b, t: (0, b * nt + t)),
            pl.BlockSpec((CONV_WIDTH, 2 * D_MODEL), lambda b, t: (0, 0)),
            pl.BlockSpec((1, D_MODEL), lambda b, t: (0, 0)),
        ],
        out_specs=pl.BlockSpec((chunk, D_MODEL), lambda b, t: (b * nt + t, 0)),
        out_shape=jax.ShapeDtypeStruct((n, D_MODEL), BF16),
        scratch_shapes=[
            pltpu.VMEM((M_HEADS, M_HEAD_DIM, M_HEAD_DIM), F32),
            pltpu.VMEM((M_HEADS, 1, M_HEAD_DIM), F32),
            pltpu.VMEM((M_HEADS, SUBLANES, LANES), F32),
            pltpu.VMEM((chunk + SUBLANES, 2 * D_MODEL), F32),
        ],
        compiler_params=_cparams(("parallel", "arbitrary")),
        name="mlstm",
    )(z, z, z, z, gates_col, gates_row, conv_w, m_norm_g)


LOG2E = 1.4426950408889634
FOX_SAFE_LOG2_RANGE = 100.0


def _split3(c):
    hi = c.astype(BF16).astype(F32)
    r1 = c - hi
    mid = r1.astype(BF16).astype(F32)
    lo = (r1 - mid).astype(BF16).astype(F32)
    return hi, mid, lo


def _fox_aug(c, lane, is_query, shift):
    hi, mid, lo = _split3(c if is_query else -c)
    off = 0 if is_query else 3
    ex = jnp.where(lane == off, hi, jnp.where(lane == off + 1, mid, jnp.where(lane == off + 2, lo, 0.0)))
    ones_at = (lane >= 3) & (lane < 6) if is_query else lane < 3
    ex = jnp.where(ones_at, 1.0, ex)
    return jnp.where(lane == 6, -shift if is_query else 1.0, ex).astype(BF16)


def _fox_kernel(bnd_ref, q_ref, k_ref, v_ref, ccol_ref, qg_ref, kg_ref, out_ref,
                ka_sc, va_sc, m_sc, acc_sc, s_sc, *, tq, seq, hp):
    g = pl.program_id(1)
    i = pl.program_id(2)
    dh = F_HEAD_DIM
    lane = lax.broadcasted_iota(I32, (tq, LANES), 1)
    shift = bnd_ref[0]
    safe = bnd_ref[1] > 0.5

    def c_of(r0, hh):
        cc = ccol_ref[pl.ds(r0, tq), :]
        return jnp.sum(jnp.where(lane == G_FF + g * hp + hh, cc, 0.0), axis=1, keepdims=True) * LOG2E

    @pl.when(i == 0)
    def _():
        def prep_chunk(c, carry):
            r0 = pl.multiple_of(c * tq, tq)
            for hh in range(hp):
                kk = k_ref[pl.ds(r0, tq), hh * dh:(hh + 1) * dh].astype(F32)
                r = lax.rsqrt(jnp.mean(kk * kk, axis=-1, keepdims=True) + EPS)
                ka_sc[hh, pl.ds(r0, tq), 0:dh] = ((kk * r) * kg_ref[...]).astype(BF16)
                ka_sc[hh, pl.ds(r0, tq), dh:2 * dh] = _fox_aug(c_of(r0, hh), lane, False, shift)
                va_sc[hh, pl.ds(r0, tq), 0:dh] = v_ref[pl.ds(r0, tq), hh * dh:(hh + 1) * dh]
                va_sc[hh, pl.ds(r0, tq), dh:2 * dh] = jnp.ones((tq, dh), BF16)
            return carry
        lax.fori_loop(0, seq // tq, prep_chunk, 0)

    q0 = pl.multiple_of(i * tq, tq)
    qa = []
    for hh in range(hp):
        q = q_ref[:, hh * dh:(hh + 1) * dh].astype(F32)
        r = lax.rsqrt(jnp.mean(q * q, axis=-1, keepdims=True) + EPS)
        qn = (((q * r) * qg_ref[...]) * (dh ** -0.5 * LOG2E)).astype(BF16)
        qa.append(jnp.concatenate([qn, _fox_aug(c_of(q0, hh), lane, True, shift)], axis=1))

    acc_sc[...] = jnp.zeros_like(acc_sc)

    def scores(j, masked):
        r0 = pl.multiple_of(j * tq, tq)
        s = [lax.dot_general(qa[hh], ka_sc[hh, pl.ds(r0, tq), :], (((1,), (1,)), ((), ())),
                             preferred_element_type=F32) for hh in range(hp)]
        if masked:
            rr = lax.broadcasted_iota(I32, (tq, tq), 0)
            cc = lax.broadcasted_iota(I32, (tq, tq), 1)
            s = [jnp.where(rr >= cc, sh, -jnp.inf) for sh in s]
        return r0, s

    def fixed_consume(j, s):
        r0 = pl.multiple_of(j * tq, tq)
        for hh in range(hp):
            p = jnp.exp2(s[hh]).astype(BF16)
            acc_sc[hh] += jnp.dot(p, va_sc[hh, pl.ds(r0, tq), :], preferred_element_type=F32)

    def run_fixed():
        def produce(buf, j, masked):
            _, s = scores(j, masked)
            for hh in range(hp):
                s_sc[buf, hh] = s[hh]

        def consume(buf, j):
            fixed_consume(j, [s_sc[buf, hh] for hh in range(hp)])

        produce(0, i, True)
        pairs = i // 2

        def body(k, carry):
            produce(1, 2 * k, False)
            consume(0, jnp.where(k == 0, i, 2 * k - 1))
            produce(0, 2 * k + 1, False)
            consume(1, 2 * k)
            return carry
        lax.fori_loop(0, pairs, body, 0)
        pending = jnp.where(pairs == 0, i, 2 * pairs - 1)

        @pl.when(i % 2 == 1)
        def _():
            produce(1, i - 1, False)
            consume(0, pending)
            consume(1, i - 1)

        @pl.when(i % 2 == 0)
        def _():
            consume(0, pending)

    def running_step(j, masked):
        r0, s = scores(j, masked)
        for hh in range(hp):
            m_old = m_sc[hh]
            m_new = jnp.maximum(m_old, jnp.max(s[hh], axis=1, keepdims=True))
            p = jnp.exp2(s[hh] - m_new).astype(BF16)
            alpha = jnp.exp2(m_old - m_new)
            acc_sc[hh] = alpha * acc_sc[hh] + jnp.dot(p, va_sc[hh, pl.ds(r0, tq), :],
                                                      preferred_element_type=F32)
            m_sc[hh] = m_new

    def run(step):
        def body(j, carry):
            step(j, False)
            return carry
        lax.fori_loop(0, i, body, 0)
        step(i, True)

    @pl.when(safe)
    def _():
        run_fixed()

    @pl.when(jnp.logical_not(safe))
    def _():
        m_sc[...] = jnp.full_like(m_sc, -jnp.inf)
        run(running_step)

    for hh in range(hp):
        out_ref[:, hh * dh:(hh + 1) * dh] = (acc_sc[hh, :, 0:dh] / acc_sc[hh, :, dh:2 * dh]).astype(BF16)


def _fox(z, ccol, qn_g, kn_g, batch, seq, tq, hp):
    n = batch * seq
    nq = seq // tq
    w = hp * F_HEAD_DIM
    fq0 = ZB_FQ * (D_MODEL // w)
    fk0 = ZB_FK * (D_MODEL // w)
    fv0 = ZB_FV * (D_MODEL // w)
    shift = jnp.ceil(1.01 * LOG2E * F_HEAD_DIM ** 0.5 * jnp.max(jnp.abs(qn_g)) * jnp.max(jnp.abs(kn_g)))
    safe = (2.0 * shift <= FOX_SAFE_LOG2_RANGE).astype(F32)
    bnd = jnp.stack([shift, safe]).astype(F32)
    return pl.pallas_call(
        functools.partial(_fox_kernel, tq=tq, seq=seq, hp=hp),
        grid=(batch, F_HEADS // hp, nq),
        in_specs=[
            pl.BlockSpec(memory_space=pltpu.SMEM),
            pl.BlockSpec((tq, w), lambda b, g, i: (b * nq + i, fq0 + g)),
            pl.BlockSpec((seq, w), lambda b, g, i: (b, fk0 + g)),
            pl.BlockSpec((seq, w), lambda b, g, i: (b, fv0 + g)),
            pl.BlockSpec((seq, LANES), lambda b, g, i: (b, 0)),
            pl.BlockSpec((1, F_HEAD_DIM), lambda b, g, i: (0, 0)),
            pl.BlockSpec((1, F_HEAD_DIM), lambda b, g, i: (0, 0)),
        ],
        out_specs=pl.BlockSpec((tq, w), lambda b, g, i: (b * nq + i, g)),
        out_shape=jax.ShapeDtypeStruct((n, D_MODEL), BF16),
        scratch_shapes=[
            pltpu.VMEM((hp, seq, 2 * F_HEAD_DIM), BF16),
            pltpu.VMEM((hp, seq, 2 * F_HEAD_DIM), BF16),
            pltpu.VMEM((hp, tq, 1), F32),
            pltpu.VMEM((hp, tq, 2 * F_HEAD_DIM), F32),
            pltpu.VMEM((2, hp, tq, tq), F32),
        ],
        compiler_params=_cparams(("parallel", "parallel", "arbitrary")),
        name="fox",
    )(bnd, z, z, z, ccol, qn_g, kn_g)


def _topk_rows(s, k, store, ids=None):
    if ids is None:
        ids = lax.broadcasted_iota(I32, s.shape, 0)
    big = jnp.int32(2 ** 30)
    for r in range(k):
        mx = jnp.max(s, axis=0, keepdims=True)
        ix = jnp.min(jnp.where(s == mx, ids, big), axis=0, keepdims=True)
        s = jnp.where(ids == ix, -jnp.inf, s)
        store(r, mx, ix)


def _pair_candidates(v0, v1):
    k = P_TOPK
    w = v0.shape[1]
    row8 = lax.broadcasted_iota(I32, (SUBLANES, w), 0)
    sums = [v0[0:1, :] + v1]
    ids = [lax.broadcasted_iota(I32, (k, w), 0)]
    for a in range(1, SUBLANES):
        nb = k // (a + 1)
        sums.append(jnp.where(row8 < nb, v0[a:a + 1, :] + v1[0:SUBLANES, :], -jnp.inf))
        ids.append(row8 + a * k)
    sums.append(v0[SUBLANES:k, :] + v1[0:1, :])
    ids.append((row8 + SUBLANES) * k)
    return jnp.concatenate(sums, axis=0), jnp.concatenate(ids, axis=0)


def _merge_kernel(hm_ref, hf_ref, gm_ref, gf_ref, x_ref, wm_ref, wf_ref, wo_ref, g2_ref,
                  wpq_ref, sk_ref, x1_ref, xn_ref, ids_i_ref, ids_j_ref, gate_ref,
                  sc_sc, val_sc, idx_sc, top_sc, pos_sc, oi_sc, oj_sc, og_sc, *, tm, tc):
    ym = jnp.dot(hm_ref[...], wm_ref[...], preferred_element_type=F32)
    yf = jnp.dot(hf_ref[...], wf_ref[...], preferred_element_type=F32)
    y = (_sigmoid(gm_ref[...].astype(F32)) * ym + _sigmoid(gf_ref[...].astype(F32)) * yf)
    x1 = x_ref[...] + jnp.dot(y.astype(BF16), wo_ref[...], preferred_element_type=F32)
    x1_ref[...] = x1
    r = lax.rsqrt(jnp.mean(x1 * x1, axis=-1, keepdims=True) + EPS)
    xn = ((x1 * r) * g2_ref[...]).astype(BF16)
    xn_ref[...] = xn
    qh = jnp.dot(xn, wpq_ref[...], preferred_element_type=F32).astype(BF16)

    for hc in range(2 * P_HEADS):
        sc_sc[hc] = lax.dot_general(sk_ref[hc], qh[:, hc * P_HALF:(hc + 1) * P_HALF],
                                    (((1,), (1,)), ((), ())), preferred_element_type=F32)

    def chunk_body(c, carry):
        c0 = pl.multiple_of(c * tc, tc)

        def first_level(hc, carry2):
            s = sc_sc[hc, :, pl.ds(c0, tc)]

            def store(r, val, idx):
                val_sc[hc, r:r + 1, :] = val
                idx_sc[hc, r:r + 1, :] = idx
            _topk_rows(s, P_TOPK, store)
            return carry2
        lax.fori_loop(0, 2 * P_HEADS, first_level, 0)

        def second_level(hd, carry2):
            v0 = val_sc[2 * hd]
            v1 = val_sc[2 * hd + 1]
            i0 = idx_sc[2 * hd]
            i1 = idx_sc[2 * hd + 1]
            cand, cand_id = _pair_candidates(v0, v1)

            def store(r, val, idx):
                top_sc[r:r + 1, :] = val
                pos_sc[r:r + 1, :] = idx
            _topk_rows(cand, P_TOPK, store, cand_id)
            top = top_sc[...]
            pos = pos_sc[...]
            pa = pos >> 4
            pb = pos & (P_TOPK - 1)
            sel_i = jnp.zeros(pos.shape, I32)
            sel_j = jnp.zeros(pos.shape, I32)
            for a in range(P_TOPK):
                sel_i = jnp.where(pa == a, i0[a:a + 1, :], sel_i)
                sel_j = jnp.where(pb == a, i1[a:a + 1, :], sel_j)
            e = jnp.exp(top - top[0:1, :])
            gate = e / jnp.sum(e, axis=0, keepdims=True)
            r0 = pl.multiple_of(hd * P_TOPK, P_TOPK)
            oi_sc[pl.ds(r0, P_TOPK), :] = sel_i
            oj_sc[pl.ds(r0, P_TOPK), :] = sel_j
            og_sc[pl.ds(r0, P_TOPK), :] = gate
            return carry2
        lax.fori_loop(0, P_HEADS, second_level, 0)

        ids_i_ref[pl.ds(c0, tc), :] = oi_sc[...].T
        ids_j_ref[pl.ds(c0, tc), :] = oj_sc[...].T
        gate_ref[pl.ds(c0, tc), :] = og_sc[...].T
        return carry
    lax.fori_loop(0, tm // tc, chunk_body, 0)


def _merge(hm, hf, z, x2, w_m, w_f, w_o, g2, w_pq, sk, tm, tc):
    n = x2.shape[0]
    hk = P_HEADS * P_TOPK
    row = lambda i: (i, 0)
    const = lambda i: (0, 0)
    return pl.pallas_call(
        functools.partial(_merge_kernel, tm=tm, tc=tc),
        grid=(n // tm,),
        in_specs=[
            pl.BlockSpec((tm, D_MODEL), row),
            pl.BlockSpec((tm, D_MODEL), row),
            pl.BlockSpec((tm, D_MODEL), lambda i: (i, ZB_GM)),
            pl.BlockSpec((tm, D_MODEL), lambda i: (i, ZB_GF)),
            pl.BlockSpec((tm, D_MODEL), row),
            pl.BlockSpec((D_MODEL, D_MODEL), const),
            pl.BlockSpec((D_MODEL, D_MODEL), const),
            pl.BlockSpec((D_MODEL, D_MODEL), const),
            pl.BlockSpec((1, D_MODEL), const),
            pl.BlockSpec((D_MODEL, 2 * P_HEADS * P_HALF), const),
            pl.BlockSpec((2 * P_HEADS, P_KEYS, P_HALF), lambda i: (0, 0, 0)),
        ],
        out_specs=[
            pl.BlockSpec((tm, D_MODEL), row),
            pl.BlockSpec((tm, D_MODEL), row),
            pl.BlockSpec((tm, hk), row),
            pl.BlockSpec((tm, hk), row),
            pl.BlockSpec((tm, hk), row),
        ],
        out_shape=[
            jax.ShapeDtypeStruct((n, D_MODEL), F32),
            jax.ShapeDtypeStruct((n, D_MODEL), BF16),
            jax.ShapeDtypeStruct((n, hk), I32),
            jax.ShapeDtypeStruct((n, hk), I32),
            jax.ShapeDtypeStruct((n, hk), F32),
        ],
        scratch_shapes=[
            pltpu.VMEM((2 * P_HEADS, P_KEYS, tm), F32),
            pltpu.VMEM((2 * P_HEADS, P_TOPK, tc), F32),
            pltpu.VMEM((2 * P_HEADS, P_TOPK, tc), I32),
            pltpu.VMEM((P_TOPK, tc), F32),
            pltpu.VMEM((P_TOPK, tc), I32),
            pltpu.VMEM((hk, tc), I32),
            pltpu.VMEM((hk, tc), I32),
            pltpu.VMEM((hk, tc), F32),
        ],
        compiler_params=_cparams(("parallel",)),
        name="merge_retrieve",
    )(hm, hf, z, z, x2, w_m, w_f, w_o, g2, w_pq, sk)


U_SUB = 512


def _peer_u_kernel(xn_ref, u_ref, ii_ref, jj_ref, gate_ref, hw_ref, ids_ref, acc_sc, *, eb):
    e = pl.program_id(1)
    ne = pl.num_programs(1)

    @pl.when(e == 0)
    def _():
        acc_sc[...] = jnp.zeros_like(acc_sc)

    xn = xn_ref[...]
    ii = ii_ref[...]
    jj = jj_ref[...]
    acc = acc_sc[...]
    for sb in range(eb // U_SUB):
        a = lax.dot_general(xn, u_ref[sb * U_SUB:(sb + 1) * U_SUB, :], (((1,), (1,)), ((), ())),
                            preferred_element_type=F32)
        for c in range(U_SUB // P_KEYS):
            blk = e * (eb // P_KEYS) + sb * (U_SUB // P_KEYS) + c
            g = jnp.take_along_axis(a[:, c * P_KEYS:(c + 1) * P_KEYS], jj, axis=1,
                                    mode="promise_in_bounds")
            acc = jnp.where(ii == blk, g, acc)
    acc_sc[...] = acc

    @pl.when(e == ne - 1)
    def _():
        av = acc_sc[...]
        gelu = 0.5 * av * (1.0 + lax.erf(av * (2.0 ** -0.5)))
        hw_ref[...] = gate_ref[...] * gelu
        ids_ref[...] = (ii * P_KEYS + jj) * (D_MODEL // (2 * LANES))


def _peer_u(xn, u_bf, ids_i, ids_j, gate, tm, eb):
    n = xn.shape[0]
    hk = P_HEADS * P_TOPK
    row = lambda i, e: (i, 0)
    return pl.pallas_call(
        functools.partial(_peer_u_kernel, eb=eb),
        grid=(n // tm, N_EXPERTS // eb),
        in_specs=[
            pl.BlockSpec((tm, D_MODEL), row),
            pl.BlockSpec((eb, D_MODEL), lambda i, e: (e, 0)),
            pl.BlockSpec((tm, hk), row),
            pl.BlockSpec((tm, hk), row),
            pl.BlockSpec((tm, hk), row),
        ],
        out_specs=[pl.BlockSpec((tm, hk), row), pl.BlockSpec((tm, hk), row)],
        out_shape=[jax.ShapeDtypeStruct((n, hk), F32), jax.ShapeDtypeStruct((n, hk), I32)],
        scratch_shapes=[pltpu.VMEM((tm, hk), F32)],
        compiler_params=_cparams(("parallel", "arbitrary")),
        name="peer_u",
    )(xn, u_bf, ids_i, ids_j, gate)


V_ROW_WORDS = D_MODEL // (2 * LANES)


V_SLABS = 4
V_GROUP = 16


def _peer_v_kernel(ids_ref, w_ref, x1_ref, tab_ref, out_ref, slab_sc, *, tt):
    hk = P_HEADS * P_TOPK
    tg = V_GROUP
    plane = tg + SUBLANES
    half = D_MODEL // 2

    def group(g, carry):
        t0 = pl.multiple_of(g * tg, tg)
        wg = w_ref[pl.ds(t0, tg), :]
        acc_lo = [jnp.zeros((tg, LANES), F32) for _ in range(V_ROW_WORDS)]
        acc_hi = [jnp.zeros((tg, LANES), F32) for _ in range(V_ROW_WORDS)]
        for m in range(hk):
            slab = slab_sc.at[m % V_SLABS]
            for t in range(tg):
                r0 = pl.multiple_of(ids_ref[t0 + t, m], V_ROW_WORDS)
                slab[pl.ds(t, V_ROW_WORDS, stride=plane), :] = tab_ref[pl.ds(r0, V_ROW_WORDS), :]
            wcol = wg[:, m:m + 1]
            for s in range(V_ROW_WORDS):
                words = slab[s * plane:s * plane + tg, :]
                acc_lo[s] = acc_lo[s] + wcol * lax.bitcast_convert_type(words << 16, F32)
                acc_hi[s] = acc_hi[s] + wcol * lax.bitcast_convert_type(words & jnp.int32(-65536), F32)
        for s in range(V_ROW_WORDS):
            lo = slice(s * LANES, (s + 1) * LANES)
            hi = slice(half + s * LANES, half + (s + 1) * LANES)
            out_ref[pl.ds(t0, tg), lo] = x1_ref[pl.ds(t0, tg), lo] + acc_lo[s]
            out_ref[pl.ds(t0, tg), hi] = x1_ref[pl.ds(t0, tg), hi] + acc_hi[s]
        return carry
    lax.fori_loop(0, tt // tg, group, 0)


def _peer_v(ids, hw, x1, v_packed, tt):
    n = ids.shape[0]
    hk = P_HEADS * P_TOPK
    return pl.pallas_call(
        functools.partial(_peer_v_kernel, tt=tt),
        grid=(n // tt,),
        in_specs=[
            pl.BlockSpec((tt, hk), lambda i: (i, 0), memory_space=pltpu.SMEM),
            pl.BlockSpec((tt, hk), lambda i: (i, 0)),
            pl.BlockSpec((tt, D_MODEL), lambda i: (i, 0)),
            pl.BlockSpec((N_EXPERTS * V_ROW_WORDS, LANES), lambda i: (0, 0),
                         pipeline_mode=pl.Buffered(1)),
        ],
        out_specs=pl.BlockSpec((tt, D_MODEL), lambda i: (i, 0)),
        out_shape=jax.ShapeDtypeStruct((n, D_MODEL), F32),
        scratch_shapes=[pltpu.VMEM((V_SLABS, V_ROW_WORDS * (V_GROUP + SUBLANES), LANES), I32)],
        compiler_params=_cparams(("arbitrary",)),
        name="peer_v",
    )(ids, hw, x1, v_packed)


def _pack_v(v_tab):
    vb = lax.bitcast_convert_type(v_tab.astype(BF16), jnp.uint16).astype(jnp.uint32)
    half = D_MODEL // 2
    packed = vb[:, :half] | (vb[:, half:] << 16)
    return lax.bitcast_convert_type(packed, I32).reshape(N_EXPERTS * V_ROW_WORDS, LANES)


def _layer(x2, batch, seq, norm1_g, w_in, b_in, conv_w, m_norm_g, qn_g, kn_g, w_m_out, w_f_out,
           w_out, norm2_g, w_pq, sub_keys, u_tab, v_tab):
    o = _OFFS
    cols = lambda a: jnp.concatenate([a[..., o[0]:o[4]], a[..., o[6]:o[9]], a[..., o[10]:o[12]]], axis=-1)
    small = lambda a: jnp.concatenate([a[..., o[4]:o[6]], a[..., o[9]:o[10]]], axis=-1)
    w_main = cols(w_in).astype(BF16)
    b_main = cols(b_in)[None, :]
    ws = small(w_in)
    bs = small(b_in)
    pad = LANES - ws.shape[1]
    ws = jnp.pad(ws, ((0, 0), (0, pad))).astype(BF16)
    bs = jnp.pad(bs, (0, pad))

    tm_a = min(1024, x2.shape[0])
    z, gates_col, gates_row = _inproj(x2, norm1_g[None, :], w_main, b_main, ws, ws.T,
                                      bs[None, :], bs[:, None], tm_a)
    ccol = _forget_cumsum(gates_row, batch, seq)
    chunk = min(256, seq)
    hm = _mlstm(z, gates_col, gates_row, conv_w, m_norm_g[None, :], batch, seq, chunk)
    tq = min(512, seq)
    hf = _fox(z, ccol, qn_g[None, :], kn_g[None, :], batch, seq, tq, 2)

    sk = sub_keys.reshape(2 * P_HEADS, P_KEYS, P_HALF).astype(BF16)
    tm_d = min(512, x2.shape[0])
    x1, xn, ids_i, ids_j, gate = _merge(hm, hf, z, x2, w_m_out.astype(BF16), w_f_out.astype(BF16),
                                        w_out.astype(BF16), norm2_g[None, :], w_pq.astype(BF16), sk,
                                        tm_d, 256)
    tm_e = min(1024, x2.shape[0])
    hw, ids = _peer_u(xn, u_tab.astype(BF16), ids_i, ids_j, gate, tm_e, 2048)
    return _peer_v(ids, hw, x1, _pack_v(v_tab), min(128, x1.shape[0]))


def kernel(x, norm1_g, w_in, b_in, conv_w, m_norm_g, qn_g, kn_g, w_m_out, w_f_out, w_out,
           norm2_g, w_pq, sub_keys, u_tab, v_tab):
    batch, seq, d = x.shape
    x2 = x.reshape(batch * seq, d)
    for l in range(w_in.shape[0]):
        x2 = _layer(x2, batch, seq, norm1_g[l], w_in[l], b_in[l], conv_w[l], m_norm_g[l], qn_g[l],
                    kn_g[l], w_m_out[l], w_f_out[l], w_out[l], norm2_g[l], w_pq[l], sub_keys[l],
                    u_tab[l], v_tab[l])
    return x2.reshape(batch, seq, d)
```

```python
import functools

import jax
import jax.numpy as jnp
from jax import lax
from jax.experimental import pallas as pl
from jax.experimental.pallas import tpu as pltpu

F32 = jnp.float32
BF16 = jnp.bfloat16
I32 = jnp.int32

D_MODEL = 1024
M_HEADS = 4
M_HEAD_DIM = 256
F_HEADS = 8
F_HEAD_DIM = 128
CONV_WIDTH = 4
P_HEADS = 8
P_KEYS = 128
P_TOPK = 16
P_HALF = 128
N_EXPERTS = P_KEYS * P_KEYS
EPS = 1e-6
M_INIT = -1e30
LANES = 128
SUBLANES = 8
VMEM_LIMIT = 56 * 1024 * 1024

_SIZES = (1024, 1024, 1024, 1024, M_HEADS, M_HEADS, 1024, 1024, 1024, F_HEADS, 1024, 1024)
_OFFS = [0]
for _s in _SIZES:
    _OFFS.append(_OFFS[-1] + _s)
Z_COLS = 9 * 1024
ZB_MQ, ZB_MK, ZB_MV, ZB_MO, ZB_FQ, ZB_FK, ZB_FV, ZB_GM, ZB_GF = range(9)
G_MI, G_MF, G_FF = 0, M_HEADS, 2 * M_HEADS


def _cparams(sem):
    return pltpu.CompilerParams(dimension_semantics=sem, vmem_limit_bytes=VMEM_LIMIT)


def _log_sigmoid(x):
    return jnp.minimum(x, 0.0) - jnp.log1p(jnp.exp(-jnp.abs(x)))


def _sigmoid(x):
    return 1.0 / (1.0 + jnp.exp(-x))


def _inproj_kernel(x_ref, g_ref, w_ref, b_ref, ws_ref, wst_ref, bs_ref, bst_ref,
                   z_ref, gc_ref, gr_ref, h_sc):
    j = pl.program_id(1)

    @pl.when(j == 0)
    def _():
        x = x_ref[...]
        r = lax.rsqrt(jnp.mean(x * x, axis=-1, keepdims=True) + EPS)
        h = ((x * r) * g_ref[...]).astype(BF16)
        h_sc[...] = h
        gc_ref[...] = jnp.dot(h, ws_ref[...], preferred_element_type=F32) + bs_ref[...]
        gr_ref[...] = lax.dot_general(wst_ref[...], h, (((1,), (1,)), ((), ())),
                                      preferred_element_type=F32) + bst_ref[...]

    z = jnp.dot(h_sc[...], w_ref[...], preferred_element_type=F32) + b_ref[...]
    z_ref[...] = z.astype(BF16)


def _inproj(x2, g1, w_main, b_main, w_small, w_small_t, b_small, b_small_t, tm):
    n = x2.shape[0]
    tn = 1024
    grid = (n // tm, Z_COLS // tn)
    return pl.pallas_call(
        _inproj_kernel,
        grid=grid,
        in_specs=[
            pl.BlockSpec((tm, D_MODEL), lambda i, j: (i, 0)),
            pl.BlockSpec((1, D_MODEL), lambda i, j: (0, 0)),
            pl.BlockSpec((D_MODEL, tn), lambda i, j: (0, j)),
            pl.BlockSpec((1, tn), lambda i, j: (0, j)),
            pl.BlockSpec((D_MODEL, LANES), lambda i, j: (0, 0)),
            pl.BlockSpec((LANES, D_MODEL), lambda i, j: (0, 0)),
            pl.BlockSpec((1, LANES), lambda i, j: (0, 0)),
            pl.BlockSpec((LANES, 1), lambda i, j: (0, 0)),
        ],
        out_specs=[
            pl.BlockSpec((tm, tn), lambda i, j: (i, j)),
            pl.BlockSpec((tm, LANES), lambda i, j: (i, 0)),
            pl.BlockSpec((LANES, tm), lambda i, j: (0, i)),
        ],
        out_shape=[
            jax.ShapeDtypeStruct((n, Z_COLS), BF16),
            jax.ShapeDtypeStruct((n, LANES), F32),
            jax.ShapeDtypeStruct((LANES, n), F32),
        ],
        scratch_shapes=[pltpu.VMEM((tm, D_MODEL), BF16)],
        compiler_params=_cparams(("parallel", "arbitrary")),
        name="inproj",
    )(x2, g1, w_main, b_main, w_small, w_small_t, b_small, b_small_t)


def _cumsum_kernel(gr_ref, ccol_ref):
    x = _log_sigmoid(gr_ref[...])
    t = x.shape[1]
    lane = lax.broadcasted_iota(I32, x.shape, 1)
    s = 1
    while s < t:
        x = x + jnp.where(lane >= s, pltpu.roll(x, s, axis=1), 0.0)
        s *= 2
    ccol_ref[...] = x.T


def _forget_cumsum(gates_row, batch, seq):
    n = batch * seq
    return pl.pallas_call(
        _cumsum_kernel,
        grid=(batch,),
        in_specs=[pl.BlockSpec((LANES, seq), lambda b: (0, b))],
        out_specs=pl.BlockSpec((seq, LANES), lambda b: (b, 0)),
        out_shape=jax.ShapeDtypeStruct((n, LANES), F32),
        compiler_params=_cparams(("parallel",)),
        name="forget_cumsum",
    )(gates_row)


def _mlstm_kernel(q_ref, k_ref, v_ref, o_ref, gc_ref, gr_ref, cw_ref, mg_ref, out_ref,
                  c_sc, n_sc, m_sc, ext_sc, *, chunk):
    L = chunk
    dh = M_HEAD_DIM
    t = pl.program_id(1)

    @pl.when(t == 0)
    def _():
        c_sc[...] = jnp.zeros_like(c_sc)
        n_sc[...] = jnp.zeros_like(n_sc)
        m_sc[...] = jnp.full_like(m_sc, M_INIT)
        ext_sc[0:SUBLANES, :] = jnp.zeros((SUBLANES, 2 * D_MODEL), F32)

    ext_sc[SUBLANES:SUBLANES + L, 0:D_MODEL] = q_ref[...].astype(F32)
    ext_sc[SUBLANES:SUBLANES + L, D_MODEL:2 * D_MODEL] = k_ref[...].astype(F32)

    gc = gc_ref[...]
    gr = gr_ref[...]
    row_i = lax.broadcasted_iota(I32, (L, L), 0)
    col_i = lax.broadcasted_iota(I32, (L, L), 1)
    tril = col_i <= row_i

    def conv_silu(c0):
        base = SUBLANES - (CONV_WIDTH - 1)
        acc = ext_sc[base:base + L, c0:c0 + dh] * cw_ref[0:1, c0:c0 + dh]
        for j in range(1, CONV_WIDTH):
            acc = acc + ext_sc[base + j:base + j + L, c0:c0 + dh] * cw_ref[j:j + 1, c0:c0 + dh]
        return acc * _sigmoid(acc)

    for h in range(M_HEADS):
        q = conv_silu(h * dh) * (dh ** -0.5)
        k = conv_silu(D_MODEL + h * dh)
        v = v_ref[:, h * dh:(h + 1) * dh]
        qb = q.astype(BF16)
        kb = k.astype(BF16)

        i_col = gc[:, G_MI + h:G_MI + h + 1]
        lf_col = _log_sigmoid(gc[:, G_MF + h:G_MF + h + 1])
        i_row = gr[G_MI + h:G_MI + h + 1, :]
        lf_row = _log_sigmoid(gr[G_MF + h:G_MF + h + 1, :])

        b_col = jnp.sum(jnp.where(tril, lf_row, 0.0), axis=1, keepdims=True)
        b_row = jnp.sum(jnp.where(row_i <= col_i, lf_col, 0.0), axis=0, keepdims=True)
        g = jnp.sum(lf_row, axis=1, keepdims=True)
        m_prev = m_sc[h][0:1, 0:1]

        dmat = jnp.where(tril, b_col - b_row + i_row, -jnp.inf)
        inter = b_col + m_prev
        m_t = jnp.maximum(jnp.max(dmat, axis=1, keepdims=True), inter)
        p = jnp.exp(dmat - m_t)
        s = lax.dot_general(qb, kb, (((1,), (1,)), ((), ())), preferred_element_type=F32) * p
        w_inter = jnp.exp(inter - m_t)
        c_old = c_sc[h]
        n_old = n_sc[h]
        num = (jnp.dot(s.astype(BF16), v, preferred_element_type=F32)
               + w_inter * jnp.dot(qb, c_old.astype(BF16), preferred_element_type=F32))
        den = (jnp.sum(s, axis=1, keepdims=True)
               + w_inter * jnp.sum(q * n_old, axis=1, keepdims=True))
        hh = num / jnp.maximum(jnp.abs(den), jnp.exp(-m_t))

        log_w = g - b_col + i_col
        m_new = jnp.maximum(g + m_prev, jnp.max(log_w, axis=0, keepdims=True))
        w_s = jnp.exp(log_w - m_new)
        decay = jnp.exp(g + m_prev - m_new)
        wv = (w_s * v.astype(F32)).astype(BF16)
        c_sc[h] = decay * c_old + lax.dot_general(kb, wv, (((0,), (0,)), ((), ())),
                                                  preferred_element_type=F32)
        n_sc[h] = decay * n_old + jnp.sum(w_s * k, axis=0, keepdims=True)
        m_sc[h] = jnp.broadcast_to(m_new, (SUBLANES, LANES))

        r = lax.rsqrt(jnp.mean(hh * hh, axis=-1, keepdims=True) + EPS)
        hn = (hh * r) * mg_ref[0:1, h * dh:(h + 1) * dh]
        og = _sigmoid(o_ref[:, h * dh:(h + 1) * dh].astype(F32))
        out_ref[:, h * dh:(h + 1) * dh] = (hn * og).astype(BF16)

    ext_sc[0:SUBLANES, :] = ext_sc[L:L + SUBLANES, :]


def _mlstm(z, gates_col, gates_row, conv_w, m_norm_g, batch, seq, chunk):
    n = batch * seq
    nt = seq // chunk
    zspec = lambda cb: pl.BlockSpec((chunk, D_MODEL), lambda b, t, cb=cb: (b * nt + t, cb))
    return pl.pallas_call(
        functools.partial(_mlstm_kernel, chunk=chunk),
        grid=(batch, nt),
        in_specs=[
            zspec(ZB_MQ), zspec(ZB_MK), zspec(ZB_MV), zspec(ZB_MO),
            pl.BlockSpec((chunk, LANES), lambda b, t: (b * nt + t, 0)),
            pl.BlockSpec((LANES, chunk), lambda b, t: (0, b * nt + t)),
            pl.BlockSpec((CONV_WIDTH, 2 * D_MODEL), lambda b, t: (0, 0)),
            pl.BlockSpec((1, D_MODEL), lambda b, t: (0, 0)),
        ],
        out_specs=pl.BlockSpec((chunk, D_MODEL), lambda b, t: (b * nt + t, 0)),
        out_shape=jax.ShapeDtypeStruct((n, D_MODEL), BF16),
        scratch_shapes=[
            pltpu.VMEM((M_HEADS, M_HEAD_DIM, M_HEAD_DIM), F32),
            pltpu.VMEM((M_HEADS, 1, M_HEAD_DIM), F32),
            pltpu.VMEM((M_HEADS, SUBLANES, LANES), F32),
            pltpu.VMEM((chunk + SUBLANES, 2 * D_MODEL), F32),
        ],
        compiler_params=_cparams(("parallel", "arbitrary")),
        name="mlstm",
    )(z, z, z, z, gates_col, gates_row, conv_w, m_norm_g)


LOG2E = 1.4426950408889634
FOX_SAFE_LOG2_RANGE = 100.0


def _split3(c):
    hi = c.astype(BF16).astype(F32)
    r1 = c - hi
    mid = r1.astype(BF16).astype(F32)
    lo = (r1 - mid).astype(BF16).astype(F32)
    return hi, mid, lo


def _fox_aug(c, lane, is_query, shift):
    hi, mid, lo = _split3(c if is_query else -c)
    off = 0 if is_query else 3
    ex = jnp.where(lane == off, hi, jnp.where(lane == off + 1, mid, jnp.where(lane == off + 2, lo, 0.0)))
    ones_at = (lane >= 3) & (lane < 6) if is_query else lane < 3
    ex = jnp.where(ones_at, 1.0, ex)
    return jnp.where(lane == 6, -shift if is_query else 1.0, ex).astype(BF16)


def _fox_kernel(bnd_ref, q_ref, k_ref, v_ref, ccol_ref, qg_ref, kg_ref, out_ref,
                ka_sc, va_sc, m_sc, acc_sc, s_sc, *, tq, seq, hp):
    g = pl.program_id(1)
    i = pl.program_id(2)
    dh = F_HEAD_DIM
    lane = lax.broadcasted_iota(I32, (tq, LANES), 1)
    shift = bnd_ref[0]
    safe = bnd_ref[1] > 0.5

    def c_of(r0, hh):
        cc = ccol_ref[pl.ds(r0, tq), :]
        return jnp.sum(jnp.where(lane == G_FF + g * hp + hh, cc, 0.0), axis=1, keepdims=True) * LOG2E

    @pl.when(i == 0)
    def _():
        def prep_chunk(c, carry):
            r0 = pl.multiple_of(c * tq, tq)
            for hh in range(hp):
                kk = k_ref[pl.ds(r0, tq), hh * dh:(hh + 1) * dh].astype(F32)
                r = lax.rsqrt(jnp.mean(kk * kk, axis=-1, keepdims=True) + EPS)
                ka_sc[hh, pl.ds(r0, tq), 0:dh] = ((kk * r) * kg_ref[...]).astype(BF16)
                ka_sc[hh, pl.ds(r0, tq), dh:2 * dh] = _fox_aug(c_of(r0, hh), lane, False, shift)
                va_sc[hh, pl.ds(r0, tq), 0:dh] = v_ref[pl.ds(r0, tq), hh * dh:(hh + 1) * dh]
                va_sc[hh, pl.ds(r0, tq), dh:2 * dh] = jnp.ones((tq, dh), BF16)
            return carry
        lax.fori_loop(0, seq // tq, prep_chunk, 0)

    q0 = pl.multiple_of(i * tq, tq)
    qa = []
    for hh in range(hp):
        q = q_ref[:, hh * dh:(hh + 1) * dh].astype(F32)
        r = lax.rsqrt(jnp.mean(q * q, axis=-1, keepdims=True) + EPS)
        qn = (((q * r) * qg_ref[...]) * (dh ** -0.5 * LOG2E)).astype(BF16)
        qa.append(jnp.concatenate([qn, _fox_aug(c_of(q0, hh), lane, True, shift)], axis=1))

    acc_sc[...] = jnp.zeros_like(acc_sc)

    def scores(j, masked):
        r0 = pl.multiple_of(j * tq, tq)
        s = [lax.dot_general(qa[hh], ka_sc[hh, pl.ds(r0, tq), :], (((1,), (1,)), ((), ())),
                             preferred_element_type=F32) for hh in range(hp)]
        if masked:
            rr = lax.broadcasted_iota(I32, (tq, tq), 0)
            cc = lax.broadcasted_iota(I32, (tq, tq), 1)
            s = [jnp.where(rr >= cc, sh, -jnp.inf) for sh in s]
        return r0, s

    def fixed_consume(j, s):
        r0 = pl.multiple_of(j * tq, tq)
        for hh in range(hp):
            p = jnp.exp2(s[hh]).astype(BF16)
            acc_sc[hh] += jnp.dot(p, va_sc[hh, pl.ds(r0, tq), :], preferred_element_type=F32)

    def run_fixed():
        def produce(buf, j, masked):
            _, s = scores(j, masked)
            for hh in range(hp):
                s_sc[buf, hh] = s[hh]

        def consume(buf, j):
            fixed_consume(j, [s_sc[buf, hh] for hh in range(hp)])

        produce(0, i, True)
        pairs = i // 2

        def body(k, carry):
            produce(1, 2 * k, False)
            consume(0, jnp.where(k == 0, i, 2 * k - 1))
            produce(0, 2 * k + 1, False)
            consume(1, 2 * k)
            return carry
        lax.fori_loop(0, pairs, body, 0)
        pending = jnp.where(pairs == 0, i, 2 * pairs - 1)

        @pl.when(i % 2 == 1)
        def _():
            produce(1, i - 1, False)
            consume(0, pending)
            consume(1, i - 1)

        @pl.when(i % 2 == 0)
        def _():
            consume(0, pending)

    def running_step(j, masked):
        r0, s = scores(j, masked)
        for hh in range(hp):
            m_old = m_sc[hh]
            m_new = jnp.maximum(m_old, jnp.max(s[hh], axis=1, keepdims=True))
            p = jnp.exp2(s[hh] - m_new).astype(BF16)
            alpha = jnp.exp2(m_old - m_new)
            acc_sc[hh] = alpha * acc_sc[hh] + jnp.dot(p, va_sc[hh, pl.ds(r0, tq), :],
                                                      preferred_element_type=F32)
            m_sc[hh] = m_new

    def run(step):
        def body(j, carry):
            step(j, False)
            return carry
        lax.fori_loop(0, i, body, 0)
        step(i, True)

    @pl.when(safe)
    def _():
        run_fixed()

    @pl.when(jnp.logical_not(safe))
    def _():
        m_sc[...] = jnp.full_like(m_sc, -jnp.inf)
        run(running_step)

    for hh in range(hp):
        out_ref[:, hh * dh:(hh + 1) * dh] = (acc_sc[hh, :, 0:dh] / acc_sc[hh, :, dh:2 * dh]).astype(BF16)


def _fox(z, ccol, qn_g, kn_g, batch, seq, tq, hp):
    n = batch * seq
    nq = seq // tq
    w = hp * F_HEAD_DIM
    fq0 = ZB_FQ * (D_MODEL // w)
    fk0 = ZB_FK * (D_MODEL // w)
    fv0 = ZB_FV * (D_MODEL // w)
    shift = jnp.ceil(1.01 * LOG2E * F_HEAD_DIM ** 0.5 * jnp.max(jnp.abs(qn_g)) * jnp.max(jnp.abs(kn_g)))
    safe = (2.0 * shift <= FOX_SAFE_LOG2_RANGE).astype(F32)
    bnd = jnp.stack([shift, safe]).astype(F32)
    return pl.pallas_call(
        functools.partial(_fox_kernel, tq=tq, seq=seq, hp=hp),
        grid=(batch, F_HEADS // hp, nq),
        in_specs=[
            pl.BlockSpec(memory_space=pltpu.SMEM),
            pl.BlockSpec((tq, w), lambda b, g, i: (b * nq + i, fq0 + g)),
            pl.BlockSpec((seq, w), lambda b, g, i: (b, fk0 + g)),
            pl.BlockSpec((seq, w), lambda b, g, i: (b, fv0 + g)),
            pl.BlockSpec((seq, LANES), lambda b, g, i: (b, 0)),
            pl.BlockSpec((1, F_HEAD_DIM), lambda b, g, i: (0, 0)),
            pl.BlockSpec((1, F_HEAD_DIM), lambda b, g, i: (0, 0)),
        ],
        out_specs=pl.BlockSpec((tq, w), lambda b, g, i: (b * nq + i, g)),
        out_shape=jax.ShapeDtypeStruct((n, D_MODEL), BF16),
        scratch_shapes=[
            pltpu.VMEM((hp, seq, 2 * F_HEAD_DIM), BF16),
            pltpu.VMEM((hp, seq, 2 * F_HEAD_DIM), BF16),
            pltpu.VMEM((hp, tq, 1), F32),
            pltpu.VMEM((hp, tq, 2 * F_HEAD_DIM), F32),
            pltpu.VMEM((2, hp, tq, tq), F32),
        ],
        compiler_params=_cparams(("parallel", "parallel", "arbitrary")),
        name="fox",
    )(bnd, z, z, z, ccol, qn_g, kn_g)


def _topk_rows(s, k, store, ids=None):
    if ids is None:
        ids = lax.broadcasted_iota(I32, s.shape, 0)
    big = jnp.int32(2 ** 30)
    for r in range(k):
        mx = jnp.max(s, axis=0, keepdims=True)
        ix = jnp.min(jnp.where(s == mx, ids, big), axis=0, keepdims=True)
        s = jnp.where(ids == ix, -jnp.inf, s)
        store(r, mx, ix)


def _pair_candidates(v0, v1):
    k = P_TOPK
    w = v0.shape[1]
    row8 = lax.broadcasted_iota(I32, (SUBLANES, w), 0)
    sums = [v0[0:1, :] + v1]
    ids = [lax.broadcasted_iota(I32, (k, w), 0)]
    for a in range(1, SUBLANES):
        nb = k // (a + 1)
        sums.append(jnp.where(row8 < nb, v0[a:a + 1, :] + v1[0:SUBLANES, :], -jnp.inf))
        ids.append(row8 + a * k)
    sums.append(v0[SUBLANES:k, :] + v1[0:1, :])
    ids.append((row8 + SUBLANES) * k)
    return jnp.concatenate(sums, axis=0), jnp.concatenate(ids, axis=0)


def _merge_kernel(hm_ref, hf_ref, gm_ref, gf_ref, x_ref, wm_ref, wf_ref, wo_ref, g2_ref,
                  wpq_ref, sk_ref, x1_ref, xn_ref, ids_i_ref, ids_j_ref, gate_ref,
                  sc_sc, val_sc, idx_sc, top_sc, pos_sc, oi_sc, oj_sc, og_sc, *, tm, tc):
    ym = jnp.dot(hm_ref[...], wm_ref[...], preferred_element_type=F32)
    yf = jnp.dot(hf_ref[...], wf_ref[...], preferred_element_type=F32)
    y = (_sigmoid(gm_ref[...].astype(F32)) * ym + _sigmoid(gf_ref[...].astype(F32)) * yf)
    x1 = x_ref[...] + jnp.dot(y.astype(BF16), wo_ref[...], preferred_element_type=F32)
    x1_ref[...] = x1
    r = lax.rsqrt(jnp.mean(x1 * x1, axis=-1, keepdims=True) + EPS)
    xn = ((x1 * r) * g2_ref[...]).astype(BF16)
    xn_ref[...] = xn
    qh = jnp.dot(xn, wpq_ref[...], preferred_element_type=F32).astype(BF16)

    for hc in range(2 * P_HEADS):
        sc_sc[hc] = lax.dot_general(sk_ref[hc], qh[:, hc * P_HALF:(hc + 1) * P_HALF],
                                    (((1,), (1,)), ((), ())), preferred_element_type=F32)

    def chunk_body(c, carry):
        c0 = pl.multiple_of(c * tc, tc)

        def first_level(hc, carry2):
            s = sc_sc[hc, :, pl.ds(c0, tc)]

            def store(r, val, idx):
                val_sc[hc, r:r + 1, :] = val
                idx_sc[hc, r:r + 1, :] = idx
            _topk_rows(s, P_TOPK, store)
            return carry2
        lax.fori_loop(0, 2 * P_HEADS, first_level, 0)

        def second_level(hd, carry2):
            v0 = val_sc[2 * hd]
            v1 = val_sc[2 * hd + 1]
            i0 = idx_sc[2 * hd]
            i1 = idx_sc[2 * hd + 1]
            cand, cand_id = _pair_candidates(v0, v1)

            def store(r, val, idx):
                top_sc[r:r + 1, :] = val
                pos_sc[r:r + 1, :] = idx
            _topk_rows(cand, P_TOPK, store, cand_id)
            top = top_sc[...]
            pos = pos_sc[...]
            pa = pos >> 4
            pb = pos & (P_TOPK - 1)
            sel_i = jnp.zeros(pos.shape, I32)
            sel_j = jnp.zeros(pos.shape, I32)
            for a in range(P_TOPK):
                sel_i = jnp.where(pa == a, i0[a:a + 1, :], sel_i)
                sel_j = jnp.where(pb == a, i1[a:a + 1, :], sel_j)
            e = jnp.exp(top - top[0:1, :])
            gate = e / jnp.sum(e, axis=0, keepdims=True)
            r0 = pl.multiple_of(hd * P_TOPK, P_TOPK)
            oi_sc[pl.ds(r0, P_TOPK), :] = sel_i
            oj_sc[pl.ds(r0, P_TOPK), :] = sel_j
            og_sc[pl.ds(r0, P_TOPK), :] = gate
            return carry2
        lax.fori_loop(0, P_HEADS, second_level, 0)

        ids_i_ref[pl.ds(c0, tc), :] = oi_sc[...].T
        ids_j_ref[pl.ds(c0, tc), :] = oj_sc[...].T
        gate_ref[pl.ds(c0, tc), :] = og_sc[...].T
        return carry
    lax.fori_loop(0, tm // tc, chunk_body, 0)


def _merge(hm, hf, z, x2, w_m, w_f, w_o, g2, w_pq, sk, tm, tc):
    n = x2.shape[0]
    hk = P_HEADS * P_TOPK
    row = lambda i: (i, 0)
    const = lambda i: (0, 0)
    return pl.pallas_call(
        functools.partial(_merge_kernel, tm=tm, tc=tc),
        grid=(n // tm,),
        in_specs=[
            pl.BlockSpec((tm, D_MODEL), row),
            pl.BlockSpec((tm, D_MODEL), row),
            pl.BlockSpec((tm, D_MODEL), lambda i: (i, ZB_GM)),
            pl.BlockSpec((tm, D_MODEL), lambda i: (i, ZB_GF)),
            pl.BlockSpec((tm, D_MODEL), row),
            pl.BlockSpec((D_MODEL, D_MODEL), const),
            pl.BlockSpec((D_MODEL, D_MODEL), const),
            pl.BlockSpec((D_MODEL, D_MODEL), const),
            pl.BlockSpec((1, D_MODEL), const),
            pl.BlockSpec((D_MODEL, 2 * P_HEADS * P_HALF), const),
            pl.BlockSpec((2 * P_HEADS, P_KEYS, P_HALF), lambda i: (0, 0, 0)),
        ],
        out_specs=[
            pl.BlockSpec((tm, D_MODEL), row),
            pl.BlockSpec((tm, D_MODEL), row),
            pl.BlockSpec((tm, hk), row),
            pl.BlockSpec((tm, hk), row),
            pl.BlockSpec((tm, hk), row),
        ],
        out_shape=[
            jax.ShapeDtypeStruct((n, D_MODEL), F32),
            jax.ShapeDtypeStruct((n, D_MODEL), BF16),
            jax.ShapeDtypeStruct((n, hk), I32),
            jax.ShapeDtypeStruct((n, hk), I32),
            jax.ShapeDtypeStruct((n, hk), F32),
        ],
        scratch_shapes=[
            pltpu.VMEM((2 * P_HEADS, P_KEYS, tm), F32),
            pltpu.VMEM((2 * P_HEADS, P_TOPK, tc), F32),
            pltpu.VMEM((2 * P_HEADS, P_TOPK, tc), I32),
            pltpu.VMEM((P_TOPK, tc), F32),
            pltpu.VMEM((P_TOPK, tc), I32),
            pltpu.VMEM((hk, tc), I32),
            pltpu.VMEM((hk, tc), I32),
            pltpu.VMEM((hk, tc), F32),
        ],
        compiler_params=_cparams(("parallel",)),
        name="merge_retrieve",
    )(hm, hf, z, z, x2, w_m, w_f, w_o, g2, w_pq, sk)


U_SUB = 512


def _peer_u_kernel(xn_ref, u_ref, ii_ref, jj_ref, gate_ref, hw_ref, ids_ref, acc_sc, *, eb):
    e = pl.program_id(1)
    ne = pl.num_programs(1)

    @pl.when(e == 0)
    def _():
        acc_sc[...] = jnp.zeros_like(acc_sc)

    xn = xn_ref[...]
    ii = ii_ref[...]
    jj = jj_ref[...]
    acc = acc_sc[...]
    for sb in range(eb // U_SUB):
        a = lax.dot_general(xn, u_ref[sb * U_SUB:(sb + 1) * U_SUB, :], (((1,), (1,)), ((), ())),
                            preferred_element_type=F32)
        for c in range(U_SUB // P_KEYS):
            blk = e * (eb // P_KEYS) + sb * (U_SUB // P_KEYS) + c
            g = jnp.take_along_axis(a[:, c * P_KEYS:(c + 1) * P_KEYS], jj, axis=1,
                                    mode="promise_in_bounds")
            acc = jnp.where(ii == blk, g, acc)
    acc_sc[...] = acc

    @pl.when(e == ne - 1)
    def _():
        av = acc_sc[...]
        gelu = 0.5 * av * (1.0 + lax.erf(av * (2.0 ** -0.5)))
        hw_ref[...] = gate_ref[...] * gelu
        ids_ref[...] = (ii * P_KEYS + jj) * (D_MODEL // (2 * LANES))


def _peer_u(xn, u_bf, ids_i, ids_j, gate, tm, eb):
    n = xn.shape[0]
    hk = P_HEADS * P_TOPK
    row = lambda i, e: (i, 0)
    return pl.pallas_call(
        functools.partial(_peer_u_kernel, eb=eb),
        grid=(n // tm, N_EXPERTS // eb),
        in_specs=[
            pl.BlockSpec((tm, D_MODEL), row),
            pl.BlockSpec((eb, D_MODEL), lambda i, e: (e, 0)),
            pl.BlockSpec((tm, hk), row),
            pl.BlockSpec((tm, hk), row),
            pl.BlockSpec((tm, hk), row),
        ],
        out_specs=[pl.BlockSpec((tm, hk), row), pl.BlockSpec((tm, hk), row)],
        out_shape=[jax.ShapeDtypeStruct((n, hk), F32), jax.ShapeDtypeStruct((n, hk), I32)],
        scratch_shapes=[pltpu.VMEM((tm, hk), F32)],
        compiler_params=_cparams(("parallel", "arbitrary")),
        name="peer_u",
    )(xn, u_bf, ids_i, ids_j, gate)


V_ROW_WORDS = D_MODEL // (2 * LANES)


V_SLABS = 4
V_GROUP = 16
V_ID_SPLIT = 4


def _peer_v_kernel(*refs, tt):
    id_refs = refs[:V_ID_SPLIT]
    w_ref, x1_ref, tab_ref, out_ref, slab_sc = refs[V_ID_SPLIT:]
    hk = P_HEADS * P_TOPK
    per = hk // V_ID_SPLIT
    tg = V_GROUP
    plane = tg + SUBLANES
    half = D_MODEL // 2

    def group(g, carry):
        t0 = pl.multiple_of(g * tg, tg)
        wg = w_ref[pl.ds(t0, tg), :]
        acc_lo = [jnp.zeros((tg, LANES), F32) for _ in range(V_ROW_WORDS)]
        acc_hi = [jnp.zeros((tg, LANES), F32) for _ in range(V_ROW_WORDS)]
        for n_slot in range(hk):
            q, k = divmod(n_slot, V_ID_SPLIT)
            m = k * per + q
            slab = slab_sc.at[n_slot % V_SLABS]
            for t in range(tg):
                r0 = pl.multiple_of(id_refs[k][t0 * per + (t * per + q)], V_ROW_WORDS)
                slab[pl.ds(t, V_ROW_WORDS, stride=plane), :] = tab_ref[pl.ds(r0, V_ROW_WORDS), :]
            wcol = wg[:, m:m + 1]
            for s in range(V_ROW_WORDS):
                words = slab[s * plane:s * plane + tg, :]
                acc_lo[s] = acc_lo[s] + wcol * lax.bitcast_convert_type(words << 16, F32)
                acc_hi[s] = acc_hi[s] + wcol * lax.bitcast_convert_type(words & jnp.int32(-65536), F32)
        for s in range(V_ROW_WORDS):
            lo = slice(s * LANES, (s + 1) * LANES)
            hi = slice(half + s * LANES, half + (s + 1) * LANES)
            out_ref[pl.ds(t0, tg), lo] = x1_ref[pl.ds(t0, tg), lo] + acc_lo[s]
            out_ref[pl.ds(t0, tg), hi] = x1_ref[pl.ds(t0, tg), hi] + acc_hi[s]
        return carry
    lax.fori_loop(0, tt // tg, group, 0)


def _peer_v(ids, hw, x1, v_packed, tt):
    n = ids.shape[0]
    hk = P_HEADS * P_TOPK
    per = hk // V_ID_SPLIT
    id_parts = [ids[:, k * per:(k + 1) * per].reshape(n * per) for k in range(V_ID_SPLIT)]
    return pl.pallas_call(
        functools.partial(_peer_v_kernel, tt=tt),
        grid=(n // tt,),
        in_specs=[pl.BlockSpec((tt * per,), lambda i: (i,), memory_space=pltpu.SMEM)] * V_ID_SPLIT + [
            pl.BlockSpec((tt, hk), lambda i: (i, 0)),
            pl.BlockSpec((tt, D_MODEL), lambda i: (i, 0)),
            pl.BlockSpec((N_EXPERTS * V_ROW_WORDS, LANES), lambda i: (0, 0),
                         pipeline_mode=pl.Buffered(1)),
        ],
        out_specs=pl.BlockSpec((tt, D_MODEL), lambda i: (i, 0)),
        out_shape=jax.ShapeDtypeStruct((n, D_MODEL), F32),
        scratch_shapes=[pltpu.VMEM((V_SLABS, V_ROW_WORDS * (V_GROUP + SUBLANES), LANES), I32)],
        compiler_params=_cparams(("arbitrary",)),
        name="peer_v",
    )(*id_parts, hw, x1, v_packed)


def _pack_v(v_tab):
    vb = lax.bitcast_convert_type(v_tab.astype(BF16), jnp.uint16).astype(jnp.uint32)
    half = D_MODEL // 2
    packed = vb[:, :half] | (vb[:, half:] << 16)
    return lax.bitcast_convert_type(packed, I32).reshape(N_EXPERTS * V_ROW_WORDS, LANES)


def _layer(x2, batch, seq, norm1_g, w_in, b_in, conv_w, m_norm_g, qn_g, kn_g, w_m_out, w_f_out,
           w_out, norm2_g, w_pq, sub_keys, u_tab, v_tab):
    o = _OFFS
    cols = lambda a: jnp.concatenate([a[..., o[0]:o[4]], a[..., o[6]:o[9]], a[..., o[10]:o[12]]], axis=-1)
    small = lambda a: jnp.concatenate([a[..., o[4]:o[6]], a[..., o[9]:o[10]]], axis=-1)
    w_main = cols(w_in).astype(BF16)
    b_main = cols(b_in)[None, :]
    ws = small(w_in)
    bs = small(b_in)
    pad = LANES - ws.shape[1]
    ws = jnp.pad(ws, ((0, 0), (0, pad))).astype(BF16)
    bs = jnp.pad(bs, (0, pad))

    tm_a = min(1024, x2.shape[0])
    z, gates_col, gates_row = _inproj(x2, norm1_g[None, :], w_main, b_main, ws, ws.T,
                                      bs[None, :], bs[:, None], tm_a)
    ccol = _forget_cumsum(gates_row, batch, seq)
    chunk = min(256, seq)
    hm = _mlstm(z, gates_col, gates_row, conv_w, m_norm_g[None, :], batch, seq, chunk)
    tq = min(512, seq)
    hf = _fox(z, ccol, qn_g[None, :], kn_g[None, :], batch, seq, tq, 2)

    sk = sub_keys.reshape(2 * P_HEADS, P_KEYS, P_HALF).astype(BF16)
    tm_d = min(512, x2.shape[0])
    x1, xn, ids_i, ids_j, gate = _merge(hm, hf, z, x2, w_m_out.astype(BF16), w_f_out.astype(BF16),
                                        w_out.astype(BF16), norm2_g[None, :], w_pq.astype(BF16), sk,
                                        tm_d, 512)
    tm_e = min(1024, x2.shape[0])
    hw, ids = _peer_u(xn, u_tab.astype(BF16), ids_i, ids_j, gate, tm_e, 2048)
    return _peer_v(ids, hw, x1, _pack_v(v_tab), min(128, x1.shape[0]))


def kernel(x, norm1_g, w_in, b_in, conv_w, m_norm_g, qn_g, kn_g, w_m_out, w_f_out, w_out,
           norm2_g, w_pq, sub_keys, u_tab, v_tab):
    batch, seq, d = x.shape
    x2 = x.reshape(batch * seq, d)
    for l in range(w_in.shape[0]):
        x2 = _layer(x2, batch, seq, norm1_g[l], w_in[l], b_in[l], conv_w[l], m_norm_g[l], qn_g[l],
                    kn_g[l], w_m_out[l], w_f_out[l], w_out[l], norm2_g[l], w_pq[l], sub_keys[l],
                    u_tab[l], v_tab[l])
    return x2.reshape(batch, seq, d)
```

```python
import functools

import jax
import jax.numpy as jnp
from jax import lax
from jax.experimental import pallas as pl
from jax.experimental.pallas import tpu as pltpu

F32 = jnp.float32
BF16 = jnp.bfloat16
I32 = jnp.int32

D_MODEL = 1024
M_HEADS = 4
M_HEAD_DIM = 256
F_HEADS = 8
F_HEAD_DIM = 128
CONV_WIDTH = 4
P_HEADS = 8
P_KEYS = 128
P_TOPK = 16
P_HALF = 128
N_EXPERTS = P_KEYS * P_KEYS
EPS = 1e-6
M_INIT = -1e30
LANES = 128
SUBLANES = 8
VMEM_LIMIT = 56 * 1024 * 1024

_SIZES = (1024, 1024, 1024, 1024, M_HEADS, M_HEADS, 1024, 1024, 1024, F_HEADS, 1024, 1024)
_OFFS = [0]
for _s in _SIZES:
    _OFFS.append(_OFFS[-1] + _s)
Z_COLS = 9 * 1024
ZB_MQ, ZB_MK, ZB_MV, ZB_MO, ZB_FQ, ZB_FK, ZB_FV, ZB_GM, ZB_GF = range(9)
G_MI, G_MF, G_FF = 0, M_HEADS, 2 * M_HEADS


def _cparams(sem):
    return pltpu.CompilerParams(dimension_semantics=sem, vmem_limit_bytes=VMEM_LIMIT)


def _log_sigmoid(x):
    return jnp.minimum(x, 0.0) - jnp.log1p(jnp.exp(-jnp.abs(x)))


def _sigmoid(x):
    return 1.0 / (1.0 + jnp.exp(-x))


def _inproj_kernel(x_ref, g_ref, w_ref, b_ref, ws_ref, wst_ref, bs_ref, bst_ref,
                   z_ref, gc_ref, gr_ref, h_sc):
    j = pl.program_id(1)

    @pl.when(j == 0)
    def _():
        x = x_ref[...]
        r = lax.rsqrt(jnp.mean(x * x, axis=-1, keepdims=True) + EPS)
        h = ((x * r) * g_ref[...]).astype(BF16)
        h_sc[...] = h
        gc_ref[...] = jnp.dot(h, ws_ref[...], preferred_element_type=F32) + bs_ref[...]
        gr_ref[...] = lax.dot_general(wst_ref[...], h, (((1,), (1,)), ((), ())),
                                      preferred_element_type=F32) + bst_ref[...]

    z = jnp.dot(h_sc[...], w_ref[...], preferred_element_type=F32) + b_ref[...]
    z_ref[...] = z.astype(BF16)


def _inproj(x2, g1, w_main, b_main, w_small, w_small_t, b_small, b_small_t, tm):
    n = x2.shape[0]
    tn = 1024
    grid = (n // tm, Z_COLS // tn)
    return pl.pallas_call(
        _inproj_kernel,
        grid=grid,
        in_specs=[
            pl.BlockSpec((tm, D_MODEL), lambda i, j: (i, 0)),
            pl.BlockSpec((1, D_MODEL), lambda i, j: (0, 0)),
            pl.BlockSpec((D_MODEL, tn), lambda i, j: (0, j)),
            pl.BlockSpec((1, tn), lambda i, j: (0, j)),
            pl.BlockSpec((D_MODEL, LANES), lambda i, j: (0, 0)),
            pl.BlockSpec((LANES, D_MODEL), lambda i, j: (0, 0)),
            pl.BlockSpec((1, LANES), lambda i, j: (0, 0)),
            pl.BlockSpec((LANES, 1), lambda i, j: (0, 0)),
        ],
        out_specs=[
            pl.BlockSpec((tm, tn), lambda i, j: (i, j)),
            pl.BlockSpec((tm, LANES), lambda i, j: (i, 0)),
            pl.BlockSpec((LANES, tm), lambda i, j: (0, i)),
        ],
        out_shape=[
            jax.ShapeDtypeStruct((n, Z_COLS), BF16),
            jax.ShapeDtypeStruct((n, LANES), F32),
            jax.ShapeDtypeStruct((LANES, n), F32),
        ],
        scratch_shapes=[pltpu.VMEM((tm, D_MODEL), BF16)],
        compiler_params=_cparams(("parallel", "arbitrary")),
        name="inproj",
    )(x2, g1, w_main, b_main, w_small, w_small_t, b_small, b_small_t)


def _cumsum_kernel(gr_ref, ccol_ref):
    x = _log_sigmoid(gr_ref[...])
    t = x.shape[1]
    lane = lax.broadcasted_iota(I32, x.shape, 1)
    s = 1
    while s < t:
        x = x + jnp.where(lane >= s, pltpu.roll(x, s, axis=1), 0.0)
        s *= 2
    ccol_ref[...] = x.T


def _forget_cumsum(gates_row, batch, seq):
    n = batch * seq
    return pl.pallas_call(
        _cumsum_kernel,
        grid=(batch,),
        in_specs=[pl.BlockSpec((LANES, seq), lambda b: (0, b))],
        out_specs=pl.BlockSpec((seq, LANES), lambda b: (b, 0)),
        out_shape=jax.ShapeDtypeStruct((n, LANES), F32),
        compiler_params=_cparams(("parallel",)),
        name="forget_cumsum",
    )(gates_row)


def _mlstm_kernel(q_ref, k_ref, v_ref, o_ref, gc_ref, gr_ref, cw_ref, mg_ref, out_ref,
                  c_sc, n_sc, m_sc, ext_sc, *, chunk):
    L = chunk
    dh = M_HEAD_DIM
    t = pl.program_id(1)

    @pl.when(t == 0)
    def _():
        c_sc[...] = jnp.zeros_like(c_sc)
        n_sc[...] = jnp.zeros_like(n_sc)
        m_sc[...] = jnp.full_like(m_sc, M_INIT)
        ext_sc[0:SUBLANES, :] = jnp.zeros((SUBLANES, 2 * D_MODEL), F32)

    ext_sc[SUBLANES:SUBLANES + L, 0:D_MODEL] = q_ref[...].astype(F32)
    ext_sc[SUBLANES:SUBLANES + L, D_MODEL:2 * D_MODEL] = k_ref[...].astype(F32)

    gc = gc_ref[...]
    gr = gr_ref[...]
    row_i = lax.broadcasted_iota(I32, (L, L), 0)
    col_i = lax.broadcasted_iota(I32, (L, L), 1)
    tril = col_i <= row_i

    def conv_silu(c0):
        base = SUBLANES - (CONV_WIDTH - 1)
        acc = ext_sc[base:base + L, c0:c0 + dh] * cw_ref[0:1, c0:c0 + dh]
        for j in range(1, CONV_WIDTH):
            acc = acc + ext_sc[base + j:base + j + L, c0:c0 + dh] * cw_ref[j:j + 1, c0:c0 + dh]
        return acc * _sigmoid(acc)

    for h in range(M_HEADS):
        q = conv_silu(h * dh) * (dh ** -0.5)
        k = conv_silu(D_MODEL + h * dh)
        v = v_ref[:, h * dh:(h + 1) * dh]
        qb = q.astype(BF16)
        kb = k.astype(BF16)

        i_col = gc[:, G_MI + h:G_MI + h + 1]
        lf_col = _log_sigmoid(gc[:, G_MF + h:G_MF + h + 1])
        i_row = gr[G_MI + h:G_MI + h + 1, :]
        lf_row = _log_sigmoid(gr[G_MF + h:G_MF + h + 1, :])

        b_col = jnp.sum(jnp.where(tril, lf_row, 0.0), axis=1, keepdims=True)
        b_row = jnp.sum(jnp.where(row_i <= col_i, lf_col, 0.0), axis=0, keepdims=True)
        g = jnp.sum(lf_row, axis=1, keepdims=True)
        m_prev = m_sc[h][0:1, 0:1]

        dmat = jnp.where(tril, b_col - b_row + i_row, -jnp.inf)
        inter = b_col + m_prev
        m_t = jnp.maximum(jnp.max(dmat, axis=1, keepdims=True), inter)
        p = jnp.exp(dmat - m_t)
        s = lax.dot_general(qb, kb, (((1,), (1,)), ((), ())), preferred_element_type=F32) * p
        w_inter = jnp.exp(inter - m_t)
        c_old = c_sc[h]
        n_old = n_sc[h]
        num = (jnp.dot(s.astype(BF16), v, preferred_element_type=F32)
               + w_inter * jnp.dot(qb, c_old.astype(BF16), preferred_element_type=F32))
        den = (jnp.sum(s, axis=1, keepdims=True)
               + w_inter * jnp.sum(q * n_old, axis=1, keepdims=True))
        hh = num / jnp.maximum(jnp.abs(den), jnp.exp(-m_t))

        log_w = g - b_col + i_col
        m_new = jnp.maximum(g + m_prev, jnp.max(log_w, axis=0, keepdims=True))
        w_s = jnp.exp(log_w - m_new)
        decay = jnp.exp(g + m_prev - m_new)
        wv = (w_s * v.astype(F32)).astype(BF16)
        c_sc[h] = decay * c_old + lax.dot_general(kb, wv, (((0,), (0,)), ((), ())),
                                                  preferred_element_type=F32)
        n_sc[h] = decay * n_old + jnp.sum(w_s * k, axis=0, keepdims=True)
        m_sc[h] = jnp.broadcast_to(m_new, (SUBLANES, LANES))

        r = lax.rsqrt(jnp.mean(hh * hh, axis=-1, keepdims=True) + EPS)
        hn = (hh * r) * mg_ref[0:1, h * dh:(h + 1) * dh]
        og = _sigmoid(o_ref[:, h * dh:(h + 1) * dh].astype(F32))
        out_ref[:, h * dh:(h + 1) * dh] = (hn * og).astype(BF16)

    ext_sc[0:SUBLANES, :] = ext_sc[L:L + SUBLANES, :]


def _mlstm(z, gates_col, gates_row, conv_w, m_norm_g, batch, seq, chunk):
    n = batch * seq
    nt = seq // chunk
    zspec = lambda cb: pl.BlockSpec((chunk, D_MODEL), lambda b, t, cb=cb: (b * nt + t, cb))
    return pl.pallas_call(
        functools.partial(_mlstm_kernel, chunk=chunk),
        grid=(batch, nt),
        in_specs=[
            zspec(ZB_MQ), zspec(ZB_MK), zspec(ZB_MV), zspec(ZB_MO),
            pl.BlockSpec((chunk, LANES), lambda b, t: (b * nt + t, 0)),
            pl.BlockSpec((LANES, chunk), lambda b, t: (0, b * nt + t)),
            pl.BlockSpec((CONV_WIDTH, 2 * D_MODEL), lambda b, t: (0, 0)),
            pl.BlockSpec((1, D_MODEL), lambda b, t: (0, 0)),
        ],
        out_specs=pl.BlockSpec((chunk, D_MODEL), lambda b, t: (b * nt + t, 0)),
        out_shape=jax.ShapeDtypeStruct((n, D_MODEL), BF16),
        scratch_shapes=[
            pltpu.VMEM((M_HEADS, M_HEAD_DIM, M_HEAD_DIM), F32),
            pltpu.VMEM((M_HEADS, 1, M_HEAD_DIM), F32),
            pltpu.VMEM((M_HEADS, SUBLANES, LANES), F32),
            pltpu.VMEM((chunk + SUBLANES, 2 * D_MODEL), F32),
        ],
        compiler_params=_cparams(("parallel", "arbitrary")),
        name="mlstm",
    )(z, z, z, z, gates_col, gates_row, conv_w, m_norm_g)


LOG2E = 1.4426950408889634
FOX_SAFE_LOG2_RANGE = 100.0


def _split3(c):
    hi = c.astype(BF16).astype(F32)
    r1 = c - hi
    mid = r1.astype(BF16).astype(F32)
    lo = (r1 - mid).astype(BF16).astype(F32)
    return hi, mid, lo


def _fox_aug(c, lane, is_query, shift):
    hi, mid, lo = _split3(c if is_query else -c)
    off = 0 if is_query else 3
    ex = jnp.where(lane == off, hi, jnp.where(lane == off + 1, mid, jnp.where(lane == off + 2, lo, 0.0)))
    ones_at = (lane >= 3) & (lane < 6) if is_query else lane < 3
    ex = jnp.where(ones_at, 1.0, ex)
    return jnp.where(lane == 6, -shift if is_query else 1.0, ex).astype(BF16)


def _fox_kernel(bnd_ref, q_ref, k_ref, v_ref, ccol_ref, qg_ref, kg_ref, out_ref,
                ka_sc, va_sc, m_sc, acc_sc, s_sc, *, tq, seq, hp):
    g = pl.program_id(1)
    i = pl.program_id(2)
    dh = F_HEAD_DIM
    lane = lax.broadcasted_iota(I32, (tq, LANES), 1)
    shift = bnd_ref[0]
    safe = bnd_ref[1] > 0.5

    def c_of(r0, hh):
        cc = ccol_ref[pl.ds(r0, tq), :]
        return jnp.sum(jnp.where(lane == G_FF + g * hp + hh, cc, 0.0), axis=1, keepdims=True) * LOG2E

    @pl.when(i == 0)
    def _():
        def prep_chunk(c, carry):
            r0 = pl.multiple_of(c * tq, tq)
            for hh in range(hp):
                kk = k_ref[pl.ds(r0, tq), hh * dh:(hh + 1) * dh].astype(F32)
                r = lax.rsqrt(jnp.mean(kk * kk, axis=-1, keepdims=True) + EPS)
                ka_sc[hh, pl.ds(r0, tq), 0:dh] = ((kk * r) * kg_ref[...]).astype(BF16)
                ka_sc[hh, pl.ds(r0, tq), dh:2 * dh] = _fox_aug(c_of(r0, hh), lane, False, shift)
                va_sc[hh, pl.ds(r0, tq), 0:dh] = v_ref[pl.ds(r0, tq), hh * dh:(hh + 1) * dh]
                va_sc[hh, pl.ds(r0, tq), dh:2 * dh] = jnp.ones((tq, dh), BF16)
            return carry
        lax.fori_loop(0, seq // tq, prep_chunk, 0)

    q0 = pl.multiple_of(i * tq, tq)
    qa = []
    for hh in range(hp):
        q = q_ref[:, hh * dh:(hh + 1) * dh].astype(F32)
        r = lax.rsqrt(jnp.mean(q * q, axis=-1, keepdims=True) + EPS)
        qn = (((q * r) * qg_ref[...]) * (dh ** -0.5 * LOG2E)).astype(BF16)
        qa.append(jnp.concatenate([qn, _fox_aug(c_of(q0, hh), lane, True, shift)], axis=1))

    acc_sc[...] = jnp.zeros_like(acc_sc)

    def scores(j, masked):
        r0 = pl.multiple_of(j * tq, tq)
        s = [lax.dot_general(qa[hh], ka_sc[hh, pl.ds(r0, tq), :], (((1,), (1,)), ((), ())),
                             preferred_element_type=F32) for hh in range(hp)]
        if masked:
            rr = lax.broadcasted_iota(I32, (tq, tq), 0)
            cc = lax.broadcasted_iota(I32, (tq, tq), 1)
            s = [jnp.where(rr >= cc, sh, -jnp.inf) for sh in s]
        return r0, s

    def fixed_consume(j, s):
        r0 = pl.multiple_of(j * tq, tq)
        for hh in range(hp):
            p = jnp.exp2(s[hh]).astype(BF16)
            acc_sc[hh] += jnp.dot(p, va_sc[hh, pl.ds(r0, tq), :], preferred_element_type=F32)

    def run_fixed():
        def produce(buf, j, masked):
            _, s = scores(j, masked)
            for hh in range(hp):
                s_sc[buf, hh] = s[hh]

        def consume(buf, j):
            fixed_consume(j, [s_sc[buf, hh] for hh in range(hp)])

        produce(0, i, True)
        pairs = i // 2

        def body(k, carry):
            produce(1, 2 * k, False)
            consume(0, jnp.where(k == 0, i, 2 * k - 1))
            produce(0, 2 * k + 1, False)
            consume(1, 2 * k)
            return carry
        lax.fori_loop(0, pairs, body, 0)
        pending = jnp.where(pairs == 0, i, 2 * pairs - 1)

        @pl.when(i % 2 == 1)
        def _():
            produce(1, i - 1, False)
            consume(0, pending)
            consume(1, i - 1)

        @pl.when(i % 2 == 0)
        def _():
            consume(0, pending)

    def running_step(j, masked):
        r0, s = scores(j, masked)
        for hh in range(hp):
            m_old = m_sc[hh]
            m_new = jnp.maximum(m_old, jnp.max(s[hh], axis=1, keepdims=True))
            p = jnp.exp2(s[hh] - m_new).astype(BF16)
            alpha = jnp.exp2(m_old - m_new)
            acc_sc[hh] = alpha * acc_sc[hh] + jnp.dot(p, va_sc[hh, pl.ds(r0, tq), :],
                                                      preferred_element_type=F32)
            m_sc[hh] = m_new

    def run(step):
        def body(j, carry):
            step(j, False)
            return carry
        lax.fori_loop(0, i, body, 0)
        step(i, True)

    @pl.when(safe)
    def _():
        run_fixed()

    @pl.when(jnp.logical_not(safe))
    def _():
        m_sc[...] = jnp.full_like(m_sc, -jnp.inf)
        run(running_step)

    for hh in range(hp):
        out_ref[:, hh * dh:(hh + 1) * dh] = (acc_sc[hh, :, 0:dh] / acc_sc[hh, :, dh:2 * dh]).astype(BF16)


def _fox(z, ccol, qn_g, kn_g, batch, seq, tq, hp):
    n = batch * seq
    nq = seq // tq
    w = hp * F_HEAD_DIM
    fq0 = ZB_FQ * (D_MODEL // w)
    fk0 = ZB_FK * (D_MODEL // w)
    fv0 = ZB_FV * (D_MODEL // w)
    shift = jnp.ceil(1.01 * LOG2E * F_HEAD_DIM ** 0.5 * jnp.max(jnp.abs(qn_g)) * jnp.max(jnp.abs(kn_g)))
    safe = (2.0 * shift <= FOX_SAFE_LOG2_RANGE).astype(F32)
    bnd = jnp.stack([shift, safe]).astype(F32)
    return pl.pallas_call(
        functools.partial(_fox_kernel, tq=tq, seq=seq, hp=hp),
        grid=(batch, F_HEADS // hp, nq),
        in_specs=[
            pl.BlockSpec(memory_space=pltpu.SMEM),
            pl.BlockSpec((tq, w), lambda b, g, i: (b * nq + i, fq0 + g)),
            pl.BlockSpec((seq, w), lambda b, g, i: (b, fk0 + g)),
            pl.BlockSpec((seq, w), lambda b, g, i: (b, fv0 + g)),
            pl.BlockSpec((seq, LANES), lambda b, g, i: (b, 0)),
            pl.BlockSpec((1, F_HEAD_DIM), lambda b, g, i: (0, 0)),
            pl.BlockSpec((1, F_HEAD_DIM), lambda b, g, i: (0, 0)),
        ],
        out_specs=pl.BlockSpec((tq, w), lambda b, g, i: (b * nq + i, g)),
        out_shape=jax.ShapeDtypeStruct((n, D_MODEL), BF16),
        scratch_shapes=[
            pltpu.VMEM((hp, seq, 2 * F_HEAD_DIM), BF16),
            pltpu.VMEM((hp, seq, 2 * F_HEAD_DIM), BF16),
            pltpu.VMEM((hp, tq, 1), F32),
            pltpu.VMEM((hp, tq, 2 * F_HEAD_DIM), F32),
            pltpu.VMEM((2, hp, tq, tq), F32),
        ],
        compiler_params=_cparams(("parallel", "parallel", "arbitrary")),
        name="fox",
    )(bnd, z, z, z, ccol, qn_g, kn_g)


def _topk_rows(s, k, store):
    rows = lax.broadcasted_iota(I32, s.shape, 0)
    for r in range(k):
        mx = jnp.max(s, axis=0, keepdims=True)
        ix = jnp.argmax(s, axis=0, keepdims=True).astype(I32)
        s = jnp.where(rows == ix, -jnp.inf, s)
        store(r, mx, ix)


def _pair_candidates(v0, v1):
    k = P_TOPK
    w = v0.shape[1]
    row8 = lax.broadcasted_iota(I32, (SUBLANES, w), 0)
    sums = [v0[0:1, :] + v1]
    for a in range(1, SUBLANES):
        nb = k // (a + 1)
        sums.append(jnp.where(row8 < nb, v0[a:a + 1, :] + v1[0:SUBLANES, :], -jnp.inf))
    sums.append(v0[SUBLANES:k, :] + v1[0:1, :])
    return jnp.concatenate(sums, axis=0)


def _pair_of_row(row):
    k = P_TOPK
    q = row - k
    mid_a = 1 + (q >> 3)
    mid_b = q & (SUBLANES - 1)
    last = row >= k + SUBLANES * (SUBLANES - 1)
    a = jnp.where(row < k, 0, jnp.where(last, row - (k + SUBLANES * (SUBLANES - 2)), mid_a))
    b = jnp.where(row < k, row, jnp.where(last, 0, mid_b))
    return a, b


def _merge_kernel(hm_ref, hf_ref, gm_ref, gf_ref, x_ref, wm_ref, wf_ref, wo_ref, g2_ref,
                  wpq_ref, sk_ref, x1_ref, xn_ref, ids_i_ref, ids_j_ref, gate_ref,
                  sc_sc, val_sc, idx_sc, top_sc, pos_sc, oi_sc, oj_sc, og_sc, *, tm, tc):
    ym = jnp.dot(hm_ref[...], wm_ref[...], preferred_element_type=F32)
    yf = jnp.dot(hf_ref[...], wf_ref[...], preferred_element_type=F32)
    y = (_sigmoid(gm_ref[...].astype(F32)) * ym + _sigmoid(gf_ref[...].astype(F32)) * yf)
    x1 = x_ref[...] + jnp.dot(y.astype(BF16), wo_ref[...], preferred_element_type=F32)
    x1_ref[...] = x1
    r = lax.rsqrt(jnp.mean(x1 * x1, axis=-1, keepdims=True) + EPS)
    xn = ((x1 * r) * g2_ref[...]).astype(BF16)
    xn_ref[...] = xn
    qh = jnp.dot(xn, wpq_ref[...], preferred_element_type=F32).astype(BF16)

    for hc in range(2 * P_HEADS):
        sc_sc[hc] = lax.dot_general(sk_ref[hc], qh[:, hc * P_HALF:(hc + 1) * P_HALF],
                                    (((1,), (1,)), ((), ())), preferred_element_type=F32)

    def chunk_body(c, carry):
        c0 = pl.multiple_of(c * tc, tc)

        def first_level(hc, carry2):
            s = sc_sc[hc, :, pl.ds(c0, tc)]

            def store(r, val, idx):
                val_sc[hc, r:r + 1, :] = val
                idx_sc[hc, r:r + 1, :] = idx
            _topk_rows(s, P_TOPK, store)
            return carry2
        lax.fori_loop(0, 2 * P_HEADS, first_level, 0)

        def second_level(hd, carry2):
            v0 = val_sc[2 * hd]
            v1 = val_sc[2 * hd + 1]
            i0 = idx_sc[2 * hd]
            i1 = idx_sc[2 * hd + 1]
            cand = _pair_candidates(v0, v1)

            def store(r, val, idx):
                top_sc[r:r + 1, :] = val
                pos_sc[r:r + 1, :] = idx
            _topk_rows(cand, P_TOPK, store)
            top = top_sc[...]
            pa, pb = _pair_of_row(pos_sc[...])
            sel_i = jnp.zeros(pa.shape, I32)
            sel_j = jnp.zeros(pa.shape, I32)
            for a in range(P_TOPK):
                sel_i = jnp.where(pa == a, i0[a:a + 1, :], sel_i)
                sel_j = jnp.where(pb == a, i1[a:a + 1, :], sel_j)
            e = jnp.exp(top - top[0:1, :])
            gate = e / jnp.sum(e, axis=0, keepdims=True)
            r0 = pl.multiple_of(hd * P_TOPK, P_TOPK)
            oi_sc[pl.ds(r0, P_TOPK), :] = sel_i
            oj_sc[pl.ds(r0, P_TOPK), :] = sel_j
            og_sc[pl.ds(r0, P_TOPK), :] = gate
            return carry2
        lax.fori_loop(0, P_HEADS, second_level, 0)

        ids_i_ref[pl.ds(c0, tc), :] = oi_sc[...].T
        ids_j_ref[pl.ds(c0, tc), :] = oj_sc[...].T
        gate_ref[pl.ds(c0, tc), :] = og_sc[...].T
        return carry
    lax.fori_loop(0, tm // tc, chunk_body, 0)


def _merge(hm, hf, z, x2, w_m, w_f, w_o, g2, w_pq, sk, tm, tc):
    n = x2.shape[0]
    hk = P_HEADS * P_TOPK
    row = lambda i: (i, 0)
    const = lambda i: (0, 0)
    return pl.pallas_call(
        functools.partial(_merge_kernel, tm=tm, tc=tc),
        grid=(n // tm,),
        in_specs=[
            pl.BlockSpec((tm, D_MODEL), row),
            pl.BlockSpec((tm, D_MODEL), row),
            pl.BlockSpec((tm, D_MODEL), lambda i: (i, ZB_GM)),
            pl.BlockSpec((tm, D_MODEL), lambda i: (i, ZB_GF)),
            pl.BlockSpec((tm, D_MODEL), row),
            pl.BlockSpec((D_MODEL, D_MODEL), const),
            pl.BlockSpec((D_MODEL, D_MODEL), const),
            pl.BlockSpec((D_MODEL, D_MODEL), const),
            pl.BlockSpec((1, D_MODEL), const),
            pl.BlockSpec((D_MODEL, 2 * P_HEADS * P_HALF), const),
            pl.BlockSpec((2 * P_HEADS, P_KEYS, P_HALF), lambda i: (0, 0, 0)),
        ],
        out_specs=[
            pl.BlockSpec((tm, D_MODEL), row),
            pl.BlockSpec((tm, D_MODEL), row),
            pl.BlockSpec((tm, hk), row),
            pl.BlockSpec((tm, hk), row),
            pl.BlockSpec((tm, hk), row),
        ],
        out_shape=[
            jax.ShapeDtypeStruct((n, D_MODEL), F32),
            jax.ShapeDtypeStruct((n, D_MODEL), BF16),
            jax.ShapeDtypeStruct((n, hk), I32),
            jax.ShapeDtypeStruct((n, hk), I32),
            jax.ShapeDtypeStruct((n, hk), F32),
        ],
        scratch_shapes=[
            pltpu.VMEM((2 * P_HEADS, P_KEYS, tm), F32),
            pltpu.VMEM((2 * P_HEADS, P_TOPK, tc), F32),
            pltpu.VMEM((2 * P_HEADS, P_TOPK, tc), I32),
            pltpu.VMEM((P_TOPK, tc), F32),
            pltpu.VMEM((P_TOPK, tc), I32),
            pltpu.VMEM((hk, tc), I32),
            pltpu.VMEM((hk, tc), I32),
            pltpu.VMEM((hk, tc), F32),
        ],
        compiler_params=_cparams(("parallel",)),
        name="merge_retrieve",
    )(hm, hf, z, z, x2, w_m, w_f, w_o, g2, w_pq, sk)


U_SUB = 512


def _peer_u_kernel(xn_ref, u_ref, ii_ref, jj_ref, gate_ref, hw_ref, ids_ref, acc_sc, *, eb):
    e = pl.program_id(1)
    ne = pl.num_programs(1)

    @pl.when(e == 0)
    def _():
        acc_sc[...] = jnp.zeros_like(acc_sc)

    xn = xn_ref[...]
    ii = ii_ref[...]
    jj = jj_ref[...]
    acc = acc_sc[...]
    for sb in range(eb // U_SUB):
        a = lax.dot_general(xn, u_ref[sb * U_SUB:(sb + 1) * U_SUB, :], (((1,), (1,)), ((), ())),
                            preferred_element_type=F32)
        for c in range(U_SUB // P_KEYS):
            blk = e * (eb // P_KEYS) + sb * (U_SUB // P_KEYS) + c
            g = jnp.take_along_axis(a[:, c * P_KEYS:(c + 1) * P_KEYS], jj, axis=1,
                                    mode="promise_in_bounds")
            acc = jnp.where(ii == blk, g, acc)
    acc_sc[...] = acc

    @pl.when(e == ne - 1)
    def _():
        av = acc_sc[...]
        gelu = 0.5 * av * (1.0 + lax.erf(av * (2.0 ** -0.5)))
        hw_ref[...] = gate_ref[...] * gelu
        ids_ref[...] = (ii * P_KEYS + jj) * (D_MODEL // (2 * LANES))


def _peer_u(xn, u_bf, ids_i, ids_j, gate, tm, eb):
    n = xn.shape[0]
    hk = P_HEADS * P_TOPK
    row = lambda i, e: (i, 0)
    return pl.pallas_call(
        functools.partial(_peer_u_kernel, eb=eb),
        grid=(n // tm, N_EXPERTS // eb),
        in_specs=[
            pl.BlockSpec((tm, D_MODEL), row),
            pl.BlockSpec((eb, D_MODEL), lambda i, e: (e, 0)),
            pl.BlockSpec((tm, hk), row),
            pl.BlockSpec((tm, hk), row),
            pl.BlockSpec((tm, hk), row),
        ],
        out_specs=[pl.BlockSpec((tm, hk), row), pl.BlockSpec((tm, hk), row)],
        out_shape=[jax.ShapeDtypeStruct((n, hk), F32), jax.ShapeDtypeStruct((n, hk), I32)],
        scratch_shapes=[pltpu.VMEM((tm, hk), F32)],
        compiler_params=_cparams(("parallel", "arbitrary")),
        name="peer_u",
    )(xn, u_bf, ids_i, ids_j, gate)


V_ROW_WORDS = D_MODEL // (2 * LANES)


V_SLABS = 128
V_GROUP = 16
V_ID_SPLIT = 4


def _peer_v_kernel(*refs, tt):
    id_refs = refs[:V_ID_SPLIT]
    w_ref, x1_ref, tab_ref, out_ref, slab_sc = refs[V_ID_SPLIT:]
    hk = P_HEADS * P_TOPK
    per = hk // V_ID_SPLIT
    tg = V_GROUP
    plane = tg + SUBLANES
    half = D_MODEL // 2

    def group(g, carry):
        t0 = pl.multiple_of(g * tg, tg)
        wg = w_ref[pl.ds(t0, tg), :]
        acc_lo = [jnp.zeros((tg, LANES), F32) for _ in range(V_ROW_WORDS)]
        acc_hi = [jnp.zeros((tg, LANES), F32) for _ in range(V_ROW_WORDS)]
        for n_slot in range(hk):
            q, k = divmod(n_slot, V_ID_SPLIT)
            m = k * per + q
            slab = slab_sc.at[n_slot % V_SLABS]
            for t in range(tg):
                r0 = pl.multiple_of(id_refs[k][t0 * per + (t * per + q)], V_ROW_WORDS)
                slab[pl.ds(t, V_ROW_WORDS, stride=plane), :] = tab_ref[pl.ds(r0, V_ROW_WORDS), :]
            wcol = wg[:, m:m + 1]
            for s in range(V_ROW_WORDS):
                words = slab[s * plane:s * plane + tg, :]
                acc_lo[s] = acc_lo[s] + wcol * lax.bitcast_convert_type(words << 16, F32)
                acc_hi[s] = acc_hi[s] + wcol * lax.bitcast_convert_type(words & jnp.int32(-65536), F32)
        for s in range(V_ROW_WORDS):
            lo = slice(s * LANES, (s + 1) * LANES)
            hi = slice(half + s * LANES, half + (s + 1) * LANES)
            out_ref[pl.ds(t0, tg), lo] = x1_ref[pl.ds(t0, tg), lo] + acc_lo[s]
            out_ref[pl.ds(t0, tg), hi] = x1_ref[pl.ds(t0, tg), hi] + acc_hi[s]
        return carry
    lax.fori_loop(0, tt // tg, group, 0)


def _peer_v(ids, hw, x1, v_packed, tt):
    n = ids.shape[0]
    hk = P_HEADS * P_TOPK
    per = hk // V_ID_SPLIT
    id_parts = [ids[:, k * per:(k + 1) * per].reshape(n * per) for k in range(V_ID_SPLIT)]
    return pl.pallas_call(
        functools.partial(_peer_v_kernel, tt=tt),
        grid=(n // tt,),
        in_specs=[pl.BlockSpec((tt * per,), lambda i: (i,), memory_space=pltpu.SMEM)] * V_ID_SPLIT + [
            pl.BlockSpec((tt, hk), lambda i: (i, 0)),
            pl.BlockSpec((tt, D_MODEL), lambda i: (i, 0)),
            pl.BlockSpec((N_EXPERTS * V_ROW_WORDS, LANES), lambda i: (0, 0),
                         pipeline_mode=pl.Buffered(1)),
        ],
        out_specs=pl.BlockSpec((tt, D_MODEL), lambda i: (i, 0)),
        out_shape=jax.ShapeDtypeStruct((n, D_MODEL), F32),
        scratch_shapes=[pltpu.VMEM((V_SLABS, V_ROW_WORDS * (V_GROUP + SUBLANES), LANES), I32)],
        compiler_params=_cparams(("arbitrary",)),
        name="peer_v",
    )(*id_parts, hw, x1, v_packed)


def _pack_v(v_tab):
    vb = lax.bitcast_convert_type(v_tab.astype(BF16), jnp.uint16).astype(jnp.uint32)
    half = D_MODEL // 2
    packed = vb[:, :half] | (vb[:, half:] << 16)
    return lax.bitcast_convert_type(packed, I32).reshape(N_EXPERTS * V_ROW_WORDS, LANES)


def _layer(x2, batch, seq, norm1_g, w_in, b_in, conv_w, m_norm_g, qn_g, kn_g, w_m_out, w_f_out,
           w_out, norm2_g, w_pq, sub_keys, u_tab, v_tab):
    o = _OFFS
    cols = lambda a: jnp.concatenate([a[..., o[0]:o[4]], a[..., o[6]:o[9]], a[..., o[10]:o[12]]], axis=-1)
    small = lambda a: jnp.concatenate([a[..., o[4]:o[6]], a[..., o[9]:o[10]]], axis=-1)
    w_main = cols(w_in).astype(BF16)
    b_main = cols(b_in)[None, :]
    ws = small(w_in)
    bs = small(b_in)
    pad = LANES - ws.shape[1]
    ws = jnp.pad(ws, ((0, 0), (0, pad))).astype(BF16)
    bs = jnp.pad(bs, (0, pad))

    tm_a = min(1024, x2.shape[0])
    z, gates_col, gates_row = _inproj(x2, norm1_g[None, :], w_main, b_main, ws, ws.T,
                                      bs[None, :], bs[:, None], tm_a)
    ccol = _forget_cumsum(gates_row, batch, seq)
    chunk = min(256, seq)
    hm = _mlstm(z, gates_col, gates_row, conv_w, m_norm_g[None, :], batch, seq, chunk)
    tq = min(512, seq)
    hf = _fox(z, ccol, qn_g[None, :], kn_g[None, :], batch, seq, tq, 2)

    sk = sub_keys.reshape(2 * P_HEADS, P_KEYS, P_HALF).astype(BF16)
    tm_d = min(512, x2.shape[0])
    x1, xn, ids_i, ids_j, gate = _merge(hm, hf, z, x2, w_m_out.astype(BF16), w_f_out.astype(BF16),
                                        w_out.astype(BF16), norm2_g[None, :], w_pq.astype(BF16), sk,
                                        tm_d, 512)
    tm_e = min(1024, x2.shape[0])
    hw, ids = _peer_u(xn, u_tab.astype(BF16), ids_i, ids_j, gate, tm_e, 2048)
    return _peer_v(ids, hw, x1, _pack_v(v_tab), min(128, x1.shape[0]))


def kernel(x, norm1_g, w_in, b_in, conv_w, m_norm_g, qn_g, kn_g, w_m_out, w_f_out, w_out,
           norm2_g, w_pq, sub_keys, u_tab, v_tab):
    batch, seq, d = x.shape
    x2 = x.reshape(batch * seq, d)
    for l in range(w_in.shape[0]):
        x2 = _layer(x2, batch, seq, norm1_g[l], w_in[l], b_in[l], conv_w[l], m_norm_g[l], qn_g[l],
                    kn_g[l], w_m_out[l], w_f_out[l], w_out[l], norm2_g[l], w_pq[l], sub_keys[l],
                    u_tab[l], v_tab[l])
    return x2.reshape(batch, seq, d)
```

```python
import functools

import jax
import jax.numpy as jnp
from jax import lax
from jax.experimental import pallas as pl
from jax.experimental.pallas import tpu as pltpu

F32 = jnp.float32
BF16 = jnp.bfloat16
I32 = jnp.int32

D_MODEL = 1024
M_HEADS = 4
M_HEAD_DIM = 256
F_HEADS = 8
F_HEAD_DIM = 128
CONV_WIDTH = 4
P_HEADS = 8
P_KEYS = 128
P_TOPK = 16
P_HALF = 128
N_EXPERTS = P_KEYS * P_KEYS
EPS = 1e-6
M_INIT = -1e30
LANES = 128
SUBLANES = 8
VMEM_LIMIT = 56 * 1024 * 1024

_SIZES = (1024, 1024, 1024, 1024, M_HEADS, M_HEADS, 1024, 1024, 1024, F_HEADS, 1024, 1024)
_OFFS = [0]
for _s in _SIZES:
    _OFFS.append(_OFFS[-1] + _s)
Z_COLS = 9 * 1024
ZB_MQ, ZB_MK, ZB_MV, ZB_MO, ZB_FQ, ZB_FK, ZB_FV, ZB_GM, ZB_GF = range(9)
G_MI, G_MF, G_FF = 0, M_HEADS, 2 * M_HEADS


def _cparams(sem):
    return pltpu.CompilerParams(dimension_semantics=sem, vmem_limit_bytes=VMEM_LIMIT)


def _log_sigmoid(x):
    return jnp.minimum(x, 0.0) - jnp.log1p(jnp.exp(-jnp.abs(x)))


def _sigmoid(x):
    return 1.0 / (1.0 + jnp.exp(-x))


def _inproj_kernel(x_ref, g_ref, w_ref, b_ref, ws_ref, wst_ref, bs_ref, bst_ref,
                   z_ref, gc_ref, gr_ref, h_sc):
    j = pl.program_id(1)

    @pl.when(j == 0)
    def _():
        x = x_ref[...]
        r = lax.rsqrt(jnp.mean(x * x, axis=-1, keepdims=True) + EPS)
        h = ((x * r) * g_ref[...]).astype(BF16)
        h_sc[...] = h
        gc_ref[...] = jnp.dot(h, ws_ref[...], preferred_element_type=F32) + bs_ref[...]
        gr_ref[...] = lax.dot_general(wst_ref[...], h, (((1,), (1,)), ((), ())),
                                      preferred_element_type=F32) + bst_ref[...]

    z = jnp.dot(h_sc[...], w_ref[...], preferred_element_type=F32) + b_ref[...]
    z_ref[...] = z.astype(BF16)


def _inproj(x2, g1, w_main, b_main, w_small, w_small_t, b_small, b_small_t, tm):
    n = x2.shape[0]
    tn = 3072
    grid = (n // tm, Z_COLS // tn)
    return pl.pallas_call(
        _inproj_kernel,
        grid=grid,
        in_specs=[
            pl.BlockSpec((tm, D_MODEL), lambda i, j: (i, 0)),
            pl.BlockSpec((1, D_MODEL), lambda i, j: (0, 0)),
            pl.BlockSpec((D_MODEL, tn), lambda i, j: (0, j)),
            pl.BlockSpec((1, tn), lambda i, j: (0, j)),
            pl.BlockSpec((D_MODEL, LANES), lambda i, j: (0, 0)),
            pl.BlockSpec((LANES, D_MODEL), lambda i, j: (0, 0)),
            pl.BlockSpec((1, LANES), lambda i, j: (0, 0)),
            pl.BlockSpec((LANES, 1), lambda i, j: (0, 0)),
        ],
        out_specs=[
            pl.BlockSpec((tm, tn), lambda i, j: (i, j)),
            pl.BlockSpec((tm, LANES), lambda i, j: (i, 0)),
            pl.BlockSpec((LANES, tm), lambda i, j: (0, i)),
        ],
        out_shape=[
            jax.ShapeDtypeStruct((n, Z_COLS), BF16),
            jax.ShapeDtypeStruct((n, LANES), F32),
            jax.ShapeDtypeStruct((LANES, n), F32),
        ],
        scratch_shapes=[pltpu.VMEM((tm, D_MODEL), BF16)],
        compiler_params=_cparams(("parallel", "arbitrary")),
        name="inproj",
    )(x2, g1, w_main, b_main, w_small, w_small_t, b_small, b_small_t)


def _cumsum_kernel(gr_ref, ccol_ref):
    x = _log_sigmoid(gr_ref[...])
    t = x.shape[1]
    lane = lax.broadcasted_iota(I32, x.shape, 1)
    s = 1
    while s < t:
        x = x + jnp.where(lane >= s, pltpu.roll(x, s, axis=1), 0.0)
        s *= 2
    ccol_ref[...] = x.T


def _forget_cumsum(gates_row, batch, seq):
    n = batch * seq
    return pl.pallas_call(
        _cumsum_kernel,
        grid=(batch,),
        in_specs=[pl.BlockSpec((LANES, seq), lambda b: (0, b))],
        out_specs=pl.BlockSpec((seq, LANES), lambda b: (b, 0)),
        out_shape=jax.ShapeDtypeStruct((n, LANES), F32),
        compiler_params=_cparams(("parallel",)),
        name="forget_cumsum",
    )(gates_row)


def _mlstm_kernel(q_ref, k_ref, v_ref, o_ref, gc_ref, gr_ref, cw_ref, mg_ref, out_ref,
                  c_sc, n_sc, m_sc, ext_sc, *, chunk):
    L = chunk
    dh = M_HEAD_DIM
    t = pl.program_id(1)

    @pl.when(t == 0)
    def _():
        c_sc[...] = jnp.zeros_like(c_sc)
        n_sc[...] = jnp.zeros_like(n_sc)
        m_sc[...] = jnp.full_like(m_sc, M_INIT)
        ext_sc[0:SUBLANES, :] = jnp.zeros((SUBLANES, 2 * D_MODEL), F32)

    ext_sc[SUBLANES:SUBLANES + L, 0:D_MODEL] = q_ref[...].astype(F32)
    ext_sc[SUBLANES:SUBLANES + L, D_MODEL:2 * D_MODEL] = k_ref[...].astype(F32)

    gc = gc_ref[...]
    gr = gr_ref[...]
    row_i = lax.broadcasted_iota(I32, (L, L), 0)
    col_i = lax.broadcasted_iota(I32, (L, L), 1)
    tril = col_i <= row_i

    def conv_silu(c0):
        base = SUBLANES - (CONV_WIDTH - 1)
        acc = ext_sc[base:base + L, c0:c0 + dh] * cw_ref[0:1, c0:c0 + dh]
        for j in range(1, CONV_WIDTH):
            acc = acc + ext_sc[base + j:base + j + L, c0:c0 + dh] * cw_ref[j:j + 1, c0:c0 + dh]
        return acc * _sigmoid(acc)

    for h in range(M_HEADS):
        q = conv_silu(h * dh) * (dh ** -0.5)
        k = conv_silu(D_MODEL + h * dh)
        v = v_ref[:, h * dh:(h + 1) * dh]
        qb = q.astype(BF16)
        kb = k.astype(BF16)

        i_col = gc[:, G_MI + h:G_MI + h + 1]
        lf_col = _log_sigmoid(gc[:, G_MF + h:G_MF + h + 1])
        i_row = gr[G_MI + h:G_MI + h + 1, :]
        lf_row = _log_sigmoid(gr[G_MF + h:G_MF + h + 1, :])

        b_col = jnp.sum(jnp.where(tril, lf_row, 0.0), axis=1, keepdims=True)
        b_row = jnp.sum(jnp.where(row_i <= col_i, lf_col, 0.0), axis=0, keepdims=True)
        g = jnp.sum(lf_row, axis=1, keepdims=True)
        m_prev = m_sc[h][0:1, 0:1]

        dmat = jnp.where(tril, b_col - b_row + i_row, -jnp.inf)
        inter = b_col + m_prev
        m_t = jnp.maximum(jnp.max(dmat, axis=1, keepdims=True), inter)
        p = jnp.exp(dmat - m_t)
        s = lax.dot_general(qb, kb, (((1,), (1,)), ((), ())), preferred_element_type=F32) * p
        w_inter = jnp.exp(inter - m_t)
        c_old = c_sc[h]
        n_old = n_sc[h]
        num = (jnp.dot(s.astype(BF16), v, preferred_element_type=F32)
               + w_inter * jnp.dot(qb, c_old.astype(BF16), preferred_element_type=F32))
        den = (jnp.sum(s, axis=1, keepdims=True)
               + w_inter * jnp.sum(q * n_old, axis=1, keepdims=True))
        hh = num / jnp.maximum(jnp.abs(den), jnp.exp(-m_t))

        log_w = g - b_col + i_col
        m_new = jnp.maximum(g + m_prev, jnp.max(log_w, axis=0, keepdims=True))
        w_s = jnp.exp(log_w - m_new)
        decay = jnp.exp(g + m_prev - m_new)
        wv = (w_s * v.astype(F32)).astype(BF16)
        c_sc[h] = decay * c_old + lax.dot_general(kb, wv, (((0,), (0,)), ((), ())),
                                                  preferred_element_type=F32)
        n_sc[h] = decay * n_old + jnp.sum(w_s * k, axis=0, keepdims=True)
        m_sc[h] = jnp.broadcast_to(m_new, (SUBLANES, LANES))

        r = lax.rsqrt(jnp.mean(hh * hh, axis=-1, keepdims=True) + EPS)
        hn = (hh * r) * mg_ref[0:1, h * dh:(h + 1) * dh]
        og = _sigmoid(o_ref[:, h * dh:(h + 1) * dh].astype(F32))
        out_ref[:, h * dh:(h + 1) * dh] = (hn * og).astype(BF16)

    ext_sc[0:SUBLANES, :] = ext_sc[L:L + SUBLANES, :]


def _mlstm(z, gates_col, gates_row, conv_w, m_norm_g, batch, seq, chunk):
    n = batch * seq
    nt = seq // chunk
    zspec = lambda cb: pl.BlockSpec((chunk, D_MODEL), lambda b, t, cb=cb: (b * nt + t, cb))
    return pl.pallas_call(
        functools.partial(_mlstm_kernel, chunk=chunk),
        grid=(batch, nt),
        in_specs=[
            zspec(ZB_MQ), zspec(ZB_MK), zspec(ZB_MV), zspec(ZB_MO),
            pl.BlockSpec((chunk, LANES), lambda b, t: (b * nt + t, 0)),
            pl.BlockSpec((LANES, chunk), lambda b, t: (0, b * nt + t)),
            pl.BlockSpec((CONV_WIDTH, 2 * D_MODEL), lambda b, t: (0, 0)),
            pl.BlockSpec((1, D_MODEL), lambda b, t: (0, 0)),
        ],
        out_specs=pl.BlockSpec((chunk, D_MODEL), lambda b, t: (b * nt + t, 0)),
        out_shape=jax.ShapeDtypeStruct((n, D_MODEL), BF16),
        scratch_shapes=[
            pltpu.VMEM((M_HEADS, M_HEAD_DIM, M_HEAD_DIM), F32),
            pltpu.VMEM((M_HEADS, 1, M_HEAD_DIM), F32),
            pltpu.VMEM((M_HEADS, SUBLANES, LANES), F32),
            pltpu.VMEM((chunk + SUBLANES, 2 * D_MODEL), F32),
        ],
        compiler_params=_cparams(("parallel", "arbitrary")),
        name="mlstm",
    )(z, z, z, z, gates_col, gates_row, conv_w, m_norm_g)


LOG2E = 1.4426950408889634
FOX_SAFE_LOG2_RANGE = 100.0


def _split3(c):
    hi = c.astype(BF16).astype(F32)
    r1 = c - hi
    mid = r1.astype(BF16).astype(F32)
    lo = (r1 - mid).astype(BF16).astype(F32)
    return hi, mid, lo


def _fox_aug(c, lane, is_query, shift):
    hi, mid, lo = _split3(c if is_query else -c)
    off = 0 if is_query else 3
    ex = jnp.where(lane == off, hi, jnp.where(lane == off + 1, mid, jnp.where(lane == off + 2, lo, 0.0)))
    ones_at = (lane >= 3) & (lane < 6) if is_query else lane < 3
    ex = jnp.where(ones_at, 1.0, ex)
    return jnp.where(lane == 6, -shift if is_query else 1.0, ex).astype(BF16)


def _fox_kernel(bnd_ref, q_ref, k_ref, v_ref, ccol_ref, qg_ref, kg_ref, out_ref,
                ka_sc, va_sc, m_sc, acc_sc, s_sc, *, tq, seq, hp):
    g = pl.program_id(1)
    i = pl.program_id(2)
    dh = F_HEAD_DIM
    lane = lax.broadcasted_iota(I32, (tq, LANES), 1)
    shift = bnd_ref[0]
    safe = bnd_ref[1] > 0.5

    def c_of(r0, hh):
        cc = ccol_ref[pl.ds(r0, tq), :]
        return jnp.sum(jnp.where(lane == G_FF + g * hp + hh, cc, 0.0), axis=1, keepdims=True) * LOG2E

    @pl.when(i == 0)
    def _():
        def prep_chunk(c, carry):
            r0 = pl.multiple_of(c * tq, tq)
            for hh in range(hp):
                kk = k_ref[pl.ds(r0, tq), hh * dh:(hh + 1) * dh].astype(F32)
                r = lax.rsqrt(jnp.mean(kk * kk, axis=-1, keepdims=True) + EPS)
                ka_sc[hh, pl.ds(r0, tq), 0:dh] = ((kk * r) * kg_ref[...]).astype(BF16)
                ka_sc[hh, pl.ds(r0, tq), dh:2 * dh] = _fox_aug(c_of(r0, hh), lane, False, shift)
                va_sc[hh, pl.ds(r0, tq), 0:dh] = v_ref[pl.ds(r0, tq), hh * dh:(hh + 1) * dh]
                va_sc[hh, pl.ds(r0, tq), dh:2 * dh] = jnp.ones((tq, dh), BF16)
            return carry
        lax.fori_loop(0, seq // tq, prep_chunk, 0)

    q0 = pl.multiple_of(i * tq, tq)
    qa = []
    for hh in range(hp):
        q = q_ref[:, hh * dh:(hh + 1) * dh].astype(F32)
        r = lax.rsqrt(jnp.mean(q * q, axis=-1, keepdims=True) + EPS)
        qn = (((q * r) * qg_ref[...]) * (dh ** -0.5 * LOG2E)).astype(BF16)
        qa.append(jnp.concatenate([qn, _fox_aug(c_of(q0, hh), lane, True, shift)], axis=1))

    acc_sc[...] = jnp.zeros_like(acc_sc)

    def scores(j, masked):
        r0 = pl.multiple_of(j * tq, tq)
        s = [lax.dot_general(qa[hh], ka_sc[hh, pl.ds(r0, tq), :], (((1,), (1,)), ((), ())),
                             preferred_element_type=F32) for hh in range(hp)]
        if masked:
            rr = lax.broadcasted_iota(I32, (tq, tq), 0)
            cc = lax.broadcasted_iota(I32, (tq, tq), 1)
            s = [jnp.where(rr >= cc, sh, -jnp.inf) for sh in s]
        return r0, s

    def fixed_consume(j, s):
        r0 = pl.multiple_of(j * tq, tq)
        for hh in range(hp):
            p = jnp.exp2(s[hh]).astype(BF16)
            acc_sc[hh] += jnp.dot(p, va_sc[hh, pl.ds(r0, tq), :], preferred_element_type=F32)

    def run_fixed():
        def produce(buf, j, masked):
            _, s = scores(j, masked)
            for hh in range(hp):
                s_sc[buf, hh] = s[hh]

        def consume(buf, j):
            fixed_consume(j, [s_sc[buf, hh] for hh in range(hp)])

        produce(0, i, True)
        pairs = i // 2

        def body(k, carry):
            produce(1, 2 * k, False)
            consume(0, jnp.where(k == 0, i, 2 * k - 1))
            produce(0, 2 * k + 1, False)
            consume(1, 2 * k)
            return carry
        lax.fori_loop(0, pairs, body, 0)
        pending = jnp.where(pairs == 0, i, 2 * pairs - 1)

        @pl.when(i % 2 == 1)
        def _():
            produce(1, i - 1, False)
            consume(0, pending)
            consume(1, i - 1)

        @pl.when(i % 2 == 0)
        def _():
            consume(0, pending)

    def running_step(j, masked):
        r0, s = scores(j, masked)
        for hh in range(hp):
            m_old = m_sc[hh]
            m_new = jnp.maximum(m_old, jnp.max(s[hh], axis=1, keepdims=True))
            p = jnp.exp2(s[hh] - m_new).astype(BF16)
            alpha = jnp.exp2(m_old - m_new)
            acc_sc[hh] = alpha * acc_sc[hh] + jnp.dot(p, va_sc[hh, pl.ds(r0, tq), :],
                                                      preferred_element_type=F32)
            m_sc[hh] = m_new

    def run(step):
        def body(j, carry):
            step(j, False)
            return carry
        lax.fori_loop(0, i, body, 0)
        step(i, True)

    @pl.when(safe)
    def _():
        run_fixed()

    @pl.when(jnp.logical_not(safe))
    def _():
        m_sc[...] = jnp.full_like(m_sc, -jnp.inf)
        run(running_step)

    for hh in range(hp):
        out_ref[:, hh * dh:(hh + 1) * dh] = (acc_sc[hh, :, 0:dh] / acc_sc[hh, :, dh:2 * dh]).astype(BF16)


def _fox(z, ccol, qn_g, kn_g, batch, seq, tq, hp):
    n = batch * seq
    nq = seq // tq
    w = hp * F_HEAD_DIM
    fq0 = ZB_FQ * (D_MODEL // w)
    fk0 = ZB_FK * (D_MODEL // w)
    fv0 = ZB_FV * (D_MODEL // w)
    shift = jnp.ceil(1.01 * LOG2E * F_HEAD_DIM ** 0.5 * jnp.max(jnp.abs(qn_g)) * jnp.max(jnp.abs(kn_g)))
    safe = (2.0 * shift <= FOX_SAFE_LOG2_RANGE).astype(F32)
    bnd = jnp.stack([shift, safe]).astype(F32)
    return pl.pallas_call(
        functools.partial(_fox_kernel, tq=tq, seq=seq, hp=hp),
        grid=(batch, F_HEADS // hp, nq),
        in_specs=[
            pl.BlockSpec(memory_space=pltpu.SMEM),
            pl.BlockSpec((tq, w), lambda b, g, i: (b * nq + i, fq0 + g)),
            pl.BlockSpec((seq, w), lambda b, g, i: (b, fk0 + g)),
            pl.BlockSpec((seq, w), lambda b, g, i: (b, fv0 + g)),
            pl.BlockSpec((seq, LANES), lambda b, g, i: (b, 0)),
            pl.BlockSpec((1, F_HEAD_DIM), lambda b, g, i: (0, 0)),
            pl.BlockSpec((1, F_HEAD_DIM), lambda b, g, i: (0, 0)),
        ],
        out_specs=pl.BlockSpec((tq, w), lambda b, g, i: (b * nq + i, g)),
        out_shape=jax.ShapeDtypeStruct((n, D_MODEL), BF16),
        scratch_shapes=[
            pltpu.VMEM((hp, seq, 2 * F_HEAD_DIM), BF16),
            pltpu.VMEM((hp, seq, 2 * F_HEAD_DIM), BF16),
            pltpu.VMEM((hp, tq, 1), F32),
            pltpu.VMEM((hp, tq, 2 * F_HEAD_DIM), F32),
            pltpu.VMEM((2, hp, tq, tq), F32),
        ],
        compiler_params=_cparams(("parallel", "parallel", "arbitrary")),
        name="fox",
    )(bnd, z, z, z, ccol, qn_g, kn_g)


def _topk_rows(s, k, store):
    rows = lax.broadcasted_iota(I32, s.shape, 0)
    n_rows = s.shape[0]
    for r in range(k):
        mx = jnp.max(s, axis=0, keepdims=True)
        ix = jnp.min(jnp.where(s == mx, rows, n_rows), axis=0, keepdims=True)
        s = jnp.where(rows == ix, -jnp.inf, s)
        store(r, mx, ix)


def _pair_candidates(v0, v1):
    k = P_TOPK
    w = v0.shape[1]
    row8 = lax.broadcasted_iota(I32, (SUBLANES, w), 0)
    sums = [v0[0:1, :] + v1]
    for a in range(1, SUBLANES):
        nb = k // (a + 1)
        sums.append(jnp.where(row8 < nb, v0[a:a + 1, :] + v1[0:SUBLANES, :], -jnp.inf))
    sums.append(v0[SUBLANES:k, :] + v1[0:1, :])
    return jnp.concatenate(sums, axis=0)


def _pair_of_row(row):
    k = P_TOPK
    q = row - k
    mid_a = 1 + (q >> 3)
    mid_b = q & (SUBLANES - 1)
    last = row >= k + SUBLANES * (SUBLANES - 1)
    a = jnp.where(row < k, 0, jnp.where(last, row - (k + SUBLANES * (SUBLANES - 2)), mid_a))
    b = jnp.where(row < k, row, jnp.where(last, 0, mid_b))
    return a, b


def _merge_kernel(hm_ref, hf_ref, gm_ref, gf_ref, x_ref, wm_ref, wf_ref, wo_ref, g2_ref,
                  wpq_ref, sk_ref, x1_ref, xn_ref, ids_i_ref, ids_j_ref, gate_ref,
                  sc_sc, val_sc, idx_sc, top_sc, pos_sc, oi_sc, oj_sc, og_sc, *, tm, tc):
    ym = jnp.dot(hm_ref[...], wm_ref[...], preferred_element_type=F32)
    yf = jnp.dot(hf_ref[...], wf_ref[...], preferred_element_type=F32)
    y = (_sigmoid(gm_ref[...].astype(F32)) * ym + _sigmoid(gf_ref[...].astype(F32)) * yf)
    x1 = x_ref[...] + jnp.dot(y.astype(BF16), wo_ref[...], preferred_element_type=F32)
    x1_ref[...] = x1
    r = lax.rsqrt(jnp.mean(x1 * x1, axis=-1, keepdims=True) + EPS)
    xn = ((x1 * r) * g2_ref[...]).astype(BF16)
    xn_ref[...] = xn
    qh = jnp.dot(xn, wpq_ref[...], preferred_element_type=F32).astype(BF16)

    for hc in range(2 * P_HEADS):
        sc_sc[hc] = lax.dot_general(sk_ref[hc], qh[:, hc * P_HALF:(hc + 1) * P_HALF],
                                    (((1,), (1,)), ((), ())), preferred_element_type=F32)

    def chunk_body(c, carry):
        c0 = pl.multiple_of(c * tc, tc)

        def first_level(hc, carry2):
            s = sc_sc[hc, :, pl.ds(c0, tc)]

            def store(r, val, idx):
                val_sc[hc, r:r + 1, :] = val
                idx_sc[hc, r:r + 1, :] = idx
            _topk_rows(s, P_TOPK, store)
            return carry2
        lax.fori_loop(0, 2 * P_HEADS, first_level, 0)

        def second_level(hd, carry2):
            v0 = val_sc[2 * hd]
            v1 = val_sc[2 * hd + 1]
            i0 = idx_sc[2 * hd]
            i1 = idx_sc[2 * hd + 1]
            cand = _pair_candidates(v0, v1)

            def store(r, val, idx):
                top_sc[r:r + 1, :] = val
                pos_sc[r:r + 1, :] = idx
            _topk_rows(cand, P_TOPK, store)
            top = top_sc[...]
            pa, pb = _pair_of_row(pos_sc[...])
            sel_i = jnp.zeros(pa.shape, I32)
            sel_j = jnp.zeros(pa.shape, I32)
            for a in range(P_TOPK):
                sel_i = jnp.where(pa == a, i0[a:a + 1, :], sel_i)
                sel_j = jnp.where(pb == a, i1[a:a + 1, :], sel_j)
            e = jnp.exp(top - top[0:1, :])
            gate = e / jnp.sum(e, axis=0, keepdims=True)
            r0 = pl.multiple_of(hd * P_TOPK, P_TOPK)
            oi_sc[pl.ds(r0, P_TOPK), :] = sel_i
            oj_sc[pl.ds(r0, P_TOPK), :] = sel_j
            og_sc[pl.ds(r0, P_TOPK), :] = gate
            return carry2
        lax.fori_loop(0, P_HEADS, second_level, 0)

        ids_i_ref[pl.ds(c0, tc), :] = oi_sc[...].T
        ids_j_ref[pl.ds(c0, tc), :] = oj_sc[...].T
        gate_ref[pl.ds(c0, tc), :] = og_sc[...].T
        return carry
    lax.fori_loop(0, tm // tc, chunk_body, 0)


def _merge(hm, hf, z, x2, w_m, w_f, w_o, g2, w_pq, sk, tm, tc):
    n = x2.shape[0]
    hk = P_HEADS * P_TOPK
    row = lambda i: (i, 0)
    const = lambda i: (0, 0)
    return pl.pallas_call(
        functools.partial(_merge_kernel, tm=tm, tc=tc),
        grid=(n // tm,),
        in_specs=[
            pl.BlockSpec((tm, D_MODEL), row),
            pl.BlockSpec((tm, D_MODEL), row),
            pl.BlockSpec((tm, D_MODEL), lambda i: (i, ZB_GM)),
            pl.BlockSpec((tm, D_MODEL), lambda i: (i, ZB_GF)),
            pl.BlockSpec((tm, D_MODEL), row),
            pl.BlockSpec((D_MODEL, D_MODEL), const),
            pl.BlockSpec((D_MODEL, D_MODEL), const),
            pl.BlockSpec((D_MODEL, D_MODEL), const),
            pl.BlockSpec((1, D_MODEL), const),
            pl.BlockSpec((D_MODEL, 2 * P_HEADS * P_HALF), const),
            pl.BlockSpec((2 * P_HEADS, P_KEYS, P_HALF), lambda i: (0, 0, 0)),
        ],
        out_specs=[
            pl.BlockSpec((tm, D_MODEL), row),
            pl.BlockSpec((tm, D_MODEL), row),
            pl.BlockSpec((tm, hk), row),
            pl.BlockSpec((tm, hk), row),
            pl.BlockSpec((tm, hk), row),
        ],
        out_shape=[
            jax.ShapeDtypeStruct((n, D_MODEL), F32),
            jax.ShapeDtypeStruct((n, D_MODEL), BF16),
            jax.ShapeDtypeStruct((n, hk), I32),
            jax.ShapeDtypeStruct((n, hk), I32),
            jax.ShapeDtypeStruct((n, hk), F32),
        ],
        scratch_shapes=[
            pltpu.VMEM((2 * P_HEADS, P_KEYS, tm), F32),
            pltpu.VMEM((2 * P_HEADS, P_TOPK, tc), F32),
            pltpu.VMEM((2 * P_HEADS, P_TOPK, tc), I32),
            pltpu.VMEM((P_TOPK, tc), F32),
            pltpu.VMEM((P_TOPK, tc), I32),
            pltpu.VMEM((hk, tc), I32),
            pltpu.VMEM((hk, tc), I32),
            pltpu.VMEM((hk, tc), F32),
        ],
        compiler_params=_cparams(("parallel",)),
        name="merge_retrieve",
    )(hm, hf, z, z, x2, w_m, w_f, w_o, g2, w_pq, sk)


U_SUB = 512


def _peer_u_kernel(xn_ref, u_ref, ii_ref, jj_ref, gate_ref, hw_ref, ids_ref, acc_sc, *, eb):
    e = pl.program_id(1)
    ne = pl.num_programs(1)

    @pl.when(e == 0)
    def _():
        acc_sc[...] = jnp.zeros_like(acc_sc)

    xn = xn_ref[...]
    ii = ii_ref[...]
    jj = jj_ref[...]
    acc = acc_sc[...]
    for sb in range(eb // U_SUB):
        a = lax.dot_general(xn, u_ref[sb * U_SUB:(sb + 1) * U_SUB, :], (((1,), (1,)), ((), ())),
                            preferred_element_type=F32)
        for c in range(U_SUB // P_KEYS):
            blk = e * (eb // P_KEYS) + sb * (U_SUB // P_KEYS) + c
            g = jnp.take_along_axis(a[:, c * P_KEYS:(c + 1) * P_KEYS], jj, axis=1,
                                    mode="promise_in_bounds")
            acc = jnp.where(ii == blk, g, acc)
    acc_sc[...] = acc

    @pl.when(e == ne - 1)
    def _():
        av = acc_sc[...]
        gelu = 0.5 * av * (1.0 + lax.erf(av * (2.0 ** -0.5)))
        hw_ref[...] = gate_ref[...] * gelu
        ids_ref[...] = (ii * P_KEYS + jj) * (D_MODEL // (2 * LANES))


def _peer_u(xn, u_bf, ids_i, ids_j, gate, tm, eb):
    n = xn.shape[0]
    hk = P_HEADS * P_TOPK
    row = lambda i, e: (i, 0)
    return pl.pallas_call(
        functools.partial(_peer_u_kernel, eb=eb),
        grid=(n // tm, N_EXPERTS // eb),
        in_specs=[
            pl.BlockSpec((tm, D_MODEL), row),
            pl.BlockSpec((eb, D_MODEL), lambda i, e: (e, 0)),
            pl.BlockSpec((tm, hk), row),
            pl.BlockSpec((tm, hk), row),
            pl.BlockSpec((tm, hk), row),
        ],
        out_specs=[pl.BlockSpec((tm, hk), row), pl.BlockSpec((tm, hk), row)],
        out_shape=[jax.ShapeDtypeStruct((n, hk), F32), jax.ShapeDtypeStruct((n, hk), I32)],
        scratch_shapes=[pltpu.VMEM((tm, hk), F32)],
        compiler_params=_cparams(("parallel", "arbitrary")),
        name="peer_u",
    )(xn, u_bf, ids_i, ids_j, gate)


V_ROW_WORDS = D_MODEL // (2 * LANES)


V_SLABS = 128
V_GROUP = 16
V_ID_SPLIT = 4


def _peer_v_kernel(*refs, tt):
    id_refs = refs[:V_ID_SPLIT]
    w_ref, x1_ref, tab_ref, out_ref, slab_sc = refs[V_ID_SPLIT:]
    hk = P_HEADS * P_TOPK
    per = hk // V_ID_SPLIT
    tg = V_GROUP
    plane = tg + SUBLANES
    half = D_MODEL // 2

    def group(g, carry):
        t0 = pl.multiple_of(g * tg, tg)
        wg = w_ref[pl.ds(t0, tg), :]
        acc_lo = [jnp.zeros((tg, LANES), F32) for _ in range(V_ROW_WORDS)]
        acc_hi = [jnp.zeros((tg, LANES), F32) for _ in range(V_ROW_WORDS)]
        for n_slot in range(hk):
            q, k = divmod(n_slot, V_ID_SPLIT)
            m = k * per + q
            slab = slab_sc.at[n_slot % V_SLABS]
            for t in range(tg):
                r0 = pl.multiple_of(id_refs[k][t0 * per + (t * per + q)], V_ROW_WORDS)
                slab[pl.ds(t, V_ROW_WORDS, stride=plane), :] = tab_ref[pl.ds(r0, V_ROW_WORDS), :]
            wcol = wg[:, m:m + 1]
            for s in range(V_ROW_WORDS):
                words = slab[s * plane:s * plane + tg, :]
                acc_lo[s] = acc_lo[s] + wcol * lax.bitcast_convert_type(words << 16, F32)
                acc_hi[s] = acc_hi[s] + wcol * lax.bitcast_convert_type(words & jnp.int32(-65536), F32)
        for s in range(V_ROW_WORDS):
            lo = slice(s * LANES, (s + 1) * LANES)
            hi = slice(half + s * LANES, half + (s + 1) * LANES)
            out_ref[pl.ds(t0, tg), lo] = x1_ref[pl.ds(t0, tg), lo] + acc_lo[s]
            out_ref[pl.ds(t0, tg), hi] = x1_ref[pl.ds(t0, tg), hi] + acc_hi[s]
        return carry
    lax.fori_loop(0, tt // tg, group, 0)


def _peer_v(ids, hw, x1, v_packed, tt):
    n = ids.shape[0]
    hk = P_HEADS * P_TOPK
    per = hk // V_ID_SPLIT
    id_parts = [ids[:, k * per:(k + 1) * per].reshape(n * per) for k in range(V_ID_SPLIT)]
    return pl.pallas_call(
        functools.partial(_peer_v_kernel, tt=tt),
        grid=(n // tt,),
        in_specs=[pl.BlockSpec((tt * per,), lambda i: (i,), memory_space=pltpu.SMEM)] * V_ID_SPLIT + [
            pl.BlockSpec((tt, hk), lambda i: (i, 0)),
            pl.BlockSpec((tt, D_MODEL), lambda i: (i, 0)),
            pl.BlockSpec((N_EXPERTS * V_ROW_WORDS, LANES), lambda i: (0, 0),
                         pipeline_mode=pl.Buffered(1)),
        ],
        out_specs=pl.BlockSpec((tt, D_MODEL), lambda i: (i, 0)),
        out_shape=jax.ShapeDtypeStruct((n, D_MODEL), F32),
        scratch_shapes=[pltpu.VMEM((V_SLABS, V_ROW_WORDS * (V_GROUP + SUBLANES), LANES), I32)],
        compiler_params=_cparams(("arbitrary",)),
        name="peer_v",
    )(*id_parts, hw, x1, v_packed)


def _pack_v(v_tab):
    vb = lax.bitcast_convert_type(v_tab.astype(BF16), jnp.uint16).astype(jnp.uint32)
    half = D_MODEL // 2
    packed = vb[:, :half] | (vb[:, half:] << 16)
    return lax.bitcast_convert_type(packed, I32).reshape(N_EXPERTS * V_ROW_WORDS, LANES)


def _layer(x2, batch, seq, norm1_g, w_in, b_in, conv_w, m_norm_g, qn_g, kn_g, w_m_out, w_f_out,
           w_out, norm2_g, w_pq, sub_keys, u_tab, v_tab):
    o = _OFFS
    cols = lambda a: jnp.concatenate([a[..., o[0]:o[4]], a[..., o[6]:o[9]], a[..., o[10]:o[12]]], axis=-1)
    small = lambda a: jnp.concatenate([a[..., o[4]:o[6]], a[..., o[9]:o[10]]], axis=-1)
    w_main = cols(w_in).astype(BF16)
    b_main = cols(b_in)[None, :]
    ws = small(w_in)
    bs = small(b_in)
    pad = LANES - ws.shape[1]
    ws = jnp.pad(ws, ((0, 0), (0, pad))).astype(BF16)
    bs = jnp.pad(bs, (0, pad))

    tm_a = min(1024, x2.shape[0])
    z, gates_col, gates_row = _inproj(x2, norm1_g[None, :], w_main, b_main, ws, ws.T,
                                      bs[None, :], bs[:, None], tm_a)
    ccol = _forget_cumsum(gates_row, batch, seq)
    chunk = min(256, seq)
    hm = _mlstm(z, gates_col, gates_row, conv_w, m_norm_g[None, :], batch, seq, chunk)
    tq = min(512, seq)
    hf = _fox(z, ccol, qn_g[None, :], kn_g[None, :], batch, seq, tq, 2)

    sk = sub_keys.reshape(2 * P_HEADS, P_KEYS, P_HALF).astype(BF16)
    tm_d = min(512, x2.shape[0])
    x1, xn, ids_i, ids_j, gate = _merge(hm, hf, z, x2, w_m_out.astype(BF16), w_f_out.astype(BF16),
                                        w_out.astype(BF16), norm2_g[None, :], w_pq.astype(BF16), sk,
                                        tm_d, 512)
    tm_e = min(1024, x2.shape[0])
    hw, ids = _peer_u(xn, u_tab.astype(BF16), ids_i, ids_j, gate, tm_e, 2048)
    return _peer_v(ids, hw, x1, _pack_v(v_tab), min(128, x1.shape[0]))


def kernel(x, norm1_g, w_in, b_in, conv_w, m_norm_g, qn_g, kn_g, w_m_out, w_f_out, w_out,
           norm2_g, w_pq, sub_keys, u_tab, v_tab):
    batch, seq, d = x.shape
    x2 = x.reshape(batch * seq, d)
    for l in range(w_in.shape[0]):
        x2 = _layer(x2, batch, seq, norm1_g[l], w_in[l], b_in[l], conv_w[l], m_norm_g[l], qn_g[l],
                    kn_g[l], w_m_out[l], w_f_out[l], w_out[l], norm2_g[l], w_pq[l], sub_keys[l],
                    u_tab[l], v_tab[l])
    return x2.reshape(batch, seq, d)
```

```python
import functools

import jax
import jax.numpy as jnp
from jax import lax
from jax.experimental import pallas as pl
from jax.experimental.pallas import tpu as pltpu

F32 = jnp.float32
BF16 = jnp.bfloat16
I32 = jnp.int32

D_MODEL = 1024
M_HEADS = 4
M_HEAD_DIM = 256
F_HEADS = 8
F_HEAD_DIM = 128
CONV_WIDTH = 4
P_HEADS = 8
P_KEYS = 128
P_TOPK = 16
P_HALF = 128
N_EXPERTS = P_KEYS * P_KEYS
EPS = 1e-6
M_INIT = -1e30
LANES = 128
SUBLANES = 8
VMEM_LIMIT = 56 * 1024 * 1024

_SIZES = (1024, 1024, 1024, 1024, M_HEADS, M_HEADS, 1024, 1024, 1024, F_HEADS, 1024, 1024)
_OFFS = [0]
for _s in _SIZES:
    _OFFS.append(_OFFS[-1] + _s)
Z_COLS = 9 * 1024
ZB_MQ, ZB_MK, ZB_MV, ZB_MO, ZB_FQ, ZB_FK, ZB_FV, ZB_GM, ZB_GF = range(9)
G_MI, G_MF, G_FF = 0, M_HEADS, 2 * M_HEADS


def _cparams(sem):
    return pltpu.CompilerParams(dimension_semantics=sem, vmem_limit_bytes=VMEM_LIMIT)


def _log_sigmoid(x):
    return jnp.minimum(x, 0.0) - jnp.log1p(jnp.exp(-jnp.abs(x)))


def _sigmoid(x):
    return 0.5 * jnp.tanh(0.5 * x) + 0.5


INPROJ_TN = 3072


def _inproj_kernel(x_ref, g_ref, w_ref, b_ref, ws_ref, wst_ref, bs_ref, bst_ref,
                   z_ref, gc_ref, gr_ref):
    x = x_ref[...]
    r = lax.rsqrt(jnp.mean(x * x, axis=-1, keepdims=True) + EPS)
    h = ((x * r) * g_ref[...]).astype(BF16)
    gc_ref[...] = jnp.dot(h, ws_ref[...], preferred_element_type=F32) + bs_ref[...]
    gr_ref[...] = lax.dot_general(wst_ref[...], h, (((1,), (1,)), ((), ())),
                                  preferred_element_type=F32) + bst_ref[...]
    for c in range(Z_COLS // INPROJ_TN):
        cols = slice(c * INPROJ_TN, (c + 1) * INPROJ_TN)
        z = jnp.dot(h, w_ref[:, cols], preferred_element_type=F32) + b_ref[:, cols]
        z_ref[:, cols] = z.astype(BF16)


def _inproj(x2, g1, w_main, b_main, w_small, w_small_t, b_small, b_small_t, tm):
    n = x2.shape[0]
    const = lambda i: (0, 0)
    return pl.pallas_call(
        _inproj_kernel,
        grid=(n // tm,),
        in_specs=[
            pl.BlockSpec((tm, D_MODEL), lambda i: (i, 0)),
            pl.BlockSpec((1, D_MODEL), const),
            pl.BlockSpec((D_MODEL, Z_COLS), const, pipeline_mode=pl.Buffered(1)),
            pl.BlockSpec((1, Z_COLS), const),
            pl.BlockSpec((D_MODEL, LANES), const),
            pl.BlockSpec((LANES, D_MODEL), const),
            pl.BlockSpec((1, LANES), const),
            pl.BlockSpec((LANES, 1), const),
        ],
        out_specs=[
            pl.BlockSpec((tm, Z_COLS), lambda i: (i, 0)),
            pl.BlockSpec((tm, LANES), lambda i: (i, 0)),
            pl.BlockSpec((LANES, tm), lambda i: (0, i)),
        ],
        out_shape=[
            jax.ShapeDtypeStruct((n, Z_COLS), BF16),
            jax.ShapeDtypeStruct((n, LANES), F32),
            jax.ShapeDtypeStruct((LANES, n), F32),
        ],
        compiler_params=_cparams(("parallel",)),
        name="inproj",
    )(x2, g1, w_main, b_main, w_small, w_small_t, b_small, b_small_t)


def _cumsum_kernel(gr_ref, ccol_ref):
    x = _log_sigmoid(gr_ref[...])
    t = x.shape[1]
    lane = lax.broadcasted_iota(I32, x.shape, 1)
    s = 1
    while s < t:
        x = x + jnp.where(lane >= s, pltpu.roll(x, s, axis=1), 0.0)
        s *= 2
    ccol_ref[...] = x.T


def _forget_cumsum(gates_row, batch, seq):
    n = batch * seq
    return pl.pallas_call(
        _cumsum_kernel,
        grid=(batch,),
        in_specs=[pl.BlockSpec((LANES, seq), lambda b: (0, b))],
        out_specs=pl.BlockSpec((seq, LANES), lambda b: (b, 0)),
        out_shape=jax.ShapeDtypeStruct((n, LANES), F32),
        compiler_params=_cparams(("parallel",)),
        name="forget_cumsum",
    )(gates_row)


def _mlstm_kernel(q_ref, k_ref, v_ref, o_ref, gc_ref, gr_ref, cw_ref, mg_ref, out_ref,
                  c_sc, n_sc, m_sc, tail_sc, qk_sc, *, chunk):
    L = chunk
    dh = M_HEAD_DIM
    t = pl.program_id(1)

    @pl.when(t == 0)
    def _():
        c_sc[...] = jnp.zeros_like(c_sc)
        n_sc[...] = jnp.zeros_like(n_sc)
        m_sc[...] = jnp.full_like(m_sc, M_INIT)
        tail_sc[...] = jnp.zeros_like(tail_sc)

    gc = gc_ref[...]
    gr = gr_ref[...]
    lsg_c = _log_sigmoid(gc)
    lsg_r = _log_sigmoid(gr[0:2 * M_HEADS, :])
    row_i = lax.broadcasted_iota(I32, (L, L), 0)
    col_i = lax.broadcasted_iota(I32, (L, L), 1)
    tril = col_i <= row_i
    shift = [(row_i - col_i == d).astype(BF16) for d in range(1, CONV_WIDTH)]
    row8 = lax.broadcasted_iota(I32, (SUBLANES, dh), 0)

    def conv_silu(x_ref, cx, c0):
        x = x_ref[:, cx:cx + dh]
        tail = tail_sc[:, c0:c0 + dh]
        acc = x.astype(F32) * cw_ref[CONV_WIDTH - 1:CONV_WIDTH, c0:c0 + dh]
        for d in range(1, CONV_WIDTH):
            xs = jnp.dot(shift[d - 1], x, preferred_element_type=F32)
            head = jnp.where(row8 < d, pltpu.roll(tail, d, axis=0), xs[0:SUBLANES, :])
            xs = jnp.concatenate([head, xs[SUBLANES:, :]], axis=0)
            acc = acc + xs * cw_ref[CONV_WIDTH - 1 - d:CONV_WIDTH - d, c0:c0 + dh]
        tail_sc[:, c0:c0 + dh] = x[L - SUBLANES:L, :].astype(F32)
        return acc * _sigmoid(acc)

    for h in range(M_HEADS):
        qk_sc[:, h * dh:(h + 1) * dh] = conv_silu(q_ref, h * dh, h * dh) * (dh ** -0.5)
        qk_sc[:, D_MODEL + h * dh:D_MODEL + (h + 1) * dh] = conv_silu(k_ref, h * dh, D_MODEL + h * dh)

    for h in range(M_HEADS):
        q = qk_sc[:, h * dh:(h + 1) * dh]
        k = qk_sc[:, D_MODEL + h * dh:D_MODEL + (h + 1) * dh]
        v = v_ref[:, h * dh:(h + 1) * dh]
        qb = q.astype(BF16)
        kb = k.astype(BF16)

        i_col = gc[:, G_MI + h:G_MI + h + 1]
        lf_col = lsg_c[:, G_MF + h:G_MF + h + 1]
        i_row = gr[G_MI + h:G_MI + h + 1, :]
        lf_row = lsg_r[G_MF + h:G_MF + h + 1, :]

        b_col = jnp.sum(jnp.where(tril, lf_row, 0.0), axis=1, keepdims=True)
        b_row = jnp.sum(jnp.where(row_i <= col_i, lf_col, 0.0), axis=0, keepdims=True)
        g = jnp.sum(lf_row, axis=1, keepdims=True)
        m_prev = m_sc[h][0:1, 0:1]

        dmat = jnp.where(tril, b_col - b_row + i_row, -jnp.inf)
        inter = b_col + m_prev
        m_t = jnp.maximum(jnp.max(dmat, axis=1, keepdims=True), inter)
        p = jnp.exp(dmat - m_t)
        s = lax.dot_general(qb, kb, (((1,), (1,)), ((), ())), preferred_element_type=F32) * p
        w_inter = jnp.exp(inter - m_t)
        c_old = c_sc[h]
        n_old = n_sc[h]
        num = (jnp.dot(s.astype(BF16), v, preferred_element_type=F32)
               + w_inter * jnp.dot(qb, c_old.astype(BF16), preferred_element_type=F32))
        den = (jnp.sum(s, axis=1, keepdims=True)
               + w_inter * jnp.sum(q * n_old, axis=1, keepdims=True))
        hh = num / jnp.maximum(jnp.abs(den), jnp.exp(-m_t))

        log_w = g - b_col + i_col
        m_new = jnp.maximum(g + m_prev, jnp.max(log_w, axis=0, keepdims=True))
        w_s = jnp.exp(log_w - m_new)
        decay = jnp.exp(g + m_prev - m_new)
        wv = (w_s * v.astype(F32)).astype(BF16)
        c_sc[h] = decay * c_old + lax.dot_general(kb, wv, (((0,), (0,)), ((), ())),
                                                  preferred_element_type=F32)
        n_sc[h] = decay * n_old + jnp.sum(w_s * k, axis=0, keepdims=True)
        m_sc[h] = jnp.broadcast_to(m_new, (SUBLANES, LANES))

        r = lax.rsqrt(jnp.mean(hh * hh, axis=-1, keepdims=True) + EPS)
        hn = (hh * r) * mg_ref[0:1, h * dh:(h + 1) * dh]
        og = _sigmoid(o_ref[:, h * dh:(h + 1) * dh].astype(F32))
        out_ref[:, h * dh:(h + 1) * dh] = (hn * og).astype(BF16)


def _mlstm(z, gates_col, gates_row, conv_w, m_norm_g, batch, seq, chunk):
    n = batch * seq
    nt = seq // chunk
    zspec = lambda cb: pl.BlockSpec((chunk, D_MODEL), lambda b, t, cb=cb: (b * nt + t, cb))
    return pl.pallas_call(
        functools.partial(_mlstm_kernel, chunk=chunk),
        grid=(batch, nt),
        in_specs=[
            zspec(ZB_MQ), zspec(ZB_MK), zspec(ZB_MV), zspec(ZB_MO),
            pl.BlockSpec((chunk, LANES), lambda b, t: (b * nt + t, 0)),
            pl.BlockSpec((LANES, chunk), lambda b, t: (0, b * nt + t)),
            pl.BlockSpec((CONV_WIDTH, 2 * D_MODEL), lambda b, t: (0, 0)),
            pl.BlockSpec((1, D_MODEL), lambda b, t: (0, 0)),
        ],
        out_specs=pl.BlockSpec((chunk, D_MODEL), lambda b, t: (b * nt + t, 0)),
        out_shape=jax.ShapeDtypeStruct((n, D_MODEL), BF16),
        scratch_shapes=[
            pltpu.VMEM((M_HEADS, M_HEAD_DIM, M_HEAD_DIM), F32),
            pltpu.VMEM((M_HEADS, 1, M_HEAD_DIM), F32),
            pltpu.VMEM((M_HEADS, SUBLANES, LANES), F32),
            pltpu.VMEM((SUBLANES, 2 * D_MODEL), F32),
            pltpu.VMEM((chunk, 2 * D_MODEL), F32),
        ],
        compiler_params=_cparams(("parallel", "arbitrary")),
        name="mlstm",
    )(z, z, z, z, gates_col, gates_row, conv_w, m_norm_g)


LOG2E = 1.4426950408889634
FOX_SAFE_LOG2_RANGE = 100.0


def _split3(c):
    hi = c.astype(BF16).astype(F32)
    r1 = c - hi
    mid = r1.astype(BF16).astype(F32)
    lo = (r1 - mid).astype(BF16).astype(F32)
    return hi, mid, lo


def _fox_aug(c, lane, is_query, shift):
    hi, mid, lo = _split3(c if is_query else -c)
    off = 0 if is_query else 3
    ex = jnp.where(lane == off, hi, jnp.where(lane == off + 1, mid, jnp.where(lane == off + 2, lo, 0.0)))
    ones_at = (lane >= 3) & (lane < 6) if is_query else lane < 3
    ex = jnp.where(ones_at, 1.0, ex)
    return jnp.where(lane == 6, -shift if is_query else 1.0, ex).astype(BF16)


def _fox_kernel(bnd_ref, q_ref, k_ref, v_ref, ccol_ref, qg_ref, kg_ref, out_ref,
                ka_sc, va_sc, m_sc, acc_sc, s_sc, *, tq, seq, hp):
    g = pl.program_id(1)
    i = pl.program_id(2)
    dh = F_HEAD_DIM
    lane = lax.broadcasted_iota(I32, (tq, LANES), 1)
    shift = bnd_ref[0]
    safe = bnd_ref[1] > 0.5

    def c_of(r0, hh):
        cc = ccol_ref[pl.ds(r0, tq), :]
        return jnp.sum(jnp.where(lane == G_FF + g * hp + hh, cc, 0.0), axis=1, keepdims=True) * LOG2E

    @pl.when(i == 0)
    def _():
        def prep_chunk(c, carry):
            r0 = pl.multiple_of(c * tq, tq)
            for hh in range(hp):
                kk = k_ref[pl.ds(r0, tq), hh * dh:(hh + 1) * dh].astype(F32)
                r = lax.rsqrt(jnp.mean(kk * kk, axis=-1, keepdims=True) + EPS)
                ka_sc[hh, pl.ds(r0, tq), 0:dh] = ((kk * r) * kg_ref[...]).astype(BF16)
                ka_sc[hh, pl.ds(r0, tq), dh:2 * dh] = _fox_aug(c_of(r0, hh), lane, False, shift)
                va_sc[hh, pl.ds(r0, tq), 0:dh] = v_ref[pl.ds(r0, tq), hh * dh:(hh + 1) * dh]
                va_sc[hh, pl.ds(r0, tq), dh:2 * dh] = jnp.ones((tq, dh), BF16)
            return carry
        lax.fori_loop(0, seq // tq, prep_chunk, 0)

    q0 = pl.multiple_of(i * tq, tq)
    qa = []
    for hh in range(hp):
        q = q_ref[:, hh * dh:(hh + 1) * dh].astype(F32)
        r = lax.rsqrt(jnp.mean(q * q, axis=-1, keepdims=True) + EPS)
        qn = (((q * r) * qg_ref[...]) * (dh ** -0.5 * LOG2E)).astype(BF16)
        qa.append(jnp.concatenate([qn, _fox_aug(c_of(q0, hh), lane, True, shift)], axis=1))

    acc_sc[...] = jnp.zeros_like(acc_sc)

    def scores(j, masked):
        r0 = pl.multiple_of(j * tq, tq)
        s = [lax.dot_general(qa[hh], ka_sc[hh, pl.ds(r0, tq), :], (((1,), (1,)), ((), ())),
                             preferred_element_type=F32) for hh in range(hp)]
        if masked:
            rr = lax.broadcasted_iota(I32, (tq, tq), 0)
            cc = lax.broadcasted_iota(I32, (tq, tq), 1)
            s = [jnp.where(rr >= cc, sh, -jnp.inf) for sh in s]
        return r0, s

    def fixed_consume(j, s):
        r0 = pl.multiple_of(j * tq, tq)
        for hh in range(hp):
            p = jnp.exp2(s[hh]).astype(BF16)
            acc_sc[hh] += jnp.dot(p, va_sc[hh, pl.ds(r0, tq), :], preferred_element_type=F32)

    def run_fixed():
        def produce(buf, j, masked):
            _, s = scores(j, masked)
            for hh in range(hp):
                s_sc[buf, hh] = s[hh]

        def consume(buf, j):
            fixed_consume(j, [s_sc[buf, hh] for hh in range(hp)])

        produce(0, i, True)
        pairs = i // 2

        def body(k, carry):
            produce(1, 2 * k, False)
            consume(0, jnp.where(k == 0, i, 2 * k - 1))
            produce(0, 2 * k + 1, False)
            consume(1, 2 * k)
            return carry
        lax.fori_loop(0, pairs, body, 0)
        pending = jnp.where(pairs == 0, i, 2 * pairs - 1)

        @pl.when(i % 2 == 1)
        def _():
            produce(1, i - 1, False)
            consume(0, pending)
            consume(1, i - 1)

        @pl.when(i % 2 == 0)
        def _():
            consume(0, pending)

    def running_step(j, masked):
        r0, s = scores(j, masked)
        for hh in range(hp):
            m_old = m_sc[hh]
            m_new = jnp.maximum(m_old, jnp.max(s[hh], axis=1, keepdims=True))
            p = jnp.exp2(s[hh] - m_new).astype(BF16)
            alpha = jnp.exp2(m_old - m_new)
            acc_sc[hh] = alpha * acc_sc[hh] + jnp.dot(p, va_sc[hh, pl.ds(r0, tq), :],
                                                      preferred_element_type=F32)
            m_sc[hh] = m_new

    def run(step):
        def body(j, carry):
            step(j, False)
            return carry
        lax.fori_loop(0, i, body, 0)
        step(i, True)

    @pl.when(safe)
    def _():
        run_fixed()

    @pl.when(jnp.logical_not(safe))
    def _():
        m_sc[...] = jnp.full_like(m_sc, -jnp.inf)
        run(running_step)

    for hh in range(hp):
        out_ref[:, hh * dh:(hh + 1) * dh] = (acc_sc[hh, :, 0:dh] / acc_sc[hh, :, dh:2 * dh]).astype(BF16)


def _fox(z, ccol, qn_g, kn_g, batch, seq, tq, hp):
    n = batch * seq
    nq = seq // tq
    w = hp * F_HEAD_DIM
    fq0 = ZB_FQ * (D_MODEL // w)
    fk0 = ZB_FK * (D_MODEL // w)
    fv0 = ZB_FV * (D_MODEL // w)
    shift = jnp.ceil(1.01 * LOG2E * F_HEAD_DIM ** 0.5 * jnp.max(jnp.abs(qn_g)) * jnp.max(jnp.abs(kn_g)))
    safe = (2.0 * shift <= FOX_SAFE_LOG2_RANGE).astype(F32)
    bnd = jnp.stack([shift, safe]).astype(F32)
    return pl.pallas_call(
        functools.partial(_fox_kernel, tq=tq, seq=seq, hp=hp),
        grid=(batch, F_HEADS // hp, nq),
        in_specs=[
            pl.BlockSpec(memory_space=pltpu.SMEM),
            pl.BlockSpec((tq, w), lambda b, g, i: (b * nq + i, fq0 + g)),
            pl.BlockSpec((seq, w), lambda b, g, i: (b, fk0 + g)),
            pl.BlockSpec((seq, w), lambda b, g, i: (b, fv0 + g)),
            pl.BlockSpec((seq, LANES), lambda b, g, i: (b, 0)),
            pl.BlockSpec((1, F_HEAD_DIM), lambda b, g, i: (0, 0)),
            pl.BlockSpec((1, F_HEAD_DIM), lambda b, g, i: (0, 0)),
        ],
        out_specs=pl.BlockSpec((tq, w), lambda b, g, i: (b * nq + i, g)),
        out_shape=jax.ShapeDtypeStruct((n, D_MODEL), BF16),
        scratch_shapes=[
            pltpu.VMEM((hp, seq, 2 * F_HEAD_DIM), BF16),
            pltpu.VMEM((hp, seq, 2 * F_HEAD_DIM), BF16),
            pltpu.VMEM((hp, tq, 1), F32),
            pltpu.VMEM((hp, tq, 2 * F_HEAD_DIM), F32),
            pltpu.VMEM((2, hp, tq, tq), F32),
        ],
        compiler_params=_cparams(("parallel", "parallel", "arbitrary")),
        name="fox",
    )(bnd, z, z, z, ccol, qn_g, kn_g)


def _topk_rows(s, k, store):
    rows = lax.broadcasted_iota(I32, s.shape, 0)
    n_rows = s.shape[0]
    for r in range(k):
        mx = jnp.max(s, axis=0, keepdims=True)
        ix = jnp.min(jnp.where(s == mx, rows, n_rows), axis=0, keepdims=True)
        s = jnp.where(rows == ix, -jnp.inf, s)
        store(r, mx, ix)


def _pair_candidates(v0, v1):
    k = P_TOPK
    w = v0.shape[1]
    row8 = lax.broadcasted_iota(I32, (SUBLANES, w), 0)
    sums = [v0[0:1, :] + v1]
    for a in range(1, SUBLANES):
        nb = k // (a + 1)
        sums.append(jnp.where(row8 < nb, v0[a:a + 1, :] + v1[0:SUBLANES, :], -jnp.inf))
    sums.append(v0[SUBLANES:k, :] + v1[0:1, :])
    return jnp.concatenate(sums, axis=0)


def _pair_of_row(row):
    k = P_TOPK
    q = row - k
    mid_a = 1 + (q >> 3)
    mid_b = q & (SUBLANES - 1)
    last = row >= k + SUBLANES * (SUBLANES - 1)
    a = jnp.where(row < k, 0, jnp.where(last, row - (k + SUBLANES * (SUBLANES - 2)), mid_a))
    b = jnp.where(row < k, row, jnp.where(last, 0, mid_b))
    return a, b


def _merge_kernel(hm_ref, hf_ref, gm_ref, gf_ref, x_ref, wm_ref, wf_ref, wo_ref, g2_ref,
                  wpq_ref, sk_ref, x1_ref, xn_ref, ids_i_ref, ids_j_ref, gate_ref,
                  sc_sc, val_sc, idx_sc, top_sc, pos_sc, oi_sc, oj_sc, og_sc, *, tm, tc):
    ym = jnp.dot(hm_ref[...], wm_ref[...], preferred_element_type=F32)
    yf = jnp.dot(hf_ref[...], wf_ref[...], preferred_element_type=F32)
    y = (_sigmoid(gm_ref[...].astype(F32)) * ym + _sigmoid(gf_ref[...].astype(F32)) * yf)
    x1 = x_ref[...] + jnp.dot(y.astype(BF16), wo_ref[...], preferred_element_type=F32)
    x1_ref[...] = x1
    r = lax.rsqrt(jnp.mean(x1 * x1, axis=-1, keepdims=True) + EPS)
    xn = ((x1 * r) * g2_ref[...]).astype(BF16)
    xn_ref[...] = xn
    qh = jnp.dot(xn, wpq_ref[...], preferred_element_type=F32).astype(BF16)

    for hc in range(2 * P_HEADS):
        sc_sc[hc] = lax.dot_general(sk_ref[hc], qh[:, hc * P_HALF:(hc + 1) * P_HALF],
                                    (((1,), (1,)), ((), ())), preferred_element_type=F32)

    def chunk_body(c, carry):
        c0 = pl.multiple_of(c * tc, tc)

        def first_level(hc, carry2):
            s = sc_sc[hc, :, pl.ds(c0, tc)]

            def store(r, val, idx):
                val_sc[hc, r:r + 1, :] = val
                idx_sc[hc, r:r + 1, :] = idx
            _topk_rows(s, P_TOPK, store)
            return carry2
        lax.fori_loop(0, 2 * P_HEADS, first_level, 0)

        def second_level(hd, carry2):
            v0 = val_sc[2 * hd]
            v1 = val_sc[2 * hd + 1]
            i0 = idx_sc[2 * hd]
            i1 = idx_sc[2 * hd + 1]
            cand = _pair_candidates(v0, v1)

            def store(r, val, idx):
                top_sc[r:r + 1, :] = val
                pos_sc[r:r + 1, :] = idx
            _topk_rows(cand, P_TOPK, store)
            top = top_sc[...]
            pa, pb = _pair_of_row(pos_sc[...])
            sel_i = jnp.zeros(pa.shape, I32)
            sel_j = jnp.zeros(pa.shape, I32)
            for a in range(P_TOPK):
                sel_i = jnp.where(pa == a, i0[a:a + 1, :], sel_i)
                sel_j = jnp.where(pb == a, i1[a:a + 1, :], sel_j)
            e = jnp.exp(top - top[0:1, :])
            gate = e / jnp.sum(e, axis=0, keepdims=True)
            r0 = pl.multiple_of(hd * P_TOPK, P_TOPK)
            oi_sc[pl.ds(r0, P_TOPK), :] = sel_i
            oj_sc[pl.ds(r0, P_TOPK), :] = sel_j
            og_sc[pl.ds(r0, P_TOPK), :] = gate
            return carry2
        lax.fori_loop(0, P_HEADS, second_level, 0)

        ids_i_ref[pl.ds(c0, tc), :] = oi_sc[...].T
        ids_j_ref[pl.ds(c0, tc), :] = oj_sc[...].T
        gate_ref[pl.ds(c0, tc), :] = og_sc[...].T
        return carry
    lax.fori_loop(0, tm // tc, chunk_body, 0)


def _merge(hm, hf, z, x2, w_m, w_f, w_o, g2, w_pq, sk, tm, tc):
    n = x2.shape[0]
    hk = P_HEADS * P_TOPK
    row = lambda i: (i, 0)
    const = lambda i: (0, 0)
    return pl.pallas_call(
        functools.partial(_merge_kernel, tm=tm, tc=tc),
        grid=(n // tm,),
        in_specs=[
            pl.BlockSpec((tm, D_MODEL), row),
            pl.BlockSpec((tm, D_MODEL), row),
            pl.BlockSpec((tm, D_MODEL), lambda i: (i, ZB_GM)),
            pl.BlockSpec((tm, D_MODEL), lambda i: (i, ZB_GF)),
            pl.BlockSpec((tm, D_MODEL), row),
            pl.BlockSpec((D_MODEL, D_MODEL), const),
            pl.BlockSpec((D_MODEL, D_MODEL), const),
            pl.BlockSpec((D_MODEL, D_MODEL), const),
            pl.BlockSpec((1, D_MODEL), const),
            pl.BlockSpec((D_MODEL, 2 * P_HEADS * P_HALF), const),
            pl.BlockSpec((2 * P_HEADS, P_KEYS, P_HALF), lambda i: (0, 0, 0)),
        ],
        out_specs=[
            pl.BlockSpec((tm, D_MODEL), row),
            pl.BlockSpec((tm, D_MODEL), row),
            pl.BlockSpec((tm, hk), row),
            pl.BlockSpec((tm, hk), row),
            pl.BlockSpec((tm, hk), row),
        ],
        out_shape=[
            jax.ShapeDtypeStruct((n, D_MODEL), F32),
            jax.ShapeDtypeStruct((n, D_MODEL), BF16),
            jax.ShapeDtypeStruct((n, hk), I32),
            jax.ShapeDtypeStruct((n, hk), I32),
            jax.ShapeDtypeStruct((n, hk), F32),
        ],
        scratch_shapes=[
            pltpu.VMEM((2 * P_HEADS, P_KEYS, tm), F32),
            pltpu.VMEM((2 * P_HEADS, P_TOPK, tc), F32),
            pltpu.VMEM((2 * P_HEADS, P_TOPK, tc), I32),
            pltpu.VMEM((P_TOPK, tc), F32),
            pltpu.VMEM((P_TOPK, tc), I32),
            pltpu.VMEM((hk, tc), I32),
            pltpu.VMEM((hk, tc), I32),
            pltpu.VMEM((hk, tc), F32),
        ],
        compiler_params=_cparams(("parallel",)),
        name="merge_retrieve",
    )(hm, hf, z, z, x2, w_m, w_f, w_o, g2, w_pq, sk)


U_SUB = 512


def _peer_u_kernel(xn_ref, u_ref, ii_ref, jj_ref, gate_ref, hw_ref, ids_ref, acc_sc, *, eb):
    e = pl.program_id(1)
    ne = pl.num_programs(1)

    @pl.when(e == 0)
    def _():
        acc_sc[...] = jnp.zeros_like(acc_sc)

    xn = xn_ref[...]
    ii = ii_ref[...]
    jj = jj_ref[...]
    acc = acc_sc[...]
    for sb in range(eb // U_SUB):
        a = lax.dot_general(xn, u_ref[sb * U_SUB:(sb + 1) * U_SUB, :], (((1,), (1,)), ((), ())),
                            preferred_element_type=F32)
        for c in range(U_SUB // P_KEYS):
            blk = e * (eb // P_KEYS) + sb * (U_SUB // P_KEYS) + c
            g = jnp.take_along_axis(a[:, c * P_KEYS:(c + 1) * P_KEYS], jj, axis=1,
                                    mode="promise_in_bounds")
            acc = jnp.where(ii == blk, g, acc)
    acc_sc[...] = acc

    @pl.when(e == ne - 1)
    def _():
        av = acc_sc[...]
        gelu = 0.5 * av * (1.0 + lax.erf(av * (2.0 ** -0.5)))
        hw_ref[...] = gate_ref[...] * gelu
        ids_ref[...] = (ii * P_KEYS + jj) * (D_MODEL // (2 * LANES))


def _peer_u(xn, u_bf, ids_i, ids_j, gate, tm, eb):
    n = xn.shape[0]
    hk = P_HEADS * P_TOPK
    row = lambda i, e: (i, 0)
    return pl.pallas_call(
        functools.partial(_peer_u_kernel, eb=eb),
        grid=(n // tm, N_EXPERTS // eb),
        in_specs=[
            pl.BlockSpec((tm, D_MODEL), row),
            pl.BlockSpec((eb, D_MODEL), lambda i, e: (e, 0)),
            pl.BlockSpec((tm, hk), row),
            pl.BlockSpec((tm, hk), row),
            pl.BlockSpec((tm, hk), row),
        ],
        out_specs=[pl.BlockSpec((tm, hk), row), pl.BlockSpec((tm, hk), row)],
        out_shape=[jax.ShapeDtypeStruct((n, hk), F32), jax.ShapeDtypeStruct((n, hk), I32)],
        scratch_shapes=[pltpu.VMEM((tm, hk), F32)],
        compiler_params=_cparams(("parallel", "arbitrary")),
        name="peer_u",
    )(xn, u_bf, ids_i, ids_j, gate)


V_ROW_WORDS = D_MODEL // (2 * LANES)


V_SLABS = 128
V_GROUP = 16
V_ID_SPLIT = 4


def _peer_v_kernel(*refs, tt):
    id_refs = refs[:V_ID_SPLIT]
    w_ref, x1_ref, tab_ref, out_ref, slab_sc = refs[V_ID_SPLIT:]
    hk = P_HEADS * P_TOPK
    per = hk // V_ID_SPLIT
    tg = V_GROUP
    plane = tg + SUBLANES
    half = D_MODEL // 2

    def group(g, carry):
        t0 = pl.multiple_of(g * tg, tg)
        wg = w_ref[pl.ds(t0, tg), :]
        acc_lo = [jnp.zeros((tg, LANES), F32) for _ in range(V_ROW_WORDS)]
        acc_hi = [jnp.zeros((tg, LANES), F32) for _ in range(V_ROW_WORDS)]
        for n_slot in range(hk):
            q, k = divmod(n_slot, V_ID_SPLIT)
            m = k * per + q
            slab = slab_sc.at[n_slot % V_SLABS]
            for t in range(tg):
                r0 = pl.multiple_of(id_refs[k][t0 * per + (t * per + q)], V_ROW_WORDS)
                slab[pl.ds(t, V_ROW_WORDS, stride=plane), :] = tab_ref[pl.ds(r0, V_ROW_WORDS), :]
            wcol = wg[:, m:m + 1]
            for s in range(V_ROW_WORDS):
                words = slab[s * plane:s * plane + tg, :]
                acc_lo[s] = acc_lo[s] + wcol * lax.bitcast_convert_type(words << 16, F32)
                acc_hi[s] = acc_hi[s] + wcol * lax.bitcast_convert_type(words & jnp.int32(-65536), F32)
        for s in range(V_ROW_WORDS):
            lo = slice(s * LANES, (s + 1) * LANES)
            hi = slice(half + s * LANES, half + (s + 1) * LANES)
            out_ref[pl.ds(t0, tg), lo] = x1_ref[pl.ds(t0, tg), lo] + acc_lo[s]
            out_ref[pl.ds(t0, tg), hi] = x1_ref[pl.ds(t0, tg), hi] + acc_hi[s]
        return carry
    lax.fori_loop(0, tt // tg, group, 0)


def _peer_v(ids, hw, x1, v_packed, tt):
    n = ids.shape[0]
    hk = P_HEADS * P_TOPK
    per = hk // V_ID_SPLIT
    id_parts = [ids[:, k * per:(k + 1) * per].reshape(n * per) for k in range(V_ID_SPLIT)]
    return pl.pallas_call(
        functools.partial(_peer_v_kernel, tt=tt),
        grid=(n // tt,),
        in_specs=[pl.BlockSpec((tt * per,), lambda i: (i,), memory_space=pltpu.SMEM)] * V_ID_SPLIT + [
            pl.BlockSpec((tt, hk), lambda i: (i, 0)),
            pl.BlockSpec((tt, D_MODEL), lambda i: (i, 0)),
            pl.BlockSpec((N_EXPERTS * V_ROW_WORDS, LANES), lambda i: (0, 0),
                         pipeline_mode=pl.Buffered(1)),
        ],
        out_specs=pl.BlockSpec((tt, D_MODEL), lambda i: (i, 0)),
        out_shape=jax.ShapeDtypeStruct((n, D_MODEL), F32),
        scratch_shapes=[pltpu.VMEM((V_SLABS, V_ROW_WORDS * (V_GROUP + SUBLANES), LANES), I32)],
        compiler_params=_cparams(("arbitrary",)),
        name="peer_v",
    )(*id_parts, hw, x1, v_packed)


def _pack_v(v_tab):
    vb = lax.bitcast_convert_type(v_tab.astype(BF16), jnp.uint16).astype(jnp.uint32)
    half = D_MODEL // 2
    packed = vb[:, :half] | (vb[:, half:] << 16)
    return lax.bitcast_convert_type(packed, I32).reshape(N_EXPERTS * V_ROW_WORDS, LANES)


def _layer(x2, batch, seq, norm1_g, w_in, b_in, conv_w, m_norm_g, qn_g, kn_g, w_m_out, w_f_out,
           w_out, norm2_g, w_pq, sub_keys, u_tab, v_tab):
    o = _OFFS
    cols = lambda a: jnp.concatenate([a[..., o[0]:o[4]], a[..., o[6]:o[9]], a[..., o[10]:o[12]]], axis=-1)
    small = lambda a: jnp.concatenate([a[..., o[4]:o[6]], a[..., o[9]:o[10]]], axis=-1)
    w_main = cols(w_in).astype(BF16)
    b_main = cols(b_in)[None, :]
    ws = small(w_in)
    bs = small(b_in)
    pad = LANES - ws.shape[1]
    ws = jnp.pad(ws, ((0, 0), (0, pad))).astype(BF16)
    bs = jnp.pad(bs, (0, pad))

    tm_a = min(512, x2.shape[0])
    z, gates_col, gates_row = _inproj(x2, norm1_g[None, :], w_main, b_main, ws, ws.T,
                                      bs[None, :], bs[:, None], tm_a)
    ccol = _forget_cumsum(gates_row, batch, seq)
    chunk = min(256, seq)
    hm = _mlstm(z, gates_col, gates_row, conv_w, m_norm_g[None, :], batch, seq, chunk)
    tq = min(512, seq)
    hf = _fox(z, ccol, qn_g[None, :], kn_g[None, :], batch, seq, tq, 2)

    sk = sub_keys.reshape(2 * P_HEADS, P_KEYS, P_HALF).astype(BF16)
    tm_d = min(512, x2.shape[0])
    x1, xn, ids_i, ids_j, gate = _merge(hm, hf, z, x2, w_m_out.astype(BF16), w_f_out.astype(BF16),
                                        w_out.astype(BF16), norm2_g[None, :], w_pq.astype(BF16), sk,
                                        tm_d, 512)
    tm_e = min(1024, x2.shape[0])
    hw, ids = _peer_u(xn, u_tab.astype(BF16), ids_i, ids_j, gate, tm_e, 2048)
    return _peer_v(ids, hw, x1, _pack_v(v_tab), min(128, x1.shape[0]))


def kernel(x, norm1_g, w_in, b_in, conv_w, m_norm_g, qn_g, kn_g, w_m_out, w_f_out, w_out,
           norm2_g, w_pq, sub_keys, u_tab, v_tab):
    batch, seq, d = x.shape
    x2 = x.reshape(batch * seq, d)
    for l in range(w_in.shape[0]):
        x2 = _layer(x2, batch, seq, norm1_g[l], w_in[l], b_in[l], conv_w[l], m_norm_g[l], qn_g[l],
                    kn_g[l], w_m_out[l], w_f_out[l], w_out[l], norm2_g[l], w_pq[l], sub_keys[l],
                    u_tab[l], v_tab[l])
    return x2.reshape(batch, seq, d)
```

```python
import functools

import jax
import jax.numpy as jnp
from jax import lax
from jax.experimental import pallas as pl
from jax.experimental.pallas import tpu as pltpu

F32 = jnp.float32
BF16 = jnp.bfloat16
I32 = jnp.int32

D_MODEL = 1024
M_HEADS = 4
M_HEAD_DIM = 256
F_HEADS = 8
F_HEAD_DIM = 128
CONV_WIDTH = 4
P_HEADS = 8
P_KEYS = 128
P_TOPK = 16
P_HALF = 128
N_EXPERTS = P_KEYS * P_KEYS
EPS = 1e-6
M_INIT = -1e30
LANES = 128
SUBLANES = 8
VMEM_LIMIT = 56 * 1024 * 1024

_SIZES = (1024, 1024, 1024, 1024, M_HEADS, M_HEADS, 1024, 1024, 1024, F_HEADS, 1024, 1024)
_OFFS = [0]
for _s in _SIZES:
    _OFFS.append(_OFFS[-1] + _s)
Z_COLS = 9 * 1024
ZB_MQ, ZB_MK, ZB_MV, ZB_MO, ZB_FQ, ZB_FK, ZB_FV, ZB_GM, ZB_GF = range(9)
G_MI, G_MF, G_FF = 0, M_HEADS, 2 * M_HEADS


def _cparams(sem):
    return pltpu.CompilerParams(dimension_semantics=sem, vmem_limit_bytes=VMEM_LIMIT)


def _log_sigmoid(x):
    return jnp.minimum(x, 0.0) - jnp.log1p(jnp.exp(-jnp.abs(x)))


def _sigmoid(x):
    return 0.5 * jnp.tanh(0.5 * x) + 0.5


INPROJ_TN = 3072


def _inproj_kernel(x_ref, g_ref, w_ref, b_ref, ws_ref, wst_ref, bs_ref, bst_ref,
                   z_ref, gc_ref, gr_ref):
    x = x_ref[...]
    r = lax.rsqrt(jnp.mean(x * x, axis=-1, keepdims=True) + EPS)
    h = ((x * r) * g_ref[...]).astype(BF16)
    gc_ref[...] = jnp.dot(h, ws_ref[...], preferred_element_type=F32) + bs_ref[...]
    gr_ref[...] = lax.dot_general(wst_ref[...], h, (((1,), (1,)), ((), ())),
                                  preferred_element_type=F32) + bst_ref[...]
    for c in range(Z_COLS // INPROJ_TN):
        cols = slice(c * INPROJ_TN, (c + 1) * INPROJ_TN)
        z = jnp.dot(h, w_ref[:, cols], preferred_element_type=F32) + b_ref[:, cols]
        z_ref[:, cols] = z.astype(BF16)


def _inproj(x2, g1, w_main, b_main, w_small, w_small_t, b_small, b_small_t, tm):
    n = x2.shape[0]
    const = lambda i: (0, 0)
    return pl.pallas_call(
        _inproj_kernel,
        grid=(n // tm,),
        in_specs=[
            pl.BlockSpec((tm, D_MODEL), lambda i: (i, 0)),
            pl.BlockSpec((1, D_MODEL), const),
            pl.BlockSpec((D_MODEL, Z_COLS), const, pipeline_mode=pl.Buffered(1)),
            pl.BlockSpec((1, Z_COLS), const),
            pl.BlockSpec((D_MODEL, LANES), const),
            pl.BlockSpec((LANES, D_MODEL), const),
            pl.BlockSpec((1, LANES), const),
            pl.BlockSpec((LANES, 1), const),
        ],
        out_specs=[
            pl.BlockSpec((tm, Z_COLS), lambda i: (i, 0)),
            pl.BlockSpec((tm, LANES), lambda i: (i, 0)),
            pl.BlockSpec((LANES, tm), lambda i: (0, i)),
        ],
        out_shape=[
            jax.ShapeDtypeStruct((n, Z_COLS), BF16),
            jax.ShapeDtypeStruct((n, LANES), F32),
            jax.ShapeDtypeStruct((LANES, n), F32),
        ],
        compiler_params=_cparams(("parallel",)),
        name="inproj",
    )(x2, g1, w_main, b_main, w_small, w_small_t, b_small, b_small_t)


def _cumsum_kernel(gr_ref, ccol_ref):
    x = _log_sigmoid(gr_ref[...])
    t = x.shape[1]
    lane = lax.broadcasted_iota(I32, x.shape, 1)
    s = 1
    while s < t:
        x = x + jnp.where(lane >= s, pltpu.roll(x, s, axis=1), 0.0)
        s *= 2
    ccol_ref[...] = x.T


def _forget_cumsum(gates_row, batch, seq):
    n = batch * seq
    return pl.pallas_call(
        _cumsum_kernel,
        grid=(batch,),
        in_specs=[pl.BlockSpec((LANES, seq), lambda b: (0, b))],
        out_specs=pl.BlockSpec((seq, LANES), lambda b: (b, 0)),
        out_shape=jax.ShapeDtypeStruct((n, LANES), F32),
        compiler_params=_cparams(("parallel",)),
        name="forget_cumsum",
    )(gates_row)


def _mlstm_kernel(q_ref, k_ref, v_ref, o_ref, gc_ref, gr_ref, cw_ref, mg_ref, out_ref,
                  c_sc, n_sc, m_sc, tail_sc, qk_sc, *, chunk):
    L = chunk
    dh = M_HEAD_DIM
    t = pl.program_id(1)

    @pl.when(t == 0)
    def _():
        c_sc[...] = jnp.zeros_like(c_sc)
        n_sc[...] = jnp.zeros_like(n_sc)
        m_sc[...] = jnp.full_like(m_sc, M_INIT)
        tail_sc[...] = jnp.zeros_like(tail_sc)

    gc = gc_ref[...]
    gr = gr_ref[...]
    lsg_c = _log_sigmoid(gc)
    lsg_r = _log_sigmoid(gr[0:2 * M_HEADS, :])
    row_i = lax.broadcasted_iota(I32, (L, L), 0)
    col_i = lax.broadcasted_iota(I32, (L, L), 1)
    tril = col_i <= row_i
    shift = [(row_i - col_i == d).astype(BF16) for d in range(1, CONV_WIDTH)]
    row8 = lax.broadcasted_iota(I32, (SUBLANES, dh), 0)

    def conv_silu(x_ref, cx, c0):
        x = x_ref[:, cx:cx + dh]
        tail = tail_sc[:, c0:c0 + dh]
        acc = x.astype(F32) * cw_ref[CONV_WIDTH - 1:CONV_WIDTH, c0:c0 + dh]
        for d in range(1, CONV_WIDTH):
            xs = jnp.dot(shift[d - 1], x, preferred_element_type=F32)
            head = jnp.where(row8 < d, pltpu.roll(tail, d, axis=0), xs[0:SUBLANES, :])
            xs = jnp.concatenate([head, xs[SUBLANES:, :]], axis=0)
            acc = acc + xs * cw_ref[CONV_WIDTH - 1 - d:CONV_WIDTH - d, c0:c0 + dh]
        tail_sc[:, c0:c0 + dh] = x[L - SUBLANES:L, :].astype(F32)
        return acc * _sigmoid(acc)

    for h in range(M_HEADS):
        qk_sc[:, h * dh:(h + 1) * dh] = conv_silu(q_ref, h * dh, h * dh) * (dh ** -0.5)
        qk_sc[:, D_MODEL + h * dh:D_MODEL + (h + 1) * dh] = conv_silu(k_ref, h * dh, D_MODEL + h * dh)

    for h in range(M_HEADS):
        q = qk_sc[:, h * dh:(h + 1) * dh]
        k = qk_sc[:, D_MODEL + h * dh:D_MODEL + (h + 1) * dh]
        v = v_ref[:, h * dh:(h + 1) * dh]
        qb = q.astype(BF16)
        kb = k.astype(BF16)

        i_col = gc[:, G_MI + h:G_MI + h + 1]
        lf_col = lsg_c[:, G_MF + h:G_MF + h + 1]
        i_row = gr[G_MI + h:G_MI + h + 1, :]
        lf_row = lsg_r[G_MF + h:G_MF + h + 1, :]

        b_col = jnp.sum(jnp.where(tril, lf_row, 0.0), axis=1, keepdims=True)
        b_row = jnp.sum(jnp.where(row_i <= col_i, lf_col, 0.0), axis=0, keepdims=True)
        g = jnp.sum(lf_row, axis=1, keepdims=True)
        m_prev = m_sc[h][0:1, 0:1]

        dmat = jnp.where(tril, b_col - b_row + i_row, -jnp.inf)
        inter = b_col + m_prev
        m_t = jnp.maximum(jnp.max(dmat, axis=1, keepdims=True), inter)
        p = jnp.exp(dmat - m_t)
        s = lax.dot_general(qb, kb, (((1,), (1,)), ((), ())), preferred_element_type=F32) * p
        w_inter = jnp.exp(inter - m_t)
        c_old = c_sc[h]
        n_old = n_sc[h]
        num = (jnp.dot(s.astype(BF16), v, preferred_element_type=F32)
               + w_inter * jnp.dot(qb, c_old.astype(BF16), preferred_element_type=F32))
        den = (jnp.sum(s, axis=1, keepdims=True)
               + w_inter * jnp.sum(q * n_old, axis=1, keepdims=True))
        hh = num / jnp.maximum(jnp.abs(den), jnp.exp(-m_t))

        log_w = g - b_col + i_col
        m_new = jnp.maximum(g + m_prev, jnp.max(log_w, axis=0, keepdims=True))
        w_s = jnp.exp(log_w - m_new)
        decay = jnp.exp(g + m_prev - m_new)
        wv = (w_s * v.astype(F32)).astype(BF16)
        c_sc[h] = decay * c_old + lax.dot_general(kb, wv, (((0,), (0,)), ((), ())),
                                                  preferred_element_type=F32)
        n_sc[h] = decay * n_old + jnp.sum(w_s * k, axis=0, keepdims=True)
        m_sc[h] = jnp.broadcast_to(m_new, (SUBLANES, LANES))

        r = lax.rsqrt(jnp.mean(hh * hh, axis=-1, keepdims=True) + EPS)
        hn = (hh * r) * mg_ref[0:1, h * dh:(h + 1) * dh]
        og = _sigmoid(o_ref[:, h * dh:(h + 1) * dh].astype(F32))
        out_ref[:, h * dh:(h + 1) * dh] = (hn * og).astype(BF16)


def _mlstm(z, gates_col, gates_row, conv_w, m_norm_g, batch, seq, chunk):
    n = batch * seq
    nt = seq // chunk
    zspec = lambda cb: pl.BlockSpec((chunk, D_MODEL), lambda b, t, cb=cb: (b * nt + t, cb))
    return pl.pallas_call(
        functools.partial(_mlstm_kernel, chunk=chunk),
        grid=(batch, nt),
        in_specs=[
            zspec(ZB_MQ), zspec(ZB_MK), zspec(ZB_MV), zspec(ZB_MO),
            pl.BlockSpec((chunk, LANES), lambda b, t: (b * nt + t, 0)),
            pl.BlockSpec((LANES, chunk), lambda b, t: (0, b * nt + t)),
            pl.BlockSpec((CONV_WIDTH, 2 * D_MODEL), lambda b, t: (0, 0)),
            pl.BlockSpec((1, D_MODEL), lambda b, t: (0, 0)),
        ],
        out_specs=pl.BlockSpec((chunk, D_MODEL), lambda b, t: (b * nt + t, 0)),
        out_shape=jax.ShapeDtypeStruct((n, D_MODEL), BF16),
        scratch_shapes=[
            pltpu.VMEM((M_HEADS, M_HEAD_DIM, M_HEAD_DIM), F32),
            pltpu.VMEM((M_HEADS, 1, M_HEAD_DIM), F32),
            pltpu.VMEM((M_HEADS, SUBLANES, LANES), F32),
            pltpu.VMEM((SUBLANES, 2 * D_MODEL), F32),
            pltpu.VMEM((chunk, 2 * D_MODEL), F32),
        ],
        compiler_params=_cparams(("parallel", "arbitrary")),
        name="mlstm",
    )(z, z, z, z, gates_col, gates_row, conv_w, m_norm_g)


LOG2E = 1.4426950408889634
FOX_SAFE_LOG2_RANGE = 100.0


def _split3(c):
    hi = c.astype(BF16).astype(F32)
    r1 = c - hi
    mid = r1.astype(BF16).astype(F32)
    lo = (r1 - mid).astype(BF16).astype(F32)
    return hi, mid, lo


def _fox_aug(c, lane, is_query, shift):
    hi, mid, lo = _split3(c if is_query else -c)
    off = 0 if is_query else 3
    ex = jnp.where(lane == off, hi, jnp.where(lane == off + 1, mid, jnp.where(lane == off + 2, lo, 0.0)))
    ones_at = (lane >= 3) & (lane < 6) if is_query else lane < 3
    ex = jnp.where(ones_at, 1.0, ex)
    return jnp.where(lane == 6, -shift if is_query else 1.0, ex).astype(BF16)


def _fox_kernel(bnd_ref, q_ref, k_ref, v_ref, ccol_ref, qg_ref, kg_ref, out_ref,
                ka_sc, va_sc, m_sc, acc_sc, s_sc, *, tq, seq, hp):
    g = pl.program_id(1)
    i = pl.program_id(2)
    dh = F_HEAD_DIM
    lane = lax.broadcasted_iota(I32, (tq, LANES), 1)
    shift = bnd_ref[0]
    safe = bnd_ref[1] > 0.5

    def c_of(r0, hh):
        cc = ccol_ref[pl.ds(r0, tq), :]
        return jnp.sum(jnp.where(lane == G_FF + g * hp + hh, cc, 0.0), axis=1, keepdims=True) * LOG2E

    @pl.when(i == 0)
    def _():
        def prep_chunk(c, carry):
            r0 = pl.multiple_of(c * tq, tq)
            for hh in range(hp):
                kk = k_ref[pl.ds(r0, tq), hh * dh:(hh + 1) * dh].astype(F32)
                r = lax.rsqrt(jnp.mean(kk * kk, axis=-1, keepdims=True) + EPS)
                ka_sc[hh, pl.ds(r0, tq), 0:dh] = ((kk * r) * kg_ref[...]).astype(BF16)
                ka_sc[hh, pl.ds(r0, tq), dh:2 * dh] = _fox_aug(c_of(r0, hh), lane, False, shift)
                va_sc[hh, pl.ds(r0, tq), 0:dh] = v_ref[pl.ds(r0, tq), hh * dh:(hh + 1) * dh]
                va_sc[hh, pl.ds(r0, tq), dh:2 * dh] = jnp.ones((tq, dh), BF16)
            return carry
        lax.fori_loop(0, seq // tq, prep_chunk, 0)

    q0 = pl.multiple_of(i * tq, tq)
    qa = []
    for hh in range(hp):
        q = q_ref[:, hh * dh:(hh + 1) * dh].astype(F32)
        r = lax.rsqrt(jnp.mean(q * q, axis=-1, keepdims=True) + EPS)
        qn = (((q * r) * qg_ref[...]) * (dh ** -0.5 * LOG2E)).astype(BF16)
        qa.append(jnp.concatenate([qn, _fox_aug(c_of(q0, hh), lane, True, shift)], axis=1))

    acc_sc[...] = jnp.zeros_like(acc_sc)

    def scores(j, masked):
        r0 = pl.multiple_of(j * tq, tq)
        s = [lax.dot_general(qa[hh], ka_sc[hh, pl.ds(r0, tq), :], (((1,), (1,)), ((), ())),
                             preferred_element_type=F32) for hh in range(hp)]
        if masked:
            rr = lax.broadcasted_iota(I32, (tq, tq), 0)
            cc = lax.broadcasted_iota(I32, (tq, tq), 1)
            s = [jnp.where(rr >= cc, sh, -jnp.inf) for sh in s]
        return r0, s

    def fixed_consume(j, s):
        r0 = pl.multiple_of(j * tq, tq)
        for hh in range(hp):
            p = jnp.exp2(s[hh]).astype(BF16)
            acc_sc[hh] += jnp.dot(p, va_sc[hh, pl.ds(r0, tq), :], preferred_element_type=F32)

    def run_fixed():
        def produce(buf, j, masked):
            _, s = scores(j, masked)
            for hh in range(hp):
                s_sc[buf, hh] = s[hh]

        def consume(buf, j):
            fixed_consume(j, [s_sc[buf, hh] for hh in range(hp)])

        produce(0, i, True)
        pairs = i // 2

        def body(k, carry):
            produce(1, 2 * k, False)
            consume(0, jnp.where(k == 0, i, 2 * k - 1))
            produce(0, 2 * k + 1, False)
            consume(1, 2 * k)
            return carry
        lax.fori_loop(0, pairs, body, 0)
        pending = jnp.where(pairs == 0, i, 2 * pairs - 1)

        @pl.when(i % 2 == 1)
        def _():
            produce(1, i - 1, False)
            consume(0, pending)
            consume(1, i - 1)

        @pl.when(i % 2 == 0)
        def _():
            consume(0, pending)

    def running_step(j, masked):
        r0, s = scores(j, masked)
        for hh in range(hp):
            m_old = m_sc[hh]
            m_new = jnp.maximum(m_old, jnp.max(s[hh], axis=1, keepdims=True))
            p = jnp.exp2(s[hh] - m_new).astype(BF16)
            alpha = jnp.exp2(m_old - m_new)
            acc_sc[hh] = alpha * acc_sc[hh] + jnp.dot(p, va_sc[hh, pl.ds(r0, tq), :],
                                                      preferred_element_type=F32)
            m_sc[hh] = m_new

    def run(step):
        def body(j, carry):
            step(j, False)
            return carry
        lax.fori_loop(0, i, body, 0)
        step(i, True)

    @pl.when(safe)
    def _():
        run_fixed()

    @pl.when(jnp.logical_not(safe))
    def _():
        m_sc[...] = jnp.full_like(m_sc, -jnp.inf)
        run(running_step)

    for hh in range(hp):
        out_ref[:, hh * dh:(hh + 1) * dh] = (acc_sc[hh, :, 0:dh] / acc_sc[hh, :, dh:2 * dh]).astype(BF16)


def _fox(z, ccol, qn_g, kn_g, batch, seq, tq, hp):
    n = batch * seq
    nq = seq // tq
    w = hp * F_HEAD_DIM
    fq0 = ZB_FQ * (D_MODEL // w)
    fk0 = ZB_FK * (D_MODEL // w)
    fv0 = ZB_FV * (D_MODEL // w)
    shift = jnp.ceil(1.01 * LOG2E * F_HEAD_DIM ** 0.5 * jnp.max(jnp.abs(qn_g)) * jnp.max(jnp.abs(kn_g)))
    safe = (2.0 * shift <= FOX_SAFE_LOG2_RANGE).astype(F32)
    bnd = jnp.stack([shift, safe]).astype(F32)
    return pl.pallas_call(
        functools.partial(_fox_kernel, tq=tq, seq=seq, hp=hp),
        grid=(batch, F_HEADS // hp, nq),
        in_specs=[
            pl.BlockSpec(memory_space=pltpu.SMEM),
            pl.BlockSpec((tq, w), lambda b, g, i: (b * nq + i, fq0 + g)),
            pl.BlockSpec((seq, w), lambda b, g, i: (b, fk0 + g)),
            pl.BlockSpec((seq, w), lambda b, g, i: (b, fv0 + g)),
            pl.BlockSpec((seq, LANES), lambda b, g, i: (b, 0)),
            pl.BlockSpec((1, F_HEAD_DIM), lambda b, g, i: (0, 0)),
            pl.BlockSpec((1, F_HEAD_DIM), lambda b, g, i: (0, 0)),
        ],
        out_specs=pl.BlockSpec((tq, w), lambda b, g, i: (b * nq + i, g)),
        out_shape=jax.ShapeDtypeStruct((n, D_MODEL), BF16),
        scratch_shapes=[
            pltpu.VMEM((hp, seq, 2 * F_HEAD_DIM), BF16),
            pltpu.VMEM((hp, seq, 2 * F_HEAD_DIM), BF16),
            pltpu.VMEM((hp, tq, 1), F32),
            pltpu.VMEM((hp, tq, 2 * F_HEAD_DIM), F32),
            pltpu.VMEM((2, hp, tq, tq), F32),
        ],
        compiler_params=_cparams(("parallel", "parallel", "arbitrary")),
        name="fox",
    )(bnd, z, z, z, ccol, qn_g, kn_g)


def _topk_rows(s, k, store):
    rows = lax.broadcasted_iota(I32, s.shape, 0).astype(F32)
    n_rows = float(s.shape[0])
    for r in range(k):
        mx = jnp.max(s, axis=0, keepdims=True)
        ix = jnp.min(jnp.where(s == mx, rows, n_rows), axis=0, keepdims=True)
        s = jnp.where(rows == ix, -jnp.inf, s)
        store(r, mx, ix.astype(I32))


def _pair_candidates(v0, v1):
    k = P_TOPK
    w = v0.shape[1]
    row8 = lax.broadcasted_iota(I32, (SUBLANES, w), 0)
    sums = [v0[0:1, :] + v1]
    for a in range(1, SUBLANES):
        nb = k // (a + 1)
        sums.append(jnp.where(row8 < nb, v0[a:a + 1, :] + v1[0:SUBLANES, :], -jnp.inf))
    sums.append(v0[SUBLANES:k, :] + v1[0:1, :])
    return jnp.concatenate(sums, axis=0)


def _pair_of_row(row):
    k = P_TOPK
    q = row - k
    mid_a = 1 + (q >> 3)
    mid_b = q & (SUBLANES - 1)
    last = row >= k + SUBLANES * (SUBLANES - 1)
    a = jnp.where(row < k, 0, jnp.where(last, row - (k + SUBLANES * (SUBLANES - 2)), mid_a))
    b = jnp.where(row < k, row, jnp.where(last, 0, mid_b))
    return a, b


def _merge_kernel(hm_ref, hf_ref, gm_ref, gf_ref, x_ref, wm_ref, wf_ref, wo_ref, g2_ref,
                  wpq_ref, sk_ref, x1_ref, xn_ref, ids_i_ref, ids_j_ref, gate_ref,
                  sc_sc, val_sc, idx_sc, top_sc, pos_sc, oi_sc, oj_sc, og_sc, *, tm, tc):
    ym = jnp.dot(hm_ref[...], wm_ref[...], preferred_element_type=F32)
    yf = jnp.dot(hf_ref[...], wf_ref[...], preferred_element_type=F32)
    y = (_sigmoid(gm_ref[...].astype(F32)) * ym + _sigmoid(gf_ref[...].astype(F32)) * yf)
    x1 = x_ref[...] + jnp.dot(y.astype(BF16), wo_ref[...], preferred_element_type=F32)
    x1_ref[...] = x1
    r = lax.rsqrt(jnp.mean(x1 * x1, axis=-1, keepdims=True) + EPS)
    xn = ((x1 * r) * g2_ref[...]).astype(BF16)
    xn_ref[...] = xn
    qh = jnp.dot(xn, wpq_ref[...], preferred_element_type=F32).astype(BF16)

    for hc in range(2 * P_HEADS):
        sc_sc[hc] = lax.dot_general(sk_ref[hc], qh[:, hc * P_HALF:(hc + 1) * P_HALF],
                                    (((1,), (1,)), ((), ())), preferred_element_type=F32)

    def chunk_body(c, carry):
        c0 = pl.multiple_of(c * tc, tc)

        def first_level(hc, carry2):
            s = sc_sc[hc, :, pl.ds(c0, tc)]

            def store(r, val, idx):
                val_sc[hc, r:r + 1, :] = val
                idx_sc[hc, r:r + 1, :] = idx
            _topk_rows(s, P_TOPK, store)
            return carry2
        lax.fori_loop(0, 2 * P_HEADS, first_level, 0)

        def second_level(hd, carry2):
            v0 = val_sc[2 * hd]
            v1 = val_sc[2 * hd + 1]
            i0 = idx_sc[2 * hd]
            i1 = idx_sc[2 * hd + 1]
            cand = _pair_candidates(v0, v1)

            def store(r, val, idx):
                top_sc[r:r + 1, :] = val
                pos_sc[r:r + 1, :] = idx
            _topk_rows(cand, P_TOPK, store)
            top = top_sc[...]
            pa, pb = _pair_of_row(pos_sc[...])
            sel_i = jnp.zeros(pa.shape, I32)
            sel_j = jnp.zeros(pa.shape, I32)
            for a in range(P_TOPK):
                sel_i = jnp.where(pa == a, i0[a:a + 1, :], sel_i)
                sel_j = jnp.where(pb == a, i1[a:a + 1, :], sel_j)
            e = jnp.exp(top - top[0:1, :])
            gate = e / jnp.sum(e, axis=0, keepdims=True)
            r0 = pl.multiple_of(hd * P_TOPK, P_TOPK)
            oi_sc[pl.ds(r0, P_TOPK), :] = sel_i
            oj_sc[pl.ds(r0, P_TOPK), :] = sel_j
            og_sc[pl.ds(r0, P_TOPK), :] = gate
            return carry2
        lax.fori_loop(0, P_HEADS, second_level, 0)

        ids_i_ref[pl.ds(c0, tc), :] = oi_sc[...].T
        ids_j_ref[pl.ds(c0, tc), :] = oj_sc[...].T
        gate_ref[pl.ds(c0, tc), :] = og_sc[...].T
        return carry
    lax.fori_loop(0, tm // tc, chunk_body, 0)


def _merge(hm, hf, z, x2, w_m, w_f, w_o, g2, w_pq, sk, tm, tc):
    n = x2.shape[0]
    hk = P_HEADS * P_TOPK
    row = lambda i: (i, 0)
    const = lambda i: (0, 0)
    return pl.pallas_call(
        functools.partial(_merge_kernel, tm=tm, tc=tc),
        grid=(n // tm,),
        in_specs=[
            pl.BlockSpec((tm, D_MODEL), row),
            pl.BlockSpec((tm, D_MODEL), row),
            pl.BlockSpec((tm, D_MODEL), lambda i: (i, ZB_GM)),
            pl.BlockSpec((tm, D_MODEL), lambda i: (i, ZB_GF)),
            pl.BlockSpec((tm, D_MODEL), row),
            pl.BlockSpec((D_MODEL, D_MODEL), const),
            pl.BlockSpec((D_MODEL, D_MODEL), const),
            pl.BlockSpec((D_MODEL, D_MODEL), const),
            pl.BlockSpec((1, D_MODEL), const),
            pl.BlockSpec((D_MODEL, 2 * P_HEADS * P_HALF), const),
            pl.BlockSpec((2 * P_HEADS, P_KEYS, P_HALF), lambda i: (0, 0, 0)),
        ],
        out_specs=[
            pl.BlockSpec((tm, D_MODEL), row),
            pl.BlockSpec((tm, D_MODEL), row),
            pl.BlockSpec((tm, hk), row),
            pl.BlockSpec((tm, hk), row),
            pl.BlockSpec((tm, hk), row),
        ],
        out_shape=[
            jax.ShapeDtypeStruct((n, D_MODEL), F32),
            jax.ShapeDtypeStruct((n, D_MODEL), BF16),
            jax.ShapeDtypeStruct((n, hk), I32),
            jax.ShapeDtypeStruct((n, hk), I32),
            jax.ShapeDtypeStruct((n, hk), F32),
        ],
        scratch_shapes=[
            pltpu.VMEM((2 * P_HEADS, P_KEYS, tm), F32),
            pltpu.VMEM((2 * P_HEADS, P_TOPK, tc), F32),
            pltpu.VMEM((2 * P_HEADS, P_TOPK, tc), I32),
            pltpu.VMEM((P_TOPK, tc), F32),
            pltpu.VMEM((P_TOPK, tc), I32),
            pltpu.VMEM((hk, tc), I32),
            pltpu.VMEM((hk, tc), I32),
            pltpu.VMEM((hk, tc), F32),
        ],
        compiler_params=_cparams(("parallel",)),
        name="merge_retrieve",
    )(hm, hf, z, z, x2, w_m, w_f, w_o, g2, w_pq, sk)


U_SUB = 512


def _peer_u_kernel(xn_ref, u_ref, ii_ref, jj_ref, gate_ref, hw_ref, ids_ref, acc_sc, *, eb):
    e = pl.program_id(1)
    ne = pl.num_programs(1)

    @pl.when(e == 0)
    def _():
        acc_sc[...] = jnp.zeros_like(acc_sc)

    xn = xn_ref[...]
    ii = ii_ref[...]
    jj = jj_ref[...]
    acc = acc_sc[...]
    for sb in range(eb // U_SUB):
        a = lax.dot_general(xn, u_ref[sb * U_SUB:(sb + 1) * U_SUB, :], (((1,), (1,)), ((), ())),
                            preferred_element_type=F32)
        for c in range(U_SUB // P_KEYS):
            blk = e * (eb // P_KEYS) + sb * (U_SUB // P_KEYS) + c
            g = jnp.take_along_axis(a[:, c * P_KEYS:(c + 1) * P_KEYS], jj, axis=1,
                                    mode="promise_in_bounds")
            acc = jnp.where(ii == blk, g, acc)
    acc_sc[...] = acc

    @pl.when(e == ne - 1)
    def _():
        av = acc_sc[...]
        gelu = 0.5 * av * (1.0 + lax.erf(av * (2.0 ** -0.5)))
        hw_ref[...] = gate_ref[...] * gelu
        ids_ref[...] = (ii * P_KEYS + jj) * (D_MODEL // (2 * LANES))


def _peer_u(xn, u_bf, ids_i, ids_j, gate, tm, eb):
    n = xn.shape[0]
    hk = P_HEADS * P_TOPK
    row = lambda i, e: (i, 0)
    return pl.pallas_call(
        functools.partial(_peer_u_kernel, eb=eb),
        grid=(n // tm, N_EXPERTS // eb),
        in_specs=[
            pl.BlockSpec((tm, D_MODEL), row),
            pl.BlockSpec((eb, D_MODEL), lambda i, e: (e, 0)),
            pl.BlockSpec((tm, hk), row),
            pl.BlockSpec((tm, hk), row),
            pl.BlockSpec((tm, hk), row),
        ],
        out_specs=[pl.BlockSpec((tm, hk), row), pl.BlockSpec((tm, hk), row)],
        out_shape=[jax.ShapeDtypeStruct((n, hk), F32), jax.ShapeDtypeStruct((n, hk), I32)],
        scratch_shapes=[pltpu.VMEM((tm, hk), F32)],
        compiler_params=_cparams(("parallel", "arbitrary")),
        name="peer_u",
    )(xn, u_bf, ids_i, ids_j, gate)


V_ROW_WORDS = D_MODEL // (2 * LANES)


V_SLABS = 128
V_GROUP = 16
V_ID_SPLIT = 4


def _peer_v_kernel(*refs, tt):
    id_refs = refs[:V_ID_SPLIT]
    w_ref, x1_ref, tab_ref, out_ref, slab_sc = refs[V_ID_SPLIT:]
    hk = P_HEADS * P_TOPK
    per = hk // V_ID_SPLIT
    tg = V_GROUP
    plane = tg + SUBLANES
    half = D_MODEL // 2

    def group(g, carry):
        t0 = pl.multiple_of(g * tg, tg)
        wg = w_ref[pl.ds(t0, tg), :]
        acc_lo = [jnp.zeros((tg, LANES), F32) for _ in range(V_ROW_WORDS)]
        acc_hi = [jnp.zeros((tg, LANES), F32) for _ in range(V_ROW_WORDS)]
        for n_slot in range(hk):
            q, k = divmod(n_slot, V_ID_SPLIT)
            m = k * per + q
            slab = slab_sc.at[n_slot % V_SLABS]
            for t in range(tg):
                r0 = pl.multiple_of(id_refs[k][t0 * per + (t * per + q)], V_ROW_WORDS)
                slab[pl.ds(t, V_ROW_WORDS, stride=plane), :] = tab_ref[pl.ds(r0, V_ROW_WORDS), :]
            wcol = wg[:, m:m + 1]
            for s in range(V_ROW_WORDS):
                words = slab[s * plane:s * plane + tg, :]
                acc_lo[s] = acc_lo[s] + wcol * lax.bitcast_convert_type(words << 16, F32)
                acc_hi[s] = acc_hi[s] + wcol * lax.bitcast_convert_type(words & jnp.int32(-65536), F32)
        for s in range(V_ROW_WORDS):
            lo = slice(s * LANES, (s + 1) * LANES)
            hi = slice(half + s * LANES, half + (s + 1) * LANES)
            out_ref[pl.ds(t0, tg), lo] = x1_ref[pl.ds(t0, tg), lo] + acc_lo[s]
            out_ref[pl.ds(t0, tg), hi] = x1_ref[pl.ds(t0, tg), hi] + acc_hi[s]
        return carry
    lax.fori_loop(0, tt // tg, group, 0)


def _peer_v(ids, hw, x1, v_packed, tt):
    n = ids.shape[0]
    hk = P_HEADS * P_TOPK
    per = hk // V_ID_SPLIT
    id_parts = [ids[:, k * per:(k + 1) * per].reshape(n * per) for k in range(V_ID_SPLIT)]
    return pl.pallas_call(
        functools.partial(_peer_v_kernel, tt=tt),
        grid=(n // tt,),
        in_specs=[pl.BlockSpec((tt * per,), lambda i: (i,), memory_space=pltpu.SMEM)] * V_ID_SPLIT + [
            pl.BlockSpec((tt, hk), lambda i: (i, 0)),
            pl.BlockSpec((tt, D_MODEL), lambda i: (i, 0)),
            pl.BlockSpec((N_EXPERTS * V_ROW_WORDS, LANES), lambda i: (0, 0),
                         pipeline_mode=pl.Buffered(1)),
        ],
        out_specs=pl.BlockSpec((tt, D_MODEL), lambda i: (i, 0)),
        out_shape=jax.ShapeDtypeStruct((n, D_MODEL), F32),
        scratch_shapes=[pltpu.VMEM((V_SLABS, V_ROW_WORDS * (V_GROUP + SUBLANES), LANES), I32)],
        compiler_params=_cparams(("arbitrary",)),
        name="peer_v",
    )(*id_parts, hw, x1, v_packed)


def _pack_v(v_tab):
    vb = lax.bitcast_convert_type(v_tab.astype(BF16), jnp.uint16).astype(jnp.uint32)
    half = D_MODEL // 2
    packed = vb[:, :half] | (vb[:, half:] << 16)
    return lax.bitcast_convert_type(packed, I32).reshape(N_EXPERTS * V_ROW_WORDS, LANES)


def _layer(x2, batch, seq, norm1_g, w_in, b_in, conv_w, m_norm_g, qn_g, kn_g, w_m_out, w_f_out,
           w_out, norm2_g, w_pq, sub_keys, u_tab, v_tab):
    o = _OFFS
    cols = lambda a: jnp.concatenate([a[..., o[0]:o[4]], a[..., o[6]:o[9]], a[..., o[10]:o[12]]], axis=-1)
    small = lambda a: jnp.concatenate([a[..., o[4]:o[6]], a[..., o[9]:o[10]]], axis=-1)
    w_main = cols(w_in).astype(BF16)
    b_main = cols(b_in)[None, :]
    ws = small(w_in)
    bs = small(b_in)
    pad = LANES - ws.shape[1]
    ws = jnp.pad(ws, ((0, 0), (0, pad))).astype(BF16)
    bs = jnp.pad(bs, (0, pad))

    tm_a = min(512, x2.shape[0])
    z, gates_col, gates_row = _inproj(x2, norm1_g[None, :], w_main, b_main, ws, ws.T,
                                      bs[None, :], bs[:, None], tm_a)
    ccol = _forget_cumsum(gates_row, batch, seq)
    chunk = min(256, seq)
    hm = _mlstm(z, gates_col, gates_row, conv_w, m_norm_g[None, :], batch, seq, chunk)
    tq = min(512, seq)
    hf = _fox(z, ccol, qn_g[None, :], kn_g[None, :], batch, seq, tq, 2)

    sk = sub_keys.reshape(2 * P_HEADS, P_KEYS, P_HALF).astype(BF16)
    tm_d = min(512, x2.shape[0])
    x1, xn, ids_i, ids_j, gate = _merge(hm, hf, z, x2, w_m_out.astype(BF16), w_f_out.astype(BF16),
                                        w_out.astype(BF16), norm2_g[None, :], w_pq.astype(BF16), sk,
                                        tm_d, 512)
    tm_e = min(1024, x2.shape[0])
    hw, ids = _peer_u(xn, u_tab.astype(BF16), ids_i, ids_j, gate, tm_e, 4096)
    return _peer_v(ids, hw, x1, _pack_v(v_tab), min(512, x1.shape[0]))


def kernel(x, norm1_g, w_in, b_in, conv_w, m_norm_g, qn_g, kn_g, w_m_out, w_f_out, w_out,
           norm2_g, w_pq, sub_keys, u_tab, v_tab):
    batch, seq, d = x.shape
    x2 = x.reshape(batch * seq, d)
    for l in range(w_in.shape[0]):
        x2 = _layer(x2, batch, seq, norm1_g[l], w_in[l], b_in[l], conv_w[l], m_norm_g[l], qn_g[l],
                    kn_g[l], w_m_out[l], w_f_out[l], w_out[l], norm2_g[l], w_pq[l], sub_keys[l],
                    u_tab[l], v_tab[l])
    return x2.reshape(batch, seq, d)
```

```python
import functools

import jax
import jax.numpy as jnp
from jax import lax
from jax.experimental import pallas as pl
from jax.experimental.pallas import tpu as pltpu

F32 = jnp.float32
BF16 = jnp.bfloat16
I32 = jnp.int32

D_MODEL = 1024
M_HEADS = 4
M_HEAD_DIM = 256
F_HEADS = 8
F_HEAD_DIM = 128
CONV_WIDTH = 4
P_HEADS = 8
P_KEYS = 128
P_TOPK = 16
P_HALF = 128
N_EXPERTS = P_KEYS * P_KEYS
EPS = 1e-6
M_INIT = -1e30
LANES = 128
SUBLANES = 8
VMEM_LIMIT = 56 * 1024 * 1024

_SIZES = (1024, 1024, 1024, 1024, M_HEADS, M_HEADS, 1024, 1024, 1024, F_HEADS, 1024, 1024)
_OFFS = [0]
for _s in _SIZES:
    _OFFS.append(_OFFS[-1] + _s)
Z_COLS = 9 * 1024
ZB_MQ, ZB_MK, ZB_MV, ZB_MO, ZB_FQ, ZB_FK, ZB_FV, ZB_GM, ZB_GF = range(9)
G_MI, G_MF, G_FF = 0, M_HEADS, 2 * M_HEADS


def _cparams(sem):
    return pltpu.CompilerParams(dimension_semantics=sem, vmem_limit_bytes=VMEM_LIMIT)


def _log_sigmoid(x):
    return jnp.minimum(x, 0.0) - jnp.log1p(jnp.exp(-jnp.abs(x)))


def _sigmoid(x):
    return 0.5 * jnp.tanh(0.5 * x) + 0.5


INPROJ_TN = 3072


def _inproj_kernel(x_ref, g_ref, w_ref, b_ref, ws_ref, wst_ref, bs_ref, bst_ref,
                   z_ref, gc_ref, gr_ref):
    x = x_ref[...]
    r = lax.rsqrt(jnp.mean(x * x, axis=-1, keepdims=True) + EPS)
    h = ((x * r) * g_ref[...]).astype(BF16)
    gc_ref[...] = jnp.dot(h, ws_ref[...], preferred_element_type=F32) + bs_ref[...]
    gr_ref[...] = lax.dot_general(wst_ref[...], h, (((1,), (1,)), ((), ())),
                                  preferred_element_type=F32) + bst_ref[...]
    for c in range(Z_COLS // INPROJ_TN):
        cols = slice(c * INPROJ_TN, (c + 1) * INPROJ_TN)
        z = jnp.dot(h, w_ref[:, cols], preferred_element_type=F32) + b_ref[:, cols]
        z_ref[:, cols] = z.astype(BF16)


def _inproj(x2, g1, w_main, b_main, w_small, w_small_t, b_small, b_small_t, tm):
    n = x2.shape[0]
    const = lambda i: (0, 0)
    return pl.pallas_call(
        _inproj_kernel,
        grid=(n // tm,),
        in_specs=[
            pl.BlockSpec((tm, D_MODEL), lambda i: (i, 0)),
            pl.BlockSpec((1, D_MODEL), const),
            pl.BlockSpec((D_MODEL, Z_COLS), const, pipeline_mode=pl.Buffered(1)),
            pl.BlockSpec((1, Z_COLS), const),
            pl.BlockSpec((D_MODEL, LANES), const),
            pl.BlockSpec((LANES, D_MODEL), const),
            pl.BlockSpec((1, LANES), const),
            pl.BlockSpec((LANES, 1), const),
        ],
        out_specs=[
            pl.BlockSpec((tm, Z_COLS), lambda i: (i, 0)),
            pl.BlockSpec((tm, LANES), lambda i: (i, 0)),
            pl.BlockSpec((LANES, tm), lambda i: (0, i)),
        ],
        out_shape=[
            jax.ShapeDtypeStruct((n, Z_COLS), BF16),
            jax.ShapeDtypeStruct((n, LANES), F32),
            jax.ShapeDtypeStruct((LANES, n), F32),
        ],
        compiler_params=_cparams(("parallel",)),
        name="inproj",
    )(x2, g1, w_main, b_main, w_small, w_small_t, b_small, b_small_t)


def _cumsum_kernel(gr_ref, ccol_ref):
    x = _log_sigmoid(gr_ref[...])
    t = x.shape[1]
    lane = lax.broadcasted_iota(I32, x.shape, 1)
    s = 1
    while s < t:
        x = x + jnp.where(lane >= s, pltpu.roll(x, s, axis=1), 0.0)
        s *= 2
    ccol_ref[...] = x.T


def _forget_cumsum(gates_row, batch, seq):
    n = batch * seq
    return pl.pallas_call(
        _cumsum_kernel,
        grid=(batch,),
        in_specs=[pl.BlockSpec((LANES, seq), lambda b: (0, b))],
        out_specs=pl.BlockSpec((seq, LANES), lambda b: (b, 0)),
        out_shape=jax.ShapeDtypeStruct((n, LANES), F32),
        compiler_params=_cparams(("parallel",)),
        name="forget_cumsum",
    )(gates_row)


def _mlstm_kernel(q_ref, k_ref, v_ref, o_ref, gc_ref, gr_ref, cw_ref, mg_ref, out_ref,
                  c_sc, n_sc, m_sc, tail_sc, qk_sc, *, chunk):
    L = chunk
    dh = M_HEAD_DIM
    t = pl.program_id(1)

    @pl.when(t == 0)
    def _():
        c_sc[...] = jnp.zeros_like(c_sc)
        n_sc[...] = jnp.zeros_like(n_sc)
        m_sc[...] = jnp.full_like(m_sc, M_INIT)
        tail_sc[...] = jnp.zeros_like(tail_sc)

    gc = gc_ref[...]
    gr = gr_ref[...]
    lsg_c = _log_sigmoid(gc)
    lsg_r = _log_sigmoid(gr[0:2 * M_HEADS, :])
    row_i = lax.broadcasted_iota(I32, (L, L), 0)
    col_i = lax.broadcasted_iota(I32, (L, L), 1)
    tril = col_i <= row_i
    shift = [(row_i - col_i == d).astype(BF16) for d in range(1, CONV_WIDTH)]
    row8 = lax.broadcasted_iota(I32, (SUBLANES, dh), 0)

    def conv_silu(x_ref, cx, c0):
        x = x_ref[:, cx:cx + dh]
        tail = tail_sc[:, c0:c0 + dh]
        acc = x.astype(F32) * cw_ref[CONV_WIDTH - 1:CONV_WIDTH, c0:c0 + dh]
        for d in range(1, CONV_WIDTH):
            xs = jnp.dot(shift[d - 1], x, preferred_element_type=F32)
            head = jnp.where(row8 < d, pltpu.roll(tail, d, axis=0), xs[0:SUBLANES, :])
            xs = jnp.concatenate([head, xs[SUBLANES:, :]], axis=0)
            acc = acc + xs * cw_ref[CONV_WIDTH - 1 - d:CONV_WIDTH - d, c0:c0 + dh]
        tail_sc[:, c0:c0 + dh] = x[L - SUBLANES:L, :].astype(F32)
        return acc * _sigmoid(acc)

    for h in range(M_HEADS):
        qk_sc[:, h * dh:(h + 1) * dh] = conv_silu(q_ref, h * dh, h * dh) * (dh ** -0.5)
        qk_sc[:, D_MODEL + h * dh:D_MODEL + (h + 1) * dh] = conv_silu(k_ref, h * dh, D_MODEL + h * dh)

    for h in range(M_HEADS):
        q = qk_sc[:, h * dh:(h + 1) * dh]
        k = qk_sc[:, D_MODEL + h * dh:D_MODEL + (h + 1) * dh]
        v = v_ref[:, h * dh:(h + 1) * dh]
        qb = q.astype(BF16)
        kb = k.astype(BF16)

        i_col = gc[:, G_MI + h:G_MI + h + 1]
        lf_col = lsg_c[:, G_MF + h:G_MF + h + 1]
        i_row = gr[G_MI + h:G_MI + h + 1, :]
        lf_row = lsg_r[G_MF + h:G_MF + h + 1, :]

        b_col = jnp.sum(jnp.where(tril, lf_row, 0.0), axis=1, keepdims=True)
        b_row = jnp.sum(jnp.where(row_i <= col_i, lf_col, 0.0), axis=0, keepdims=True)
        g = jnp.sum(lf_row, axis=1, keepdims=True)
        m_prev = m_sc[h][0:1, 0:1]

        dmat = jnp.where(tril, b_col - b_row + i_row, -jnp.inf)
        inter = b_col + m_prev
        m_t = jnp.maximum(jnp.max(dmat, axis=1, keepdims=True), inter)
        p = jnp.exp(dmat - m_t)
        s = lax.dot_general(qb, kb, (((1,), (1,)), ((), ())), preferred_element_type=F32) * p
        w_inter = jnp.exp(inter - m_t)
        c_old = c_sc[h]
        n_old = n_sc[h]
        num = (jnp.dot(s.astype(BF16), v, preferred_element_type=F32)
               + w_inter * jnp.dot(qb, c_old.astype(BF16), preferred_element_type=F32))
        den = (jnp.sum(s, axis=1, keepdims=True)
               + w_inter * jnp.sum(q * n_old, axis=1, keepdims=True))
        hh = num / jnp.maximum(jnp.abs(den), jnp.exp(-m_t))

        log_w = g - b_col + i_col
        m_new = jnp.maximum(g + m_prev, jnp.max(log_w, axis=0, keepdims=True))
        w_s = jnp.exp(log_w - m_new)
        decay = jnp.exp(g + m_prev - m_new)
        wv = (w_s * v.astype(F32)).astype(BF16)
        c_sc[h] = decay * c_old + lax.dot_general(kb, wv, (((0,), (0,)), ((), ())),
                                                  preferred_element_type=F32)
        n_sc[h] = decay * n_old + jnp.sum(w_s * k, axis=0, keepdims=True)
        m_sc[h] = jnp.broadcast_to(m_new, (SUBLANES, LANES))

        r = lax.rsqrt(jnp.mean(hh * hh, axis=-1, keepdims=True) + EPS)
        hn = (hh * r) * mg_ref[0:1, h * dh:(h + 1) * dh]
        og = _sigmoid(o_ref[:, h * dh:(h + 1) * dh].astype(F32))
        out_ref[:, h * dh:(h + 1) * dh] = (hn * og).astype(BF16)


def _mlstm(z, gates_col, gates_row, conv_w, m_norm_g, batch, seq, chunk):
    n = batch * seq
    nt = seq // chunk
    zspec = lambda cb: pl.BlockSpec((chunk, D_MODEL), lambda b, t, cb=cb: (b * nt + t, cb))
    return pl.pallas_call(
        functools.partial(_mlstm_kernel, chunk=chunk),
        grid=(batch, nt),
        in_specs=[
            zspec(ZB_MQ), zspec(ZB_MK), zspec(ZB_MV), zspec(ZB_MO),
            pl.BlockSpec((chunk, LANES), lambda b, t: (b * nt + t, 0)),
            pl.BlockSpec((LANES, chunk), lambda b, t: (0, b * nt + t)),
            pl.BlockSpec((CONV_WIDTH, 2 * D_MODEL), lambda b, t: (0, 0)),
            pl.BlockSpec((1, D_MODEL), lambda b, t: (0, 0)),
        ],
        out_specs=pl.BlockSpec((chunk, D_MODEL), lambda b, t: (b * nt + t, 0)),
        out_shape=jax.ShapeDtypeStruct((n, D_MODEL), BF16),
        scratch_shapes=[
            pltpu.VMEM((M_HEADS, M_HEAD_DIM, M_HEAD_DIM), F32),
            pltpu.VMEM((M_HEADS, 1, M_HEAD_DIM), F32),
            pltpu.VMEM((M_HEADS, SUBLANES, LANES), F32),
            pltpu.VMEM((SUBLANES, 2 * D_MODEL), F32),
            pltpu.VMEM((chunk, 2 * D_MODEL), F32),
        ],
        compiler_params=_cparams(("parallel", "arbitrary")),
        name="mlstm",
    )(z, z, z, z, gates_col, gates_row, conv_w, m_norm_g)


LOG2E = 1.4426950408889634
FOX_SAFE_LOG2_RANGE = 100.0


def _split3(c):
    hi = c.astype(BF16).astype(F32)
    r1 = c - hi
    mid = r1.astype(BF16).astype(F32)
    lo = (r1 - mid).astype(BF16).astype(F32)
    return hi, mid, lo


def _fox_aug(c, lane, is_query, shift):
    hi, mid, lo = _split3(c if is_query else -c)
    off = 0 if is_query else 3
    ex = jnp.where(lane == off, hi, jnp.where(lane == off + 1, mid, jnp.where(lane == off + 2, lo, 0.0)))
    ones_at = (lane >= 3) & (lane < 6) if is_query else lane < 3
    ex = jnp.where(ones_at, 1.0, ex)
    return jnp.where(lane == 6, -shift if is_query else 1.0, ex).astype(BF16)


def _fox_kernel(bnd_ref, q_ref, k_ref, v_ref, ccol_ref, qg_ref, kg_ref, out_ref,
                ka_sc, va_sc, m_sc, acc_sc, s_sc, *, tq, seq, hp):
    g = pl.program_id(1)
    i = pl.program_id(2)
    dh = F_HEAD_DIM
    lane = lax.broadcasted_iota(I32, (tq, LANES), 1)
    shift = bnd_ref[0]
    safe = bnd_ref[1] > 0.5

    def c_of(r0, hh):
        cc = ccol_ref[pl.ds(r0, tq), :]
        return jnp.sum(jnp.where(lane == G_FF + g * hp + hh, cc, 0.0), axis=1, keepdims=True) * LOG2E

    @pl.when(i == 0)
    def _():
        def prep_chunk(c, carry):
            r0 = pl.multiple_of(c * tq, tq)
            for hh in range(hp):
                kk = k_ref[pl.ds(r0, tq), hh * dh:(hh + 1) * dh].astype(F32)
                r = lax.rsqrt(jnp.mean(kk * kk, axis=-1, keepdims=True) + EPS)
                ka_sc[hh, pl.ds(r0, tq), 0:dh] = ((kk * r) * kg_ref[...]).astype(BF16)
                ka_sc[hh, pl.ds(r0, tq), dh:2 * dh] = _fox_aug(c_of(r0, hh), lane, False, shift)
                va_sc[hh, pl.ds(r0, tq), 0:dh] = v_ref[pl.ds(r0, tq), hh * dh:(hh + 1) * dh]
                va_sc[hh, pl.ds(r0, tq), dh:2 * dh] = jnp.ones((tq, dh), BF16)
            return carry
        lax.fori_loop(0, seq // tq, prep_chunk, 0)

    q0 = pl.multiple_of(i * tq, tq)
    qa = []
    for hh in range(hp):
        q = q_ref[:, hh * dh:(hh + 1) * dh].astype(F32)
        r = lax.rsqrt(jnp.mean(q * q, axis=-1, keepdims=True) + EPS)
        qn = (((q * r) * qg_ref[...]) * (dh ** -0.5 * LOG2E)).astype(BF16)
        qa.append(jnp.concatenate([qn, _fox_aug(c_of(q0, hh), lane, True, shift)], axis=1))

    acc_sc[...] = jnp.zeros_like(acc_sc)

    def scores(j, masked):
        r0 = pl.multiple_of(j * tq, tq)
        s = [lax.dot_general(qa[hh], ka_sc[hh, pl.ds(r0, tq), :], (((1,), (1,)), ((), ())),
                             preferred_element_type=F32) for hh in range(hp)]
        if masked:
            rr = lax.broadcasted_iota(I32, (tq, tq), 0)
            cc = lax.broadcasted_iota(I32, (tq, tq), 1)
            s = [jnp.where(rr >= cc, sh, -jnp.inf) for sh in s]
        return r0, s

    def fixed_consume(j, s):
        r0 = pl.multiple_of(j * tq, tq)
        for hh in range(hp):
            p = jnp.exp2(s[hh]).astype(BF16)
            acc_sc[hh] += jnp.dot(p, va_sc[hh, pl.ds(r0, tq), :], preferred_element_type=F32)

    def run_fixed():
        def produce(buf, j, masked):
            _, s = scores(j, masked)
            for hh in range(hp):
                s_sc[buf, hh] = s[hh]

        def consume(buf, j):
            fixed_consume(j, [s_sc[buf, hh] for hh in range(hp)])

        def pair(j0):
            produce(1, j0, False)
            consume(0, jnp.where(j0 == 0, i, j0 - 1))
            produce(0, j0 + 1, False)
            consume(1, j0)

        produce(0, i, True)
        quads = i // 4
        pairs = i // 2

        def quad_body(k, carry):
            pair(4 * k)
            pair(4 * k + 2)
            return carry
        lax.fori_loop(0, quads, quad_body, 0)

        def pair_body(k, carry):
            pair(2 * k)
            return carry
        lax.fori_loop(2 * quads, pairs, pair_body, 0)
        pending = jnp.where(pairs == 0, i, 2 * pairs - 1)

        @pl.when(i % 2 == 1)
        def _():
            produce(1, i - 1, False)
            consume(0, pending)
            consume(1, i - 1)

        @pl.when(i % 2 == 0)
        def _():
            consume(0, pending)

    def running_step(j, masked):
        r0, s = scores(j, masked)
        for hh in range(hp):
            m_old = m_sc[hh]
            m_new = jnp.maximum(m_old, jnp.max(s[hh], axis=1, keepdims=True))
            p = jnp.exp2(s[hh] - m_new).astype(BF16)
            alpha = jnp.exp2(m_old - m_new)
            acc_sc[hh] = alpha * acc_sc[hh] + jnp.dot(p, va_sc[hh, pl.ds(r0, tq), :],
                                                      preferred_element_type=F32)
            m_sc[hh] = m_new

    def run(step):
        def body(j, carry):
            step(j, False)
            return carry
        lax.fori_loop(0, i, body, 0)
        step(i, True)

    @pl.when(safe)
    def _():
        run_fixed()

    @pl.when(jnp.logical_not(safe))
    def _():
        m_sc[...] = jnp.full_like(m_sc, -jnp.inf)
        run(running_step)

    for hh in range(hp):
        out_ref[:, hh * dh:(hh + 1) * dh] = (acc_sc[hh, :, 0:dh] / acc_sc[hh, :, dh:2 * dh]).astype(BF16)


def _fox(z, ccol, qn_g, kn_g, batch, seq, tq, hp):
    n = batch * seq
    nq = seq // tq
    w = hp * F_HEAD_DIM
    fq0 = ZB_FQ * (D_MODEL // w)
    fk0 = ZB_FK * (D_MODEL // w)
    fv0 = ZB_FV * (D_MODEL // w)
    shift = jnp.ceil(1.01 * LOG2E * F_HEAD_DIM ** 0.5 * jnp.max(jnp.abs(qn_g)) * jnp.max(jnp.abs(kn_g)))
    safe = (2.0 * shift <= FOX_SAFE_LOG2_RANGE).astype(F32)
    bnd = jnp.stack([shift, safe]).astype(F32)
    return pl.pallas_call(
        functools.partial(_fox_kernel, tq=tq, seq=seq, hp=hp),
        grid=(batch, F_HEADS // hp, nq),
        in_specs=[
            pl.BlockSpec(memory_space=pltpu.SMEM),
            pl.BlockSpec((tq, w), lambda b, g, i: (b * nq + i, fq0 + g)),
            pl.BlockSpec((seq, w), lambda b, g, i: (b, fk0 + g)),
            pl.BlockSpec((seq, w), lambda b, g, i: (b, fv0 + g)),
            pl.BlockSpec((seq, LANES), lambda b, g, i: (b, 0)),
            pl.BlockSpec((1, F_HEAD_DIM), lambda b, g, i: (0, 0)),
            pl.BlockSpec((1, F_HEAD_DIM), lambda b, g, i: (0, 0)),
        ],
        out_specs=pl.BlockSpec((tq, w), lambda b, g, i: (b * nq + i, g)),
        out_shape=jax.ShapeDtypeStruct((n, D_MODEL), BF16),
        scratch_shapes=[
            pltpu.VMEM((hp, seq, 2 * F_HEAD_DIM), BF16),
            pltpu.VMEM((hp, seq, 2 * F_HEAD_DIM), BF16),
            pltpu.VMEM((hp, tq, 1), F32),
            pltpu.VMEM((hp, tq, 2 * F_HEAD_DIM), F32),
            pltpu.VMEM((2, hp, tq, tq), F32),
        ],
        compiler_params=_cparams(("parallel", "parallel", "arbitrary")),
        name="fox",
    )(bnd, z, z, z, ccol, qn_g, kn_g)


def _topk_rows(s, k, store):
    rows = lax.broadcasted_iota(I32, s.shape, 0).astype(F32)
    n_rows = float(s.shape[0])
    for r in range(k):
        mx = jnp.max(s, axis=0, keepdims=True)
        ix = jnp.min(jnp.where(s == mx, rows, n_rows), axis=0, keepdims=True)
        s = jnp.where(rows == ix, -jnp.inf, s)
        store(r, mx, ix.astype(I32))


def _pair_candidates(v0, v1):
    k = P_TOPK
    w = v0.shape[1]
    row8 = lax.broadcasted_iota(I32, (SUBLANES, w), 0)
    sums = [v0[0:1, :] + v1]
    for a in range(1, SUBLANES):
        nb = k // (a + 1)
        sums.append(jnp.where(row8 < nb, v0[a:a + 1, :] + v1[0:SUBLANES, :], -jnp.inf))
    sums.append(v0[SUBLANES:k, :] + v1[0:1, :])
    return jnp.concatenate(sums, axis=0)


def _pair_of_row(row):
    k = P_TOPK
    q = row - k
    mid_a = 1 + (q >> 3)
    mid_b = q & (SUBLANES - 1)
    last = row >= k + SUBLANES * (SUBLANES - 1)
    a = jnp.where(row < k, 0, jnp.where(last, row - (k + SUBLANES * (SUBLANES - 2)), mid_a))
    b = jnp.where(row < k, row, jnp.where(last, 0, mid_b))
    return a, b


def _merge_kernel(hm_ref, hf_ref, gm_ref, gf_ref, x_ref, wm_ref, wf_ref, wo_ref, g2_ref,
                  wpq_ref, sk_ref, x1_ref, xn_ref, ids_i_ref, ids_j_ref, gate_ref,
                  sc_sc, val_sc, idx_sc, top_sc, pos_sc, oi_sc, oj_sc, og_sc, *, tm, tc):
    ym = jnp.dot(hm_ref[...], wm_ref[...], preferred_element_type=F32)
    yf = jnp.dot(hf_ref[...], wf_ref[...], preferred_element_type=F32)
    y = (_sigmoid(gm_ref[...].astype(F32)) * ym + _sigmoid(gf_ref[...].astype(F32)) * yf)
    x1 = x_ref[...] + jnp.dot(y.astype(BF16), wo_ref[...], preferred_element_type=F32)
    x1_ref[...] = x1
    r = lax.rsqrt(jnp.mean(x1 * x1, axis=-1, keepdims=True) + EPS)
    xn = ((x1 * r) * g2_ref[...]).astype(BF16)
    xn_ref[...] = xn
    qh = jnp.dot(xn, wpq_ref[...], preferred_element_type=F32).astype(BF16)

    for hc in range(2 * P_HEADS):
        sc_sc[hc] = lax.dot_general(sk_ref[hc], qh[:, hc * P_HALF:(hc + 1) * P_HALF],
                                    (((1,), (1,)), ((), ())), preferred_element_type=F32)

    def chunk_body(c, carry):
        c0 = pl.multiple_of(c * tc, tc)

        def first_level(hc, carry2):
            s = sc_sc[hc, :, pl.ds(c0, tc)]

            def store(r, val, idx):
                val_sc[hc, r:r + 1, :] = val
                idx_sc[hc, r:r + 1, :] = idx
            _topk_rows(s, P_TOPK, store)
            return carry2
        lax.fori_loop(0, 2 * P_HEADS, first_level, 0)

        def second_level(hd, carry2):
            v0 = val_sc[2 * hd]
            v1 = val_sc[2 * hd + 1]
            i0 = idx_sc[2 * hd]
            i1 = idx_sc[2 * hd + 1]
            cand = _pair_candidates(v0, v1)

            def store(r, val, idx):
                top_sc[r:r + 1, :] = val
                pos_sc[r:r + 1, :] = idx
            _topk_rows(cand, P_TOPK, store)
            top = top_sc[...]
            pa, pb = _pair_of_row(pos_sc[...])
            sel_i = jnp.zeros(pa.shape, I32)
            sel_j = jnp.zeros(pa.shape, I32)
            for a in range(P_TOPK):
                sel_i = jnp.where(pa == a, i0[a:a + 1, :], sel_i)
                sel_j = jnp.where(pb == a, i1[a:a + 1, :], sel_j)
            e = jnp.exp(top - top[0:1, :])
            gate = e / jnp.sum(e, axis=0, keepdims=True)
            r0 = pl.multiple_of(hd * P_TOPK, P_TOPK)
            oi_sc[pl.ds(r0, P_TOPK), :] = sel_i
            oj_sc[pl.ds(r0, P_TOPK), :] = sel_j
            og_sc[pl.ds(r0, P_TOPK), :] = gate
            return carry2
        lax.fori_loop(0, P_HEADS, second_level, 0)

        ids_i_ref[pl.ds(c0, tc), :] = oi_sc[...].T
        ids_j_ref[pl.ds(c0, tc), :] = oj_sc[...].T
        gate_ref[pl.ds(c0, tc), :] = og_sc[...].T
        return carry
    lax.fori_loop(0, tm // tc, chunk_body, 0)


def _merge(hm, hf, z, x2, w_m, w_f, w_o, g2, w_pq, sk, tm, tc):
    n = x2.shape[0]
    hk = P_HEADS * P_TOPK
    row = lambda i: (i, 0)
    const = lambda i: (0, 0)
    return pl.pallas_call(
        functools.partial(_merge_kernel, tm=tm, tc=tc),
        grid=(n // tm,),
        in_specs=[
            pl.BlockSpec((tm, D_MODEL), row),
            pl.BlockSpec((tm, D_MODEL), row),
            pl.BlockSpec((tm, D_MODEL), lambda i: (i, ZB_GM)),
            pl.BlockSpec((tm, D_MODEL), lambda i: (i, ZB_GF)),
            pl.BlockSpec((tm, D_MODEL), row),
            pl.BlockSpec((D_MODEL, D_MODEL), const),
            pl.BlockSpec((D_MODEL, D_MODEL), const),
            pl.BlockSpec((D_MODEL, D_MODEL), const),
            pl.BlockSpec((1, D_MODEL), const),
            pl.BlockSpec((D_MODEL, 2 * P_HEADS * P_HALF), const),
            pl.BlockSpec((2 * P_HEADS, P_KEYS, P_HALF), lambda i: (0, 0, 0)),
        ],
        out_specs=[
            pl.BlockSpec((tm, D_MODEL), row),
            pl.BlockSpec((tm, D_MODEL), row),
            pl.BlockSpec((tm, hk), row),
            pl.BlockSpec((tm, hk), row),
            pl.BlockSpec((tm, hk), row),
        ],
        out_shape=[
            jax.ShapeDtypeStruct((n, D_MODEL), F32),
            jax.ShapeDtypeStruct((n, D_MODEL), BF16),
            jax.ShapeDtypeStruct((n, hk), I32),
            jax.ShapeDtypeStruct((n, hk), I32),
            jax.ShapeDtypeStruct((n, hk), F32),
        ],
        scratch_shapes=[
            pltpu.VMEM((2 * P_HEADS, P_KEYS, tm), F32),
            pltpu.VMEM((2 * P_HEADS, P_TOPK, tc), F32),
            pltpu.VMEM((2 * P_HEADS, P_TOPK, tc), I32),
            pltpu.VMEM((P_TOPK, tc), F32),
            pltpu.VMEM((P_TOPK, tc), I32),
            pltpu.VMEM((hk, tc), I32),
            pltpu.VMEM((hk, tc), I32),
            pltpu.VMEM((hk, tc), F32),
        ],
        compiler_params=_cparams(("parallel",)),
        name="merge_retrieve",
    )(hm, hf, z, z, x2, w_m, w_f, w_o, g2, w_pq, sk)


U_SUB = 512


def _peer_u_kernel(xn_ref, u_ref, ii_ref, jj_ref, gate_ref, hw_ref, ids_ref, acc_sc, *, eb):
    e = pl.program_id(1)
    ne = pl.num_programs(1)

    @pl.when(e == 0)
    def _():
        acc_sc[...] = jnp.zeros_like(acc_sc)

    xn = xn_ref[...]
    ii = ii_ref[...]
    jj = jj_ref[...]
    acc = acc_sc[...]
    for sb in range(eb // U_SUB):
        a = lax.dot_general(xn, u_ref[sb * U_SUB:(sb + 1) * U_SUB, :], (((1,), (1,)), ((), ())),
                            preferred_element_type=F32)
        for c in range(U_SUB // P_KEYS):
            blk = e * (eb // P_KEYS) + sb * (U_SUB // P_KEYS) + c
            g = jnp.take_along_axis(a[:, c * P_KEYS:(c + 1) * P_KEYS], jj, axis=1,
                                    mode="promise_in_bounds")
            acc = jnp.where(ii == blk, g, acc)
    acc_sc[...] = acc

    @pl.when(e == ne - 1)
    def _():
        av = acc_sc[...]
        gelu = 0.5 * av * (1.0 + lax.erf(av * (2.0 ** -0.5)))
        hw_ref[...] = gate_ref[...] * gelu
        ids_ref[...] = (ii * P_KEYS + jj) * (D_MODEL // (2 * LANES))


def _peer_u(xn, u_bf, ids_i, ids_j, gate, tm, eb):
    n = xn.shape[0]
    hk = P_HEADS * P_TOPK
    row = lambda i, e: (i, 0)
    return pl.pallas_call(
        functools.partial(_peer_u_kernel, eb=eb),
        grid=(n // tm, N_EXPERTS // eb),
        in_specs=[
            pl.BlockSpec((tm, D_MODEL), row),
            pl.BlockSpec((eb, D_MODEL), lambda i, e: (e, 0)),
            pl.BlockSpec((tm, hk), row),
            pl.BlockSpec((tm, hk), row),
            pl.BlockSpec((tm, hk), row),
        ],
        out_specs=[pl.BlockSpec((tm, hk), row), pl.BlockSpec((tm, hk), row)],
        out_shape=[jax.ShapeDtypeStruct((n, hk), F32), jax.ShapeDtypeStruct((n, hk), I32)],
        scratch_shapes=[pltpu.VMEM((tm, hk), F32)],
        compiler_params=_cparams(("parallel", "arbitrary")),
        name="peer_u",
    )(xn, u_bf, ids_i, ids_j, gate)


V_ROW_WORDS = D_MODEL // (2 * LANES)


V_SLABS = 128
V_GROUP = 16
V_ID_SPLIT = 4


def _peer_v_kernel(*refs, tt):
    id_refs = refs[:V_ID_SPLIT]
    w_ref, x1_ref, tab_ref, out_ref, slab_sc = refs[V_ID_SPLIT:]
    hk = P_HEADS * P_TOPK
    per = hk // V_ID_SPLIT
    tg = V_GROUP
    plane = tg + SUBLANES
    half = D_MODEL // 2

    def group(g, carry):
        t0 = pl.multiple_of(g * tg, tg)
        wg = w_ref[pl.ds(t0, tg), :]
        acc_lo = [jnp.zeros((tg, LANES), F32) for _ in range(V_ROW_WORDS)]
        acc_hi = [jnp.zeros((tg, LANES), F32) for _ in range(V_ROW_WORDS)]
        for n_slot in range(hk):
            q, k = divmod(n_slot, V_ID_SPLIT)
            m = k * per + q
            slab = slab_sc.at[n_slot % V_SLABS]
            for t in range(tg):
                r0 = pl.multiple_of(id_refs[k][t0 * per + (t * per + q)], V_ROW_WORDS)
                slab[pl.ds(t, V_ROW_WORDS, stride=plane), :] = tab_ref[pl.ds(r0, V_ROW_WORDS), :]
            wcol = wg[:, m:m + 1]
            for s in range(V_ROW_WORDS):
                words = slab[s * plane:s * plane + tg, :]
                acc_lo[s] = acc_lo[s] + wcol * lax.bitcast_convert_type(words << 16, F32)
                acc_hi[s] = acc_hi[s] + wcol * lax.bitcast_convert_type(words & jnp.int32(-65536), F32)
        for s in range(V_ROW_WORDS):
            lo = slice(s * LANES, (s + 1) * LANES)
            hi = slice(half + s * LANES, half + (s + 1) * LANES)
            out_ref[pl.ds(t0, tg), lo] = x1_ref[pl.ds(t0, tg), lo] + acc_lo[s]
            out_ref[pl.ds(t0, tg), hi] = x1_ref[pl.ds(t0, tg), hi] + acc_hi[s]
        return carry
    lax.fori_loop(0, tt // tg, group, 0)


def _peer_v(ids, hw, x1, v_packed, tt):
    n = ids.shape[0]
    hk = P_HEADS * P_TOPK
    per = hk // V_ID_SPLIT
    id_parts = [ids[:, k * per:(k + 1) * per].reshape(n * per) for k in range(V_ID_SPLIT)]
    return pl.pallas_call(
        functools.partial(_peer_v_kernel, tt=tt),
        grid=(n // tt,),
        in_specs=[pl.BlockSpec((tt * per,), lambda i: (i,), memory_space=pltpu.SMEM)] * V_ID_SPLIT + [
            pl.BlockSpec((tt, hk), lambda i: (i, 0)),
            pl.BlockSpec((tt, D_MODEL), lambda i: (i, 0)),
            pl.BlockSpec((N_EXPERTS * V_ROW_WORDS, LANES), lambda i: (0, 0),
                         pipeline_mode=pl.Buffered(1)),
        ],
        out_specs=pl.BlockSpec((tt, D_MODEL), lambda i: (i, 0)),
        out_shape=jax.ShapeDtypeStruct((n, D_MODEL), F32),
        scratch_shapes=[pltpu.VMEM((V_SLABS, V_ROW_WORDS * (V_GROUP + SUBLANES), LANES), I32)],
        compiler_params=_cparams(("arbitrary",)),
        name="peer_v",
    )(*id_parts, hw, x1, v_packed)


def _pack_v(v_tab):
    vb = lax.bitcast_convert_type(v_tab.astype(BF16), jnp.uint16).astype(jnp.uint32)
    half = D_MODEL // 2
    packed = vb[:, :half] | (vb[:, half:] << 16)
    return lax.bitcast_convert_type(packed, I32).reshape(N_EXPERTS * V_ROW_WORDS, LANES)


def _layer(x2, batch, seq, norm1_g, w_in, b_in, conv_w, m_norm_g, qn_g, kn_g, w_m_out, w_f_out,
           w_out, norm2_g, w_pq, sub_keys, u_tab, v_tab):
    o = _OFFS
    cols = lambda a: jnp.concatenate([a[..., o[0]:o[4]], a[..., o[6]:o[9]], a[..., o[10]:o[12]]], axis=-1)
    small = lambda a: jnp.concatenate([a[..., o[4]:o[6]], a[..., o[9]:o[10]]], axis=-1)
    w_main = cols(w_in).astype(BF16)
    b_main = cols(b_in)[None, :]
    ws = small(w_in)
    bs = small(b_in)
    pad = LANES - ws.shape[1]
    ws = jnp.pad(ws, ((0, 0), (0, pad))).astype(BF16)
    bs = jnp.pad(bs, (0, pad))

    tm_a = min(512, x2.shape[0])
    z, gates_col, gates_row = _inproj(x2, norm1_g[None, :], w_main, b_main, ws, ws.T,
                                      bs[None, :], bs[:, None], tm_a)
    ccol = _forget_cumsum(gates_row, batch, seq)
    chunk = min(256, seq)
    hm = _mlstm(z, gates_col, gates_row, conv_w, m_norm_g[None, :], batch, seq, chunk)
    tq = min(512, seq)
    hf = _fox(z, ccol, qn_g[None, :], kn_g[None, :], batch, seq, tq, 2)

    sk = sub_keys.reshape(2 * P_HEADS, P_KEYS, P_HALF).astype(BF16)
    tm_d = min(512, x2.shape[0])
    x1, xn, ids_i, ids_j, gate = _merge(hm, hf, z, x2, w_m_out.astype(BF16), w_f_out.astype(BF16),
                                        w_out.astype(BF16), norm2_g[None, :], w_pq.astype(BF16), sk,
                                        tm_d, 512)
    tm_e = min(1024, x2.shape[0])
    hw, ids = _peer_u(xn, u_tab.astype(BF16), ids_i, ids_j, gate, tm_e, 4096)
    return _peer_v(ids, hw, x1, _pack_v(v_tab), min(512, x1.shape[0]))


def kernel(x, norm1_g, w_in, b_in, conv_w, m_norm_g, qn_g, kn_g, w_m_out, w_f_out, w_out,
           norm2_g, w_pq, sub_keys, u_tab, v_tab):
    batch, seq, d = x.shape
    x2 = x.reshape(batch * seq, d)
    for l in range(w_in.shape[0]):
        x2 = _layer(x2, batch, seq, norm1_g[l], w_in[l], b_in[l], conv_w[l], m_norm_g[l], qn_g[l],
                    kn_g[l], w_m_out[l], w_f_out[l], w_out[l], norm2_g[l], w_pq[l], sub_keys[l],
                    u_tab[l], v_tab[l])
    return x2.reshape(batch, seq, d)
```

```python
import functools

import jax
import jax.numpy as jnp
from jax import lax
from jax.experimental import pallas as pl
from jax.experimental.pallas import tpu as pltpu

F32 = jnp.float32
BF16 = jnp.bfloat16
I32 = jnp.int32

D_MODEL = 1024
M_HEADS = 4
M_HEAD_DIM = 256
F_HEADS = 8
F_HEAD_DIM = 128
CONV_WIDTH = 4
P_HEADS = 8
P_KEYS = 128
P_TOPK = 16
P_HALF = 128
N_EXPERTS = P_KEYS * P_KEYS
EPS = 1e-6
M_INIT = -1e30
LANES = 128
SUBLANES = 8
VMEM_LIMIT = 56 * 1024 * 1024

_SIZES = (1024, 1024, 1024, 1024, M_HEADS, M_HEADS, 1024, 1024, 1024, F_HEADS, 1024, 1024)
_OFFS = [0]
for _s in _SIZES:
    _OFFS.append(_OFFS[-1] + _s)
Z_COLS = 9 * 1024
ZB_MQ, ZB_MK, ZB_MV, ZB_MO, ZB_FQ, ZB_FK, ZB_FV, ZB_GM, ZB_GF = range(9)
G_MI, G_MF, G_FF = 0, M_HEADS, 2 * M_HEADS


def _cparams(sem):
    return pltpu.CompilerParams(dimension_semantics=sem, vmem_limit_bytes=VMEM_LIMIT)


def _log_sigmoid(x):
    return jnp.minimum(x, 0.0) - jnp.log1p(jnp.exp(-jnp.abs(x)))


def _sigmoid(x):
    return 0.5 * jnp.tanh(0.5 * x) + 0.5


INPROJ_TN = 3072


def _inproj_kernel(x_ref, g_ref, w_ref, b_ref, ws_ref, wst_ref, bs_ref, bst_ref,
                   z_ref, gc_ref, gr_ref):
    x = x_ref[...]
    r = lax.rsqrt(jnp.mean(x * x, axis=-1, keepdims=True) + EPS)
    h = ((x * r) * g_ref[...]).astype(BF16)
    gc_ref[...] = jnp.dot(h, ws_ref[...], preferred_element_type=F32) + bs_ref[...]
    gr_ref[...] = lax.dot_general(wst_ref[...], h, (((1,), (1,)), ((), ())),
                                  preferred_element_type=F32) + bst_ref[...]
    for c in range(Z_COLS // INPROJ_TN):
        cols = slice(c * INPROJ_TN, (c + 1) * INPROJ_TN)
        z = jnp.dot(h, w_ref[:, cols], preferred_element_type=F32) + b_ref[:, cols]
        z_ref[:, cols] = z.astype(BF16)


def _inproj(x2, g1, w_main, b_main, w_small, w_small_t, b_small, b_small_t, tm):
    n = x2.shape[0]
    const = lambda i: (0, 0)
    return pl.pallas_call(
        _inproj_kernel,
        grid=(n // tm,),
        in_specs=[
            pl.BlockSpec((tm, D_MODEL), lambda i: (i, 0)),
            pl.BlockSpec((1, D_MODEL), const),
            pl.BlockSpec((D_MODEL, Z_COLS), const, pipeline_mode=pl.Buffered(1)),
            pl.BlockSpec((1, Z_COLS), const),
            pl.BlockSpec((D_MODEL, LANES), const),
            pl.BlockSpec((LANES, D_MODEL), const),
            pl.BlockSpec((1, LANES), const),
            pl.BlockSpec((LANES, 1), const),
        ],
        out_specs=[
            pl.BlockSpec((tm, Z_COLS), lambda i: (i, 0)),
            pl.BlockSpec((tm, LANES), lambda i: (i, 0)),
            pl.BlockSpec((LANES, tm), lambda i: (0, i)),
        ],
        out_shape=[
            jax.ShapeDtypeStruct((n, Z_COLS), BF16),
            jax.ShapeDtypeStruct((n, LANES), F32),
            jax.ShapeDtypeStruct((LANES, n), F32),
        ],
        compiler_params=_cparams(("parallel",)),
        name="inproj",
    )(x2, g1, w_main, b_main, w_small, w_small_t, b_small, b_small_t)


def _cumsum_kernel(gr_ref, ccol_ref):
    x = _log_sigmoid(gr_ref[...])
    t = x.shape[1]
    lane = lax.broadcasted_iota(I32, x.shape, 1)
    s = 1
    while s < t:
        x = x + jnp.where(lane >= s, pltpu.roll(x, s, axis=1), 0.0)
        s *= 2
    ccol_ref[...] = x.T


def _forget_cumsum(gates_row, batch, seq):
    n = batch * seq
    return pl.pallas_call(
        _cumsum_kernel,
        grid=(batch,),
        in_specs=[pl.BlockSpec((LANES, seq), lambda b: (0, b))],
        out_specs=pl.BlockSpec((seq, LANES), lambda b: (b, 0)),
        out_shape=jax.ShapeDtypeStruct((n, LANES), F32),
        compiler_params=_cparams(("parallel",)),
        name="forget_cumsum",
    )(gates_row)


def _mlstm_kernel(q_ref, k_ref, v_ref, o_ref, gc_ref, gr_ref, cw_ref, mg_ref, out_ref,
                  c_sc, n_sc, m_sc, tail_sc, qk_sc, *, chunk):
    L = chunk
    dh = M_HEAD_DIM
    t = pl.program_id(1)

    @pl.when(t == 0)
    def _():
        c_sc[...] = jnp.zeros_like(c_sc)
        n_sc[...] = jnp.zeros_like(n_sc)
        m_sc[...] = jnp.full_like(m_sc, M_INIT)
        tail_sc[...] = jnp.zeros_like(tail_sc)

    gc = gc_ref[...]
    gr = gr_ref[...]
    lsg_c = _log_sigmoid(gc)
    lsg_r = _log_sigmoid(gr[0:2 * M_HEADS, :])
    row_i = lax.broadcasted_iota(I32, (L, L), 0)
    col_i = lax.broadcasted_iota(I32, (L, L), 1)
    tril = col_i <= row_i
    shift = [(row_i - col_i == d).astype(BF16) for d in range(1, CONV_WIDTH)]
    row8 = lax.broadcasted_iota(I32, (SUBLANES, dh), 0)

    def conv_silu(x_ref, cx, c0):
        x = x_ref[:, cx:cx + dh]
        tail = tail_sc[:, c0:c0 + dh]
        acc = x.astype(F32) * cw_ref[CONV_WIDTH - 1:CONV_WIDTH, c0:c0 + dh]
        for d in range(1, CONV_WIDTH):
            xs = jnp.dot(shift[d - 1], x, preferred_element_type=F32)
            head = jnp.where(row8 < d, pltpu.roll(tail, d, axis=0), xs[0:SUBLANES, :])
            xs = jnp.concatenate([head, xs[SUBLANES:, :]], axis=0)
            acc = acc + xs * cw_ref[CONV_WIDTH - 1 - d:CONV_WIDTH - d, c0:c0 + dh]
        tail_sc[:, c0:c0 + dh] = x[L - SUBLANES:L, :].astype(F32)
        return acc * _sigmoid(acc)

    for h in range(M_HEADS):
        qk_sc[:, h * dh:(h + 1) * dh] = conv_silu(q_ref, h * dh, h * dh) * (dh ** -0.5)
        qk_sc[:, D_MODEL + h * dh:D_MODEL + (h + 1) * dh] = conv_silu(k_ref, h * dh, D_MODEL + h * dh)

    for h in range(M_HEADS):
        q = qk_sc[:, h * dh:(h + 1) * dh]
        k = qk_sc[:, D_MODEL + h * dh:D_MODEL + (h + 1) * dh]
        v = v_ref[:, h * dh:(h + 1) * dh]
        qb = q.astype(BF16)
        kb = k.astype(BF16)

        i_col = gc[:, G_MI + h:G_MI + h + 1]
        lf_col = lsg_c[:, G_MF + h:G_MF + h + 1]
        i_row = gr[G_MI + h:G_MI + h + 1, :]
        lf_row = lsg_r[G_MF + h:G_MF + h + 1, :]

        b_col = jnp.sum(jnp.where(tril, lf_row, 0.0), axis=1, keepdims=True)
        b_row = jnp.sum(jnp.where(row_i <= col_i, lf_col, 0.0), axis=0, keepdims=True)
        g = jnp.sum(lf_row, axis=1, keepdims=True)
        m_prev = m_sc[h][0:1, 0:1]

        dmat = jnp.where(tril, b_col - b_row + i_row, -jnp.inf)
        inter = b_col + m_prev
        m_t = jnp.maximum(jnp.max(dmat, axis=1, keepdims=True), inter)
        p = jnp.exp(dmat - m_t)
        s = lax.dot_general(qb, kb, (((1,), (1,)), ((), ())), preferred_element_type=F32) * p
        w_inter = jnp.exp(inter - m_t)
        c_old = c_sc[h]
        n_old = n_sc[h]
        num = (jnp.dot(s.astype(BF16), v, preferred_element_type=F32)
               + w_inter * jnp.dot(qb, c_old.astype(BF16), preferred_element_type=F32))
        den = (jnp.sum(s, axis=1, keepdims=True)
               + w_inter * jnp.sum(q * n_old, axis=1, keepdims=True))
        hh = num / jnp.maximum(jnp.abs(den), jnp.exp(-m_t))

        log_w = g - b_col + i_col
        m_new = jnp.maximum(g + m_prev, jnp.max(log_w, axis=0, keepdims=True))
        w_s = jnp.exp(log_w - m_new)
        decay = jnp.exp(g + m_prev - m_new)
        wv = (w_s * v.astype(F32)).astype(BF16)
        c_sc[h] = decay * c_old + lax.dot_general(kb, wv, (((0,), (0,)), ((), ())),
                                                  preferred_element_type=F32)
        n_sc[h] = decay * n_old + jnp.sum(w_s * k, axis=0, keepdims=True)
        m_sc[h] = jnp.broadcast_to(m_new, (SUBLANES, LANES))

        r = lax.rsqrt(jnp.mean(hh * hh, axis=-1, keepdims=True) + EPS)
        hn = (hh * r) * mg_ref[0:1, h * dh:(h + 1) * dh]
        og = _sigmoid(o_ref[:, h * dh:(h + 1) * dh].astype(F32))
        out_ref[:, h * dh:(h + 1) * dh] = (hn * og).astype(BF16)


def _mlstm(z, gates_col, gates_row, conv_w, m_norm_g, batch, seq, chunk):
    n = batch * seq
    nt = seq // chunk
    zspec = lambda cb: pl.BlockSpec((chunk, D_MODEL), lambda b, t, cb=cb: (b * nt + t, cb))
    return pl.pallas_call(
        functools.partial(_mlstm_kernel, chunk=chunk),
        grid=(batch, nt),
        in_specs=[
            zspec(ZB_MQ), zspec(ZB_MK), zspec(ZB_MV), zspec(ZB_MO),
            pl.BlockSpec((chunk, LANES), lambda b, t: (b * nt + t, 0)),
            pl.BlockSpec((LANES, chunk), lambda b, t: (0, b * nt + t)),
            pl.BlockSpec((CONV_WIDTH, 2 * D_MODEL), lambda b, t: (0, 0)),
            pl.BlockSpec((1, D_MODEL), lambda b, t: (0, 0)),
        ],
        out_specs=pl.BlockSpec((chunk, D_MODEL), lambda b, t: (b * nt + t, 0)),
        out_shape=jax.ShapeDtypeStruct((n, D_MODEL), BF16),
        scratch_shapes=[
            pltpu.VMEM((M_HEADS, M_HEAD_DIM, M_HEAD_DIM), F32),
            pltpu.VMEM((M_HEADS, 1, M_HEAD_DIM), F32),
            pltpu.VMEM((M_HEADS, SUBLANES, LANES), F32),
            pltpu.VMEM((SUBLANES, 2 * D_MODEL), F32),
            pltpu.VMEM((chunk, 2 * D_MODEL), F32),
        ],
        compiler_params=_cparams(("parallel", "arbitrary")),
        name="mlstm",
    )(z, z, z, z, gates_col, gates_row, conv_w, m_norm_g)


LOG2E = 1.4426950408889634
FOX_SAFE_LOG2_RANGE = 100.0


def _split3(c):
    hi = c.astype(BF16).astype(F32)
    r1 = c - hi
    mid = r1.astype(BF16).astype(F32)
    lo = (r1 - mid).astype(BF16).astype(F32)
    return hi, mid, lo


def _fox_aug(c, lane, is_query, shift):
    hi, mid, lo = _split3(c if is_query else -c)
    off = 0 if is_query else 3
    ex = jnp.where(lane == off, hi, jnp.where(lane == off + 1, mid, jnp.where(lane == off + 2, lo, 0.0)))
    ones_at = (lane >= 3) & (lane < 6) if is_query else lane < 3
    ex = jnp.where(ones_at, 1.0, ex)
    return jnp.where(lane == 6, -shift if is_query else 1.0, ex).astype(BF16)


def _fox_kernel(bnd_ref, q_ref, k_ref, v_ref, ccol_ref, qg_ref, kg_ref, out_ref,
                ka_sc, va_sc, m_sc, acc_sc, s_sc, *, tq, seq, hp):
    g = pl.program_id(1)
    i = pl.program_id(2)
    dh = F_HEAD_DIM
    lane = lax.broadcasted_iota(I32, (tq, LANES), 1)
    shift = bnd_ref[0]
    safe = bnd_ref[1] > 0.5

    def c_of(r0, hh):
        cc = ccol_ref[pl.ds(r0, tq), :]
        return jnp.sum(jnp.where(lane == G_FF + g * hp + hh, cc, 0.0), axis=1, keepdims=True) * LOG2E

    @pl.when(i == 0)
    def _():
        def prep_chunk(c, carry):
            r0 = pl.multiple_of(c * tq, tq)
            for hh in range(hp):
                kk = k_ref[pl.ds(r0, tq), hh * dh:(hh + 1) * dh].astype(F32)
                r = lax.rsqrt(jnp.mean(kk * kk, axis=-1, keepdims=True) + EPS)
                ka_sc[hh, pl.ds(r0, tq), 0:dh] = ((kk * r) * kg_ref[...]).astype(BF16)
                ka_sc[hh, pl.ds(r0, tq), dh:2 * dh] = _fox_aug(c_of(r0, hh), lane, False, shift)
                va_sc[hh, pl.ds(r0, tq), 0:dh] = v_ref[pl.ds(r0, tq), hh * dh:(hh + 1) * dh]
                va_sc[hh, pl.ds(r0, tq), dh:2 * dh] = jnp.ones((tq, dh), BF16)
            return carry
        lax.fori_loop(0, seq // tq, prep_chunk, 0)

    q0 = pl.multiple_of(i * tq, tq)
    qa = []
    for hh in range(hp):
        q = q_ref[:, hh * dh:(hh + 1) * dh].astype(F32)
        r = lax.rsqrt(jnp.mean(q * q, axis=-1, keepdims=True) + EPS)
        qn = (((q * r) * qg_ref[...]) * (dh ** -0.5 * LOG2E)).astype(BF16)
        qa.append(jnp.concatenate([qn, _fox_aug(c_of(q0, hh), lane, True, shift)], axis=1))

    acc_sc[...] = jnp.zeros_like(acc_sc)

    def scores(j, masked):
        r0 = pl.multiple_of(j * tq, tq)
        s = [lax.dot_general(qa[hh], ka_sc[hh, pl.ds(r0, tq), :], (((1,), (1,)), ((), ())),
                             preferred_element_type=F32) for hh in range(hp)]
        if masked:
            rr = lax.broadcasted_iota(I32, (tq, tq), 0)
            cc = lax.broadcasted_iota(I32, (tq, tq), 1)
            s = [jnp.where(rr >= cc, sh, -jnp.inf) for sh in s]
        return r0, s

    def fixed_consume(j, s):
        r0 = pl.multiple_of(j * tq, tq)
        for hh in range(hp):
            p = jnp.exp2(s[hh]).astype(BF16)
            acc_sc[hh] += jnp.dot(p, va_sc[hh, pl.ds(r0, tq), :], preferred_element_type=F32)

    def run_fixed():
        def produce(buf, j, masked):
            _, s = scores(j, masked)
            for hh in range(hp):
                s_sc[buf, hh] = s[hh]

        def consume(buf, j):
            fixed_consume(j, [s_sc[buf, hh] for hh in range(hp)])

        def pair(j0):
            produce(1, j0, False)
            consume(0, jnp.where(j0 == 0, i, j0 - 1))
            produce(0, j0 + 1, False)
            consume(1, j0)

        produce(0, i, True)
        quads = i // 4
        pairs = i // 2

        def quad_body(k, carry):
            pair(4 * k)
            pair(4 * k + 2)
            return carry
        lax.fori_loop(0, quads, quad_body, 0)

        def pair_body(k, carry):
            pair(2 * k)
            return carry
        lax.fori_loop(2 * quads, pairs, pair_body, 0)
        pending = jnp.where(pairs == 0, i, 2 * pairs - 1)

        @pl.when(i % 2 == 1)
        def _():
            produce(1, i - 1, False)
            consume(0, pending)
            consume(1, i - 1)

        @pl.when(i % 2 == 0)
        def _():
            consume(0, pending)

    def running_step(j, masked):
        r0, s = scores(j, masked)
        for hh in range(hp):
            m_old = m_sc[hh]
            m_new = jnp.maximum(m_old, jnp.max(s[hh], axis=1, keepdims=True))
            p = jnp.exp2(s[hh] - m_new).astype(BF16)
            alpha = jnp.exp2(m_old - m_new)
            acc_sc[hh] = alpha * acc_sc[hh] + jnp.dot(p, va_sc[hh, pl.ds(r0, tq), :],
                                                      preferred_element_type=F32)
            m_sc[hh] = m_new

    def run(step):
        def body(j, carry):
            step(j, False)
            return carry
        lax.fori_loop(0, i, body, 0)
        step(i, True)

    @pl.when(safe)
    def _():
        run_fixed()

    @pl.when(jnp.logical_not(safe))
    def _():
        m_sc[...] = jnp.full_like(m_sc, -jnp.inf)
        run(running_step)

    for hh in range(hp):
        out_ref[:, hh * dh:(hh + 1) * dh] = (acc_sc[hh, :, 0:dh] / acc_sc[hh, :, dh:2 * dh]).astype(BF16)


def _fox(z, ccol, qn_g, kn_g, batch, seq, tq, hp):
    n = batch * seq
    nq = seq // tq
    w = hp * F_HEAD_DIM
    fq0 = ZB_FQ * (D_MODEL // w)
    fk0 = ZB_FK * (D_MODEL // w)
    fv0 = ZB_FV * (D_MODEL // w)
    shift = jnp.ceil(1.01 * LOG2E * F_HEAD_DIM ** 0.5 * jnp.max(jnp.abs(qn_g)) * jnp.max(jnp.abs(kn_g)))
    safe = (2.0 * shift <= FOX_SAFE_LOG2_RANGE).astype(F32)
    bnd = jnp.stack([shift, safe]).astype(F32)
    return pl.pallas_call(
        functools.partial(_fox_kernel, tq=tq, seq=seq, hp=hp),
        grid=(batch, F_HEADS // hp, nq),
        in_specs=[
            pl.BlockSpec(memory_space=pltpu.SMEM),
            pl.BlockSpec((tq, w), lambda b, g, i: (b * nq + i, fq0 + g)),
            pl.BlockSpec((seq, w), lambda b, g, i: (b, fk0 + g)),
            pl.BlockSpec((seq, w), lambda b, g, i: (b, fv0 + g)),
            pl.BlockSpec((seq, LANES), lambda b, g, i: (b, 0)),
            pl.BlockSpec((1, F_HEAD_DIM), lambda b, g, i: (0, 0)),
            pl.BlockSpec((1, F_HEAD_DIM), lambda b, g, i: (0, 0)),
        ],
        out_specs=pl.BlockSpec((tq, w), lambda b, g, i: (b * nq + i, g)),
        out_shape=jax.ShapeDtypeStruct((n, D_MODEL), BF16),
        scratch_shapes=[
            pltpu.VMEM((hp, seq, 2 * F_HEAD_DIM), BF16),
            pltpu.VMEM((hp, seq, 2 * F_HEAD_DIM), BF16),
            pltpu.VMEM((hp, tq, 1), F32),
            pltpu.VMEM((hp, tq, 2 * F_HEAD_DIM), F32),
            pltpu.VMEM((2, hp, tq, tq), F32),
        ],
        compiler_params=_cparams(("parallel", "parallel", "arbitrary")),
        name="fox",
    )(bnd, z, z, z, ccol, qn_g, kn_g)


def _topk_rows(s, k, store):
    rows = lax.broadcasted_iota(I32, s.shape, 0).astype(F32)
    n_rows = float(s.shape[0])
    for r in range(k):
        mx = jnp.max(s, axis=0, keepdims=True)
        ix = jnp.min(jnp.where(s == mx, rows, n_rows), axis=0, keepdims=True)
        s = jnp.where(rows == ix, -jnp.inf, s)
        store(r, mx, ix.astype(I32))


def _pair_candidates(v0, v1):
    k = P_TOPK
    w = v0.shape[1]
    row8 = lax.broadcasted_iota(I32, (SUBLANES, w), 0)
    sums = [v0[0:1, :] + v1]
    for a in range(1, SUBLANES):
        nb = k // (a + 1)
        sums.append(jnp.where(row8 < nb, v0[a:a + 1, :] + v1[0:SUBLANES, :], -jnp.inf))
    sums.append(v0[SUBLANES:k, :] + v1[0:1, :])
    return jnp.concatenate(sums, axis=0)


def _pair_of_row(row):
    k = P_TOPK
    q = row - k
    mid_a = 1 + (q >> 3)
    mid_b = q & (SUBLANES - 1)
    last = row >= k + SUBLANES * (SUBLANES - 1)
    a = jnp.where(row < k, 0, jnp.where(last, row - (k + SUBLANES * (SUBLANES - 2)), mid_a))
    b = jnp.where(row < k, row, jnp.where(last, 0, mid_b))
    return a, b


def _merge_kernel(hm_ref, hf_ref, gm_ref, gf_ref, x_ref, wm_ref, wf_ref, wo_ref, g2_ref,
                  wpq_ref, sk_ref, x1_ref, xn_ref, ids_i_ref, ids_j_ref, gate_ref,
                  sc_sc, val_sc, idx_sc, top_sc, pos_sc, oi_sc, oj_sc, og_sc, *, tm, tc):
    ym = jnp.dot(hm_ref[...], wm_ref[...], preferred_element_type=F32)
    yf = jnp.dot(hf_ref[...], wf_ref[...], preferred_element_type=F32)
    y = (_sigmoid(gm_ref[...].astype(F32)) * ym + _sigmoid(gf_ref[...].astype(F32)) * yf)
    x1 = x_ref[...] + jnp.dot(y.astype(BF16), wo_ref[...], preferred_element_type=F32)
    x1_ref[...] = x1
    r = lax.rsqrt(jnp.mean(x1 * x1, axis=-1, keepdims=True) + EPS)
    xn = ((x1 * r) * g2_ref[...]).astype(BF16)
    xn_ref[...] = xn
    qh = jnp.dot(xn, wpq_ref[...], preferred_element_type=F32).astype(BF16)

    for hc in range(2 * P_HEADS):
        sc_sc[hc] = lax.dot_general(sk_ref[hc], qh[:, hc * P_HALF:(hc + 1) * P_HALF],
                                    (((1,), (1,)), ((), ())), preferred_element_type=F32)

    def chunk_body(c, carry):
        c0 = pl.multiple_of(c * tc, tc)

        def first_level(hd, carry2):
            for hc in (2 * hd, 2 * hd + 1):
                s = sc_sc[hc, :, pl.ds(c0, tc)]

                def store(r, val, idx, hc=hc):
                    val_sc[hc, r:r + 1, :] = val
                    idx_sc[hc, r:r + 1, :] = idx
                _topk_rows(s, P_TOPK, store)
            return carry2

        def second_level(hd, carry2):
            v0 = val_sc[2 * hd]
            v1 = val_sc[2 * hd + 1]
            i0 = idx_sc[2 * hd]
            i1 = idx_sc[2 * hd + 1]
            cand = _pair_candidates(v0, v1)

            def store(r, val, idx):
                top_sc[r:r + 1, :] = val
                pos_sc[r:r + 1, :] = idx
            _topk_rows(cand, P_TOPK, store)
            top = top_sc[...]
            pa, pb = _pair_of_row(pos_sc[...])
            sel_i = jnp.zeros(pa.shape, I32)
            sel_j = jnp.zeros(pa.shape, I32)
            for a in range(P_TOPK):
                sel_i = jnp.where(pa == a, i0[a:a + 1, :], sel_i)
                sel_j = jnp.where(pb == a, i1[a:a + 1, :], sel_j)
            e = jnp.exp(top - top[0:1, :])
            gate = e / jnp.sum(e, axis=0, keepdims=True)
            r0 = hd * P_TOPK
            if not isinstance(hd, int):
                r0 = pl.multiple_of(r0, P_TOPK)
            oi_sc[pl.ds(r0, P_TOPK), :] = sel_i
            oj_sc[pl.ds(r0, P_TOPK), :] = sel_j
            og_sc[pl.ds(r0, P_TOPK), :] = gate
            return carry2

        def both_levels(hd, carry2):
            first_level(hd, carry2)
            return second_level(hd - 1, carry2)
        first_level(0, 0)
        lax.fori_loop(1, P_HEADS, both_levels, 0)
        second_level(P_HEADS - 1, 0)

        ids_i_ref[pl.ds(c0, tc), :] = oi_sc[...].T
        ids_j_ref[pl.ds(c0, tc), :] = oj_sc[...].T
        gate_ref[pl.ds(c0, tc), :] = og_sc[...].T
        return carry
    lax.fori_loop(0, tm // tc, chunk_body, 0)


def _merge(hm, hf, z, x2, w_m, w_f, w_o, g2, w_pq, sk, tm, tc):
    n = x2.shape[0]
    hk = P_HEADS * P_TOPK
    row = lambda i: (i, 0)
    const = lambda i: (0, 0)
    return pl.pallas_call(
        functools.partial(_merge_kernel, tm=tm, tc=tc),
        grid=(n // tm,),
        in_specs=[
            pl.BlockSpec((tm, D_MODEL), row),
            pl.BlockSpec((tm, D_MODEL), row),
            pl.BlockSpec((tm, D_MODEL), lambda i: (i, ZB_GM)),
            pl.BlockSpec((tm, D_MODEL), lambda i: (i, ZB_GF)),
            pl.BlockSpec((tm, D_MODEL), row),
            pl.BlockSpec((D_MODEL, D_MODEL), const),
            pl.BlockSpec((D_MODEL, D_MODEL), const),
            pl.BlockSpec((D_MODEL, D_MODEL), const),
            pl.BlockSpec((1, D_MODEL), const),
            pl.BlockSpec((D_MODEL, 2 * P_HEADS * P_HALF), const),
            pl.BlockSpec((2 * P_HEADS, P_KEYS, P_HALF), lambda i: (0, 0, 0)),
        ],
        out_specs=[
            pl.BlockSpec((tm, D_MODEL), row),
            pl.BlockSpec((tm, D_MODEL), row),
            pl.BlockSpec((tm, hk), row),
            pl.BlockSpec((tm, hk), row),
            pl.BlockSpec((tm, hk), row),
        ],
        out_shape=[
            jax.ShapeDtypeStruct((n, D_MODEL), F32),
            jax.ShapeDtypeStruct((n, D_MODEL), BF16),
            jax.ShapeDtypeStruct((n, hk), I32),
            jax.ShapeDtypeStruct((n, hk), I32),
            jax.ShapeDtypeStruct((n, hk), F32),
        ],
        scratch_shapes=[
            pltpu.VMEM((2 * P_HEADS, P_KEYS, tm), F32),
            pltpu.VMEM((2 * P_HEADS, P_TOPK, tc), F32),
            pltpu.VMEM((2 * P_HEADS, P_TOPK, tc), I32),
            pltpu.VMEM((P_TOPK, tc), F32),
            pltpu.VMEM((P_TOPK, tc), I32),
            pltpu.VMEM((hk, tc), I32),
            pltpu.VMEM((hk, tc), I32),
            pltpu.VMEM((hk, tc), F32),
        ],
        compiler_params=_cparams(("parallel",)),
        name="merge_retrieve",
    )(hm, hf, z, z, x2, w_m, w_f, w_o, g2, w_pq, sk)


U_SUB = 512


def _peer_u_kernel(xn_ref, u_ref, ii_ref, jj_ref, gate_ref, hw_ref, ids_ref, acc_sc, *, eb):
    e = pl.program_id(1)
    ne = pl.num_programs(1)

    @pl.when(e == 0)
    def _():
        acc_sc[...] = jnp.zeros_like(acc_sc)

    xn = xn_ref[...]
    ii = ii_ref[...]
    jj = jj_ref[...]
    acc = acc_sc[...]
    for sb in range(eb // U_SUB):
        a = lax.dot_general(xn, u_ref[sb * U_SUB:(sb + 1) * U_SUB, :], (((1,), (1,)), ((), ())),
                            preferred_element_type=F32)
        for c in range(U_SUB // P_KEYS):
            blk = e * (eb // P_KEYS) + sb * (U_SUB // P_KEYS) + c
            g = jnp.take_along_axis(a[:, c * P_KEYS:(c + 1) * P_KEYS], jj, axis=1,
                                    mode="promise_in_bounds")
            acc = jnp.where(ii == blk, g, acc)
    acc_sc[...] = acc

    @pl.when(e == ne - 1)
    def _():
        av = acc_sc[...]
        gelu = 0.5 * av * (1.0 + lax.erf(av * (2.0 ** -0.5)))
        hw_ref[...] = gate_ref[...] * gelu
        ids_ref[...] = (ii * P_KEYS + jj) * (D_MODEL // (2 * LANES))


def _peer_u(xn, u_bf, ids_i, ids_j, gate, tm, eb):
    n = xn.shape[0]
    hk = P_HEADS * P_TOPK
    row = lambda i, e: (i, 0)
    return pl.pallas_call(
        functools.partial(_peer_u_kernel, eb=eb),
        grid=(n // tm, N_EXPERTS // eb),
        in_specs=[
            pl.BlockSpec((tm, D_MODEL), row),
            pl.BlockSpec((eb, D_MODEL), lambda i, e: (e, 0)),
            pl.BlockSpec((tm, hk), row),
            pl.BlockSpec((tm, hk), row),
            pl.BlockSpec((tm, hk), row),
        ],
        out_specs=[pl.BlockSpec((tm, hk), row), pl.BlockSpec((tm, hk), row)],
        out_shape=[jax.ShapeDtypeStruct((n, hk), F32), jax.ShapeDtypeStruct((n, hk), I32)],
        scratch_shapes=[pltpu.VMEM((tm, hk), F32)],
        compiler_params=_cparams(("parallel", "arbitrary")),
        name="peer_u",
    )(xn, u_bf, ids_i, ids_j, gate)


V_ROW_WORDS = D_MODEL // (2 * LANES)


V_SLABS = 128
V_GROUP = 16
V_ID_SPLIT = 4


def _peer_v_kernel(*refs, tt):
    id_refs = refs[:V_ID_SPLIT]
    w_ref, x1_ref, tab_ref, out_ref, slab_sc = refs[V_ID_SPLIT:]
    hk = P_HEADS * P_TOPK
    per = hk // V_ID_SPLIT
    tg = V_GROUP
    plane = tg + SUBLANES
    half = D_MODEL // 2

    def group(g, carry):
        t0 = pl.multiple_of(g * tg, tg)
        wg = w_ref[pl.ds(t0, tg), :]
        acc_lo = [jnp.zeros((tg, LANES), F32) for _ in range(V_ROW_WORDS)]
        acc_hi = [jnp.zeros((tg, LANES), F32) for _ in range(V_ROW_WORDS)]
        for n_slot in range(hk):
            q, k = divmod(n_slot, V_ID_SPLIT)
            m = k * per + q
            slab = slab_sc.at[n_slot % V_SLABS]
            for t in range(tg):
                r0 = pl.multiple_of(id_refs[k][t0 * per + (t * per + q)], V_ROW_WORDS)
                slab[pl.ds(t, V_ROW_WORDS, stride=plane), :] = tab_ref[pl.ds(r0, V_ROW_WORDS), :]
            wcol = wg[:, m:m + 1]
            for s in range(V_ROW_WORDS):
                words = slab[s * plane:s * plane + tg, :]
                acc_lo[s] = acc_lo[s] + wcol * lax.bitcast_convert_type(words << 16, F32)
                acc_hi[s] = acc_hi[s] + wcol * lax.bitcast_convert_type(words & jnp.int32(-65536), F32)
        for s in range(V_ROW_WORDS):
            lo = slice(s * LANES, (s + 1) * LANES)
            hi = slice(half + s * LANES, half + (s + 1) * LANES)
            out_ref[pl.ds(t0, tg), lo] = x1_ref[pl.ds(t0, tg), lo] + acc_lo[s]
            out_ref[pl.ds(t0, tg), hi] = x1_ref[pl.ds(t0, tg), hi] + acc_hi[s]
        return carry
    lax.fori_loop(0, tt // tg, group, 0)


def _peer_v(ids, hw, x1, v_packed, tt):
    n = ids.shape[0]
    hk = P_HEADS * P_TOPK
    per = hk // V_ID_SPLIT
    id_parts = [ids[:, k * per:(k + 1) * per].reshape(n * per) for k in range(V_ID_SPLIT)]
    return pl.pallas_call(
        functools.partial(_peer_v_kernel, tt=tt),
        grid=(n // tt,),
        in_specs=[pl.BlockSpec((tt * per,), lambda i: (i,), memory_space=pltpu.SMEM)] * V_ID_SPLIT + [
            pl.BlockSpec((tt, hk), lambda i: (i, 0)),
            pl.BlockSpec((tt, D_MODEL), lambda i: (i, 0)),
            pl.BlockSpec((N_EXPERTS * V_ROW_WORDS, LANES), lambda i: (0, 0),
                         pipeline_mode=pl.Buffered(1)),
        ],
        out_specs=pl.BlockSpec((tt, D_MODEL), lambda i: (i, 0)),
        out_shape=jax.ShapeDtypeStruct((n, D_MODEL), F32),
        scratch_shapes=[pltpu.VMEM((V_SLABS, V_ROW_WORDS * (V_GROUP + SUBLANES), LANES), I32)],
        compiler_params=_cparams(("arbitrary",)),
        name="peer_v",
    )(*id_parts, hw, x1, v_packed)


def _pack_v(v_tab):
    vb = lax.bitcast_convert_type(v_tab.astype(BF16), jnp.uint16).astype(jnp.uint32)
    half = D_MODEL // 2
    packed = vb[:, :half] | (vb[:, half:] << 16)
    return lax.bitcast_convert_type(packed, I32).reshape(N_EXPERTS * V_ROW_WORDS, LANES)


def _layer(x2, batch, seq, norm1_g, w_in, b_in, conv_w, m_norm_g, qn_g, kn_g, w_m_out, w_f_out,
           w_out, norm2_g, w_pq, sub_keys, u_tab, v_tab):
    o = _OFFS
    cols = lambda a: jnp.concatenate([a[..., o[0]:o[4]], a[..., o[6]:o[9]], a[..., o[10]:o[12]]], axis=-1)
    small = lambda a: jnp.concatenate([a[..., o[4]:o[6]], a[..., o[9]:o[10]]], axis=-1)
    w_main = cols(w_in).astype(BF16)
    b_main = cols(b_in)[None, :]
    ws = small(w_in)
    bs = small(b_in)
    pad = LANES - ws.shape[1]
    ws = jnp.pad(ws, ((0, 0), (0, pad))).astype(BF16)
    bs = jnp.pad(bs, (0, pad))

    tm_a = min(512, x2.shape[0])
    z, gates_col, gates_row = _inproj(x2, norm1_g[None, :], w_main, b_main, ws, ws.T,
                                      bs[None, :], bs[:, None], tm_a)
    ccol = _forget_cumsum(gates_row, batch, seq)
    chunk = min(256, seq)
    hm = _mlstm(z, gates_col, gates_row, conv_w, m_norm_g[None, :], batch, seq, chunk)
    tq = min(512, seq)
    hf = _fox(z, ccol, qn_g[None, :], kn_g[None, :], batch, seq, tq, 2)

    sk = sub_keys.reshape(2 * P_HEADS, P_KEYS, P_HALF).astype(BF16)
    tm_d = min(512, x2.shape[0])
    x1, xn, ids_i, ids_j, gate = _merge(hm, hf, z, x2, w_m_out.astype(BF16), w_f_out.astype(BF16),
                                        w_out.astype(BF16), norm2_g[None, :], w_pq.astype(BF16), sk,
                                        tm_d, 512)
    tm_e = min(1024, x2.shape[0])
    hw, ids = _peer_u(xn, u_tab.astype(BF16), ids_i, ids_j, gate, tm_e, 4096)
    return _peer_v(ids, hw, x1, _pack_v(v_tab), min(512, x1.shape[0]))


def kernel(x, norm1_g, w_in, b_in, conv_w, m_norm_g, qn_g, kn_g, w_m_out, w_f_out, w_out,
           norm2_g, w_pq, sub_keys, u_tab, v_tab):
    batch, seq, d = x.shape
    x2 = x.reshape(batch * seq, d)
    for l in range(w_in.shape[0]):
        x2 = _layer(x2, batch, seq, norm1_g[l], w_in[l], b_in[l], conv_w[l], m_norm_g[l], qn_g[l],
                    kn_g[l], w_m_out[l], w_f_out[l], w_out[l], norm2_g[l], w_pq[l], sub_keys[l],
                    u_tab[l], v_tab[l])
    return x2.reshape(batch, seq, d)
```

```python
import functools

import jax
import jax.numpy as jnp
from jax import lax
from jax.experimental import pallas as pl
from jax.experimental.pallas import tpu as pltpu

F32 = jnp.float32
BF16 = jnp.bfloat16
I32 = jnp.int32

D_MODEL = 1024
M_HEADS = 4
M_HEAD_DIM = 256
F_HEADS = 8
F_HEAD_DIM = 128
CONV_WIDTH = 4
P_HEADS = 8
P_KEYS = 128
P_TOPK = 16
P_HALF = 128
N_EXPERTS = P_KEYS * P_KEYS
EPS = 1e-6
M_INIT = -1e30
LANES = 128
SUBLANES = 8
VMEM_LIMIT = 56 * 1024 * 1024

_SIZES = (1024, 1024, 1024, 1024, M_HEADS, M_HEADS, 1024, 1024, 1024, F_HEADS, 1024, 1024)
_OFFS = [0]
for _s in _SIZES:
    _OFFS.append(_OFFS[-1] + _s)
Z_COLS = 9 * 1024
ZB_MQ, ZB_MK, ZB_MV, ZB_MO, ZB_FQ, ZB_FK, ZB_FV, ZB_GM, ZB_GF = range(9)
G_MI, G_MF, G_FF = 0, M_HEADS, 2 * M_HEADS


def _cparams(sem):
    return pltpu.CompilerParams(dimension_semantics=sem, vmem_limit_bytes=VMEM_LIMIT)


def _log_sigmoid(x):
    return jnp.minimum(x, 0.0) - jnp.log1p(jnp.exp(-jnp.abs(x)))


def _sigmoid(x):
    return 0.5 * jnp.tanh(0.5 * x) + 0.5


INPROJ_TN = 3072


def _inproj_kernel(x_ref, g_ref, w_ref, b_ref, ws_ref, wst_ref, bs_ref, bst_ref,
                   z_ref, gc_ref, gr_ref):
    x = x_ref[...]
    r = lax.rsqrt(jnp.mean(x * x, axis=-1, keepdims=True) + EPS)
    h = ((x * r) * g_ref[...]).astype(BF16)
    gc_ref[...] = jnp.dot(h, ws_ref[...], preferred_element_type=F32) + bs_ref[...]
    gr_ref[...] = lax.dot_general(wst_ref[...], h, (((1,), (1,)), ((), ())),
                                  preferred_element_type=F32) + bst_ref[...]
    for c in range(Z_COLS // INPROJ_TN):
        cols = slice(c * INPROJ_TN, (c + 1) * INPROJ_TN)
        z = jnp.dot(h, w_ref[:, cols], preferred_element_type=F32) + b_ref[:, cols]
        z_ref[:, cols] = z.astype(BF16)


def _inproj(x2, g1, w_main, b_main, w_small, w_small_t, b_small, b_small_t, tm):
    n = x2.shape[0]
    const = lambda i: (0, 0)
    return pl.pallas_call(
        _inproj_kernel,
        grid=(n // tm,),
        in_specs=[
            pl.BlockSpec((tm, D_MODEL), lambda i: (i, 0)),
            pl.BlockSpec((1, D_MODEL), const),
            pl.BlockSpec((D_MODEL, Z_COLS), const, pipeline_mode=pl.Buffered(1)),
            pl.BlockSpec((1, Z_COLS), const),
            pl.BlockSpec((D_MODEL, LANES), const),
            pl.BlockSpec((LANES, D_MODEL), const),
            pl.BlockSpec((1, LANES), const),
            pl.BlockSpec((LANES, 1), const),
        ],
        out_specs=[
            pl.BlockSpec((tm, Z_COLS), lambda i: (i, 0)),
            pl.BlockSpec((tm, LANES), lambda i: (i, 0)),
            pl.BlockSpec((LANES, tm), lambda i: (0, i)),
        ],
        out_shape=[
            jax.ShapeDtypeStruct((n, Z_COLS), BF16),
            jax.ShapeDtypeStruct((n, LANES), F32),
            jax.ShapeDtypeStruct((LANES, n), F32),
        ],
        compiler_params=_cparams(("parallel",)),
        name="inproj",
    )(x2, g1, w_main, b_main, w_small, w_small_t, b_small, b_small_t)


def _cumsum_kernel(gr_ref, ccol_ref):
    x = _log_sigmoid(gr_ref[...])
    t = x.shape[1]
    lane = lax.broadcasted_iota(I32, x.shape, 1)
    s = 1
    while s < t:
        x = x + jnp.where(lane >= s, pltpu.roll(x, s, axis=1), 0.0)
        s *= 2
    ccol_ref[...] = x.T


def _forget_cumsum(gates_row, batch, seq):
    n = batch * seq
    return pl.pallas_call(
        _cumsum_kernel,
        grid=(batch,),
        in_specs=[pl.BlockSpec((LANES, seq), lambda b: (0, b))],
        out_specs=pl.BlockSpec((seq, LANES), lambda b: (b, 0)),
        out_shape=jax.ShapeDtypeStruct((n, LANES), F32),
        compiler_params=_cparams(("parallel",)),
        name="forget_cumsum",
    )(gates_row)


def _mlstm_kernel(q_ref, k_ref, v_ref, o_ref, gc_ref, gr_ref, cw_ref, mg_ref, out_ref,
                  c_sc, n_sc, m_sc, tail_sc, qk_sc, *, chunk):
    L = chunk
    dh = M_HEAD_DIM
    t = pl.program_id(1)

    @pl.when(t == 0)
    def _():
        c_sc[...] = jnp.zeros_like(c_sc)
        n_sc[...] = jnp.zeros_like(n_sc)
        m_sc[...] = jnp.full_like(m_sc, M_INIT)
        tail_sc[...] = jnp.zeros_like(tail_sc)

    gc = gc_ref[...]
    gr = gr_ref[...]
    lsg_c = _log_sigmoid(gc)
    lsg_r = _log_sigmoid(gr[0:2 * M_HEADS, :])
    row_i = lax.broadcasted_iota(I32, (L, L), 0)
    col_i = lax.broadcasted_iota(I32, (L, L), 1)
    tril = col_i <= row_i
    shift = [(row_i - col_i == d).astype(BF16) for d in range(1, CONV_WIDTH)]
    row8 = lax.broadcasted_iota(I32, (SUBLANES, dh), 0)

    def conv_silu(x_ref, cx, c0):
        x = x_ref[:, cx:cx + dh]
        tail = tail_sc[:, c0:c0 + dh]
        acc = x.astype(F32) * cw_ref[CONV_WIDTH - 1:CONV_WIDTH, c0:c0 + dh]
        for d in range(1, CONV_WIDTH):
            xs = jnp.dot(shift[d - 1], x, preferred_element_type=F32)
            head = jnp.where(row8 < d, pltpu.roll(tail, d, axis=0), xs[0:SUBLANES, :])
            xs = jnp.concatenate([head, xs[SUBLANES:, :]], axis=0)
            acc = acc + xs * cw_ref[CONV_WIDTH - 1 - d:CONV_WIDTH - d, c0:c0 + dh]
        tail_sc[:, c0:c0 + dh] = x[L - SUBLANES:L, :].astype(F32)
        return acc * _sigmoid(acc)

    for h in range(M_HEADS):
        qk_sc[:, h * dh:(h + 1) * dh] = conv_silu(q_ref, h * dh, h * dh) * (dh ** -0.5)
        qk_sc[:, D_MODEL + h * dh:D_MODEL + (h + 1) * dh] = conv_silu(k_ref, h * dh, D_MODEL + h * dh)

    for h in range(M_HEADS):
        q = qk_sc[:, h * dh:(h + 1) * dh]
        k = qk_sc[:, D_MODEL + h * dh:D_MODEL + (h + 1) * dh]
        v = v_ref[:, h * dh:(h + 1) * dh]
        qb = q.astype(BF16)
        kb = k.astype(BF16)

        i_col = gc[:, G_MI + h:G_MI + h + 1]
        lf_col = lsg_c[:, G_MF + h:G_MF + h + 1]
        i_row = gr[G_MI + h:G_MI + h + 1, :]
        lf_row = lsg_r[G_MF + h:G_MF + h + 1, :]

        b_col = jnp.sum(jnp.where(tril, lf_row, 0.0), axis=1, keepdims=True)
        b_row = jnp.sum(jnp.where(row_i <= col_i, lf_col, 0.0), axis=0, keepdims=True)
        g = jnp.sum(lf_row, axis=1, keepdims=True)
        m_prev = m_sc[h][0:1, 0:1]

        dmat = jnp.where(tril, b_col - b_row + i_row, -jnp.inf)
        inter = b_col + m_prev
        m_t = jnp.maximum(jnp.max(dmat, axis=1, keepdims=True), inter)
        p = jnp.exp(dmat - m_t)
        s = lax.dot_general(qb, kb, (((1,), (1,)), ((), ())), preferred_element_type=F32) * p
        w_inter = jnp.exp(inter - m_t)
        c_old = c_sc[h]
        n_old = n_sc[h]
        num = (jnp.dot(s.astype(BF16), v, preferred_element_type=F32)
               + w_inter * jnp.dot(qb, c_old.astype(BF16), preferred_element_type=F32))
        den = (jnp.sum(s, axis=1, keepdims=True)
               + w_inter * jnp.sum(q * n_old, axis=1, keepdims=True))
        hh = num / jnp.maximum(jnp.abs(den), jnp.exp(-m_t))

        log_w = g - b_col + i_col
        m_new = jnp.maximum(g + m_prev, jnp.max(log_w, axis=0, keepdims=True))
        w_s = jnp.exp(log_w - m_new)
        decay = jnp.exp(g + m_prev - m_new)
        wv = (w_s * v.astype(F32)).astype(BF16)
        c_sc[h] = decay * c_old + lax.dot_general(kb, wv, (((0,), (0,)), ((), ())),
                                                  preferred_element_type=F32)
        n_sc[h] = decay * n_old + jnp.sum(w_s * k, axis=0, keepdims=True)
        m_sc[h] = jnp.broadcast_to(m_new, (SUBLANES, LANES))

        r = lax.rsqrt(jnp.mean(hh * hh, axis=-1, keepdims=True) + EPS)
        hn = (hh * r) * mg_ref[0:1, h * dh:(h + 1) * dh]
        og = _sigmoid(o_ref[:, h * dh:(h + 1) * dh].astype(F32))
        out_ref[:, h * dh:(h + 1) * dh] = (hn * og).astype(BF16)


def _mlstm(z, gates_col, gates_row, conv_w, m_norm_g, batch, seq, chunk):
    n = batch * seq
    nt = seq // chunk
    zspec = lambda cb: pl.BlockSpec((chunk, D_MODEL), lambda b, t, cb=cb: (b * nt + t, cb))
    return pl.pallas_call(
        functools.partial(_mlstm_kernel, chunk=chunk),
        grid=(batch, nt),
        in_specs=[
            zspec(ZB_MQ), zspec(ZB_MK), zspec(ZB_MV), zspec(ZB_MO),
            pl.BlockSpec((chunk, LANES), lambda b, t: (b * nt + t, 0)),
            pl.BlockSpec((LANES, chunk), lambda b, t: (0, b * nt + t)),
            pl.BlockSpec((CONV_WIDTH, 2 * D_MODEL), lambda b, t: (0, 0)),
            pl.BlockSpec((1, D_MODEL), lambda b, t: (0, 0)),
        ],
        out_specs=pl.BlockSpec((chunk, D_MODEL), lambda b, t: (b * nt + t, 0)),
        out_shape=jax.ShapeDtypeStruct((n, D_MODEL), BF16),
        scratch_shapes=[
            pltpu.VMEM((M_HEADS, M_HEAD_DIM, M_HEAD_DIM), F32),
            pltpu.VMEM((M_HEADS, 1, M_HEAD_DIM), F32),
            pltpu.VMEM((M_HEADS, SUBLANES, LANES), F32),
            pltpu.VMEM((SUBLANES, 2 * D_MODEL), F32),
            pltpu.VMEM((chunk, 2 * D_MODEL), F32),
        ],
        compiler_params=_cparams(("parallel", "arbitrary")),
        name="mlstm",
    )(z, z, z, z, gates_col, gates_row, conv_w, m_norm_g)


LOG2E = 1.4426950408889634
FOX_SAFE_LOG2_RANGE = 100.0


def _split3(c):
    hi = c.astype(BF16).astype(F32)
    r1 = c - hi
    mid = r1.astype(BF16).astype(F32)
    lo = (r1 - mid).astype(BF16).astype(F32)
    return hi, mid, lo


def _fox_aug_select(head_col, is_query):
    off = 0 if is_query else 3
    r = lax.broadcasted_iota(I32, (3 * LANES, LANES), 0)
    c = lax.broadcasted_iota(I32, (3 * LANES, LANES), 1)
    hit = (r == (c - off) * LANES + head_col) & (c >= off) & (c < off + 3)
    return jnp.where(hit, 1.0 if is_query else -1.0, 0.0).astype(BF16)


def _fox_aug(c3, sel, lane1, is_query, shift):
    ones_at = (lane1 >= 3) & (lane1 < 6) if is_query else lane1 < 3
    const = jnp.where(ones_at, 1.0, jnp.where(lane1 == 6, -shift if is_query else 1.0, 0.0))
    return (jnp.dot(c3, sel, preferred_element_type=F32) + const).astype(BF16)


def _fox_kernel(bnd_ref, q_ref, k_ref, v_ref, ccol_ref, qg_ref, kg_ref, out_ref,
                ka_sc, va_sc, m_sc, acc_sc, s_sc, sel_sc, *, tq, seq, hp):
    g = pl.program_id(1)
    i = pl.program_id(2)
    dh = F_HEAD_DIM
    shift = bnd_ref[0]
    safe = bnd_ref[1] > 0.5

    lane1 = lax.broadcasted_iota(I32, (1, LANES), 1)

    def c_planes(r0):
        hi, mid, lo = _split3(ccol_ref[pl.ds(r0, tq), :] * LOG2E)
        return jnp.concatenate([hi, mid, lo], axis=1).astype(BF16)

    @pl.when(i == 0)
    def _():
        for hh in range(hp):
            sel_sc[hh] = _fox_aug_select(G_FF + g * hp + hh, False)
            sel_sc[hp + hh] = _fox_aug_select(G_FF + g * hp + hh, True)

        def prep_chunk(c, carry):
            r0 = pl.multiple_of(c * tq, tq)
            c3 = c_planes(r0)
            for hh in range(hp):
                kk = k_ref[pl.ds(r0, tq), hh * dh:(hh + 1) * dh].astype(F32)
                r = lax.rsqrt(jnp.mean(kk * kk, axis=-1, keepdims=True) + EPS)
                ka_sc[hh, pl.ds(r0, tq), 0:dh] = ((kk * r) * kg_ref[...]).astype(BF16)
                ka_sc[hh, pl.ds(r0, tq), dh:2 * dh] = _fox_aug(c3, sel_sc[hh], lane1, False, shift)
                va_sc[hh, pl.ds(r0, tq), 0:dh] = v_ref[pl.ds(r0, tq), hh * dh:(hh + 1) * dh]
                va_sc[hh, pl.ds(r0, tq), dh:2 * dh] = jnp.ones((tq, dh), BF16)
            return carry
        lax.fori_loop(0, seq // tq, prep_chunk, 0)

    q0 = pl.multiple_of(i * tq, tq)
    qc3 = c_planes(q0)
    qa = []
    for hh in range(hp):
        q = q_ref[:, hh * dh:(hh + 1) * dh].astype(F32)
        r = lax.rsqrt(jnp.mean(q * q, axis=-1, keepdims=True) + EPS)
        qn = (((q * r) * qg_ref[...]) * (dh ** -0.5 * LOG2E)).astype(BF16)
        qa.append(jnp.concatenate([qn, _fox_aug(qc3, sel_sc[hp + hh], lane1, True, shift)], axis=1))

    acc_sc[...] = jnp.zeros_like(acc_sc)

    def scores(j, masked):
        r0 = pl.multiple_of(j * tq, tq)
        s = [lax.dot_general(qa[hh], ka_sc[hh, pl.ds(r0, tq), :], (((1,), (1,)), ((), ())),
                             preferred_element_type=F32) for hh in range(hp)]
        if masked:
            rr = lax.broadcasted_iota(I32, (tq, tq), 0)
            cc = lax.broadcasted_iota(I32, (tq, tq), 1)
            s = [jnp.where(rr >= cc, sh, -jnp.inf) for sh in s]
        return r0, s

    def fixed_consume(j, s):
        r0 = pl.multiple_of(j * tq, tq)
        for hh in range(hp):
            p = jnp.exp2(s[hh]).astype(BF16)
            acc_sc[hh] += jnp.dot(p, va_sc[hh, pl.ds(r0, tq), :], preferred_element_type=F32)

    def run_fixed():
        def produce(buf, j, masked):
            _, s = scores(j, masked)
            for hh in range(hp):
                s_sc[buf, hh] = s[hh]

        def consume(buf, j):
            fixed_consume(j, [s_sc[buf, hh] for hh in range(hp)])

        def pair(j0):
            produce(1, j0, False)
            consume(0, jnp.where(j0 == 0, i, j0 - 1))
            produce(0, j0 + 1, False)
            consume(1, j0)

        produce(0, i, True)
        quads = i // 4
        pairs = i // 2

        def quad_body(k, carry):
            pair(4 * k)
            pair(4 * k + 2)
            return carry
        lax.fori_loop(0, quads, quad_body, 0)

        def pair_body(k, carry):
            pair(2 * k)
            return carry
        lax.fori_loop(2 * quads, pairs, pair_body, 0)
        pending = jnp.where(pairs == 0, i, 2 * pairs - 1)

        @pl.when(i % 2 == 1)
        def _():
            produce(1, i - 1, False)
            consume(0, pending)
            consume(1, i - 1)

        @pl.when(i % 2 == 0)
        def _():
            consume(0, pending)

    def running_step(j, masked):
        r0, s = scores(j, masked)
        for hh in range(hp):
            m_old = m_sc[hh]
            m_new = jnp.maximum(m_old, jnp.max(s[hh], axis=1, keepdims=True))
            p = jnp.exp2(s[hh] - m_new).astype(BF16)
            alpha = jnp.exp2(m_old - m_new)
            acc_sc[hh] = alpha * acc_sc[hh] + jnp.dot(p, va_sc[hh, pl.ds(r0, tq), :],
                                                      preferred_element_type=F32)
            m_sc[hh] = m_new

    def run(step):
        def body(j, carry):
            step(j, False)
            return carry
        lax.fori_loop(0, i, body, 0)
        step(i, True)

    @pl.when(safe)
    def _():
        run_fixed()

    @pl.when(jnp.logical_not(safe))
    def _():
        m_sc[...] = jnp.full_like(m_sc, -jnp.inf)
        run(running_step)

    for hh in range(hp):
        out_ref[:, hh * dh:(hh + 1) * dh] = (acc_sc[hh, :, 0:dh] / acc_sc[hh, :, dh:2 * dh]).astype(BF16)


def _fox(z, ccol, qn_g, kn_g, batch, seq, tq, hp):
    n = batch * seq
    nq = seq // tq
    w = hp * F_HEAD_DIM
    fq0 = ZB_FQ * (D_MODEL // w)
    fk0 = ZB_FK * (D_MODEL // w)
    fv0 = ZB_FV * (D_MODEL // w)
    shift = jnp.ceil(1.01 * LOG2E * F_HEAD_DIM ** 0.5 * jnp.max(jnp.abs(qn_g)) * jnp.max(jnp.abs(kn_g)))
    safe = (2.0 * shift <= FOX_SAFE_LOG2_RANGE).astype(F32)
    bnd = jnp.stack([shift, safe]).astype(F32)
    return pl.pallas_call(
        functools.partial(_fox_kernel, tq=tq, seq=seq, hp=hp),
        grid=(batch, F_HEADS // hp, nq),
        in_specs=[
            pl.BlockSpec(memory_space=pltpu.SMEM),
            pl.BlockSpec((tq, w), lambda b, g, i: (b * nq + i, fq0 + g)),
            pl.BlockSpec((seq, w), lambda b, g, i: (b, fk0 + g)),
            pl.BlockSpec((seq, w), lambda b, g, i: (b, fv0 + g)),
            pl.BlockSpec((seq, LANES), lambda b, g, i: (b, 0)),
            pl.BlockSpec((1, F_HEAD_DIM), lambda b, g, i: (0, 0)),
            pl.BlockSpec((1, F_HEAD_DIM), lambda b, g, i: (0, 0)),
        ],
        out_specs=pl.BlockSpec((tq, w), lambda b, g, i: (b * nq + i, g)),
        out_shape=jax.ShapeDtypeStruct((n, D_MODEL), BF16),
        scratch_shapes=[
            pltpu.VMEM((hp, seq, 2 * F_HEAD_DIM), BF16),
            pltpu.VMEM((hp, seq, 2 * F_HEAD_DIM), BF16),
            pltpu.VMEM((hp, tq, 1), F32),
            pltpu.VMEM((hp, tq, 2 * F_HEAD_DIM), F32),
            pltpu.VMEM((2, hp, tq, tq), F32),
            pltpu.VMEM((2 * hp, 3 * LANES, LANES), BF16),
        ],
        compiler_params=_cparams(("parallel", "parallel", "arbitrary")),
        name="fox",
    )(bnd, z, z, z, ccol, qn_g, kn_g)


def _topk_rows(s, k, store):
    rows = lax.broadcasted_iota(I32, s.shape, 0).astype(F32)
    n_rows = float(s.shape[0])
    for r in range(k):
        mx = jnp.max(s, axis=0, keepdims=True)
        ix = jnp.min(jnp.where(s == mx, rows, n_rows), axis=0, keepdims=True)
        s = jnp.where(rows == ix, -jnp.inf, s)
        store(r, mx, ix.astype(I32))


def _pair_candidates(v0, v1):
    k = P_TOPK
    w = v0.shape[1]
    row8 = lax.broadcasted_iota(I32, (SUBLANES, w), 0)
    sums = [v0[0:1, :] + v1]
    for a in range(1, SUBLANES):
        nb = k // (a + 1)
        sums.append(jnp.where(row8 < nb, v0[a:a + 1, :] + v1[0:SUBLANES, :], -jnp.inf))
    sums.append(v0[SUBLANES:k, :] + v1[0:1, :])
    return jnp.concatenate(sums, axis=0)


def _pair_of_row(row):
    k = P_TOPK
    q = row - k
    mid_a = 1 + (q >> 3)
    mid_b = q & (SUBLANES - 1)
    last = row >= k + SUBLANES * (SUBLANES - 1)
    a = jnp.where(row < k, 0, jnp.where(last, row - (k + SUBLANES * (SUBLANES - 2)), mid_a))
    b = jnp.where(row < k, row, jnp.where(last, 0, mid_b))
    return a, b


def _merge_kernel(hm_ref, hf_ref, gm_ref, gf_ref, x_ref, wm_ref, wf_ref, wo_ref, g2_ref,
                  wpq_ref, sk_ref, x1_ref, xn_ref, ids_i_ref, ids_j_ref, gate_ref,
                  sc_sc, val_sc, idx_sc, top_sc, pos_sc, oi_sc, oj_sc, og_sc, *, tm, tc):
    ym = jnp.dot(hm_ref[...], wm_ref[...], preferred_element_type=F32)
    yf = jnp.dot(hf_ref[...], wf_ref[...], preferred_element_type=F32)
    y = (_sigmoid(gm_ref[...].astype(F32)) * ym + _sigmoid(gf_ref[...].astype(F32)) * yf)
    x1 = x_ref[...] + jnp.dot(y.astype(BF16), wo_ref[...], preferred_element_type=F32)
    x1_ref[...] = x1
    r = lax.rsqrt(jnp.mean(x1 * x1, axis=-1, keepdims=True) + EPS)
    xn = ((x1 * r) * g2_ref[...]).astype(BF16)
    xn_ref[...] = xn
    qh = jnp.dot(xn, wpq_ref[...], preferred_element_type=F32).astype(BF16)

    for hc in range(2 * P_HEADS):
        sc_sc[hc] = lax.dot_general(sk_ref[hc], qh[:, hc * P_HALF:(hc + 1) * P_HALF],
                                    (((1,), (1,)), ((), ())), preferred_element_type=F32)

    def chunk_body(c, carry):
        c0 = pl.multiple_of(c * tc, tc)

        def first_level(hd, carry2):
            for hc in (2 * hd, 2 * hd + 1):
                s = sc_sc[hc, :, pl.ds(c0, tc)]

                def store(r, val, idx, hc=hc):
                    val_sc[hc, r:r + 1, :] = val
                    idx_sc[hc, r:r + 1, :] = idx
                _topk_rows(s, P_TOPK, store)
            return carry2

        def second_level(hd, carry2):
            v0 = val_sc[2 * hd]
            v1 = val_sc[2 * hd + 1]
            i0 = idx_sc[2 * hd]
            i1 = idx_sc[2 * hd + 1]
            cand = _pair_candidates(v0, v1)

            def store(r, val, idx):
                top_sc[r:r + 1, :] = val
                pos_sc[r:r + 1, :] = idx
            _topk_rows(cand, P_TOPK, store)
            top = top_sc[...]
            pa, pb = _pair_of_row(pos_sc[...])
            sel_i = jnp.zeros(pa.shape, I32)
            sel_j = jnp.zeros(pa.shape, I32)
            for a in range(P_TOPK):
                sel_i = jnp.where(pa == a, i0[a:a + 1, :], sel_i)
                sel_j = jnp.where(pb == a, i1[a:a + 1, :], sel_j)
            e = jnp.exp(top - top[0:1, :])
            gate = e / jnp.sum(e, axis=0, keepdims=True)
            r0 = hd * P_TOPK
            if not isinstance(hd, int):
                r0 = pl.multiple_of(r0, P_TOPK)
            oi_sc[pl.ds(r0, P_TOPK), :] = sel_i
            oj_sc[pl.ds(r0, P_TOPK), :] = sel_j
            og_sc[pl.ds(r0, P_TOPK), :] = gate
            return carry2

        def both_levels(hd, carry2):
            first_level(hd, carry2)
            return second_level(hd - 1, carry2)
        first_level(0, 0)
        lax.fori_loop(1, P_HEADS, both_levels, 0)
        second_level(P_HEADS - 1, 0)

        ids_i_ref[pl.ds(c0, tc), :] = oi_sc[...].T
        ids_j_ref[pl.ds(c0, tc), :] = oj_sc[...].T
        gate_ref[pl.ds(c0, tc), :] = og_sc[...].T
        return carry
    lax.fori_loop(0, tm // tc, chunk_body, 0)


def _merge(hm, hf, z, x2, w_m, w_f, w_o, g2, w_pq, sk, tm, tc):
    n = x2.shape[0]
    hk = P_HEADS * P_TOPK
    row = lambda i: (i, 0)
    const = lambda i: (0, 0)
    return pl.pallas_call(
        functools.partial(_merge_kernel, tm=tm, tc=tc),
        grid=(n // tm,),
        in_specs=[
            pl.BlockSpec((tm, D_MODEL), row),
            pl.BlockSpec((tm, D_MODEL), row),
            pl.BlockSpec((tm, D_MODEL), lambda i: (i, ZB_GM)),
            pl.BlockSpec((tm, D_MODEL), lambda i: (i, ZB_GF)),
            pl.BlockSpec((tm, D_MODEL), row),
            pl.BlockSpec((D_MODEL, D_MODEL), const),
            pl.BlockSpec((D_MODEL, D_MODEL), const),
            pl.BlockSpec((D_MODEL, D_MODEL), const),
            pl.BlockSpec((1, D_MODEL), const),
            pl.BlockSpec((D_MODEL, 2 * P_HEADS * P_HALF), const),
            pl.BlockSpec((2 * P_HEADS, P_KEYS, P_HALF), lambda i: (0, 0, 0)),
        ],
        out_specs=[
            pl.BlockSpec((tm, D_MODEL), row),
            pl.BlockSpec((tm, D_MODEL), row),
            pl.BlockSpec((tm, hk), row),
            pl.BlockSpec((tm, hk), row),
            pl.BlockSpec((tm, hk), row),
        ],
        out_shape=[
            jax.ShapeDtypeStruct((n, D_MODEL), F32),
            jax.ShapeDtypeStruct((n, D_MODEL), BF16),
            jax.ShapeDtypeStruct((n, hk), I32),
            jax.ShapeDtypeStruct((n, hk), I32),
            jax.ShapeDtypeStruct((n, hk), F32),
        ],
        scratch_shapes=[
            pltpu.VMEM((2 * P_HEADS, P_KEYS, tm), F32),
            pltpu.VMEM((2 * P_HEADS, P_TOPK, tc), F32),
            pltpu.VMEM((2 * P_HEADS, P_TOPK, tc), I32),
            pltpu.VMEM((P_TOPK, tc), F32),
            pltpu.VMEM((P_TOPK, tc), I32),
            pltpu.VMEM((hk, tc), I32),
            pltpu.VMEM((hk, tc), I32),
            pltpu.VMEM((hk, tc), F32),
        ],
        compiler_params=_cparams(("parallel",)),
        name="merge_retrieve",
    )(hm, hf, z, z, x2, w_m, w_f, w_o, g2, w_pq, sk)


U_SUB = 512


def _peer_u_kernel(xn_ref, u_ref, ii_ref, jj_ref, gate_ref, hw_ref, ids_ref, acc_sc, *, eb):
    e = pl.program_id(1)
    ne = pl.num_programs(1)

    @pl.when(e == 0)
    def _():
        acc_sc[...] = jnp.zeros_like(acc_sc)

    xn = xn_ref[...]
    ii = ii_ref[...]
    jj = jj_ref[...]
    acc = acc_sc[...]
    for sb in range(eb // U_SUB):
        a = lax.dot_general(xn, u_ref[sb * U_SUB:(sb + 1) * U_SUB, :], (((1,), (1,)), ((), ())),
                            preferred_element_type=F32)
        for c in range(U_SUB // P_KEYS):
            blk = e * (eb // P_KEYS) + sb * (U_SUB // P_KEYS) + c
            g = jnp.take_along_axis(a[:, c * P_KEYS:(c + 1) * P_KEYS], jj, axis=1,
                                    mode="promise_in_bounds")
            acc = jnp.where(ii == blk, g, acc)
    acc_sc[...] = acc

    @pl.when(e == ne - 1)
    def _():
        av = acc_sc[...]
        gelu = 0.5 * av * (1.0 + lax.erf(av * (2.0 ** -0.5)))
        hw_ref[...] = gate_ref[...] * gelu
        ids_ref[...] = (ii * P_KEYS + jj) * (D_MODEL // (2 * LANES))


def _peer_u(xn, u_bf, ids_i, ids_j, gate, tm, eb):
    n = xn.shape[0]
    hk = P_HEADS * P_TOPK
    row = lambda i, e: (i, 0)
    return pl.pallas_call(
        functools.partial(_peer_u_kernel, eb=eb),
        grid=(n // tm, N_EXPERTS // eb),
        in_specs=[
            pl.BlockSpec((tm, D_MODEL), row),
            pl.BlockSpec((eb, D_MODEL), lambda i, e: (e, 0)),
            pl.BlockSpec((tm, hk), row),
            pl.BlockSpec((tm, hk), row),
            pl.BlockSpec((tm, hk), row),
        ],
        out_specs=[pl.BlockSpec((tm, hk), row), pl.BlockSpec((tm, hk), row)],
        out_shape=[jax.ShapeDtypeStruct((n, hk), F32), jax.ShapeDtypeStruct((n, hk), I32)],
        scratch_shapes=[pltpu.VMEM((tm, hk), F32)],
        compiler_params=_cparams(("parallel", "arbitrary")),
        name="peer_u",
    )(xn, u_bf, ids_i, ids_j, gate)


V_ROW_WORDS = D_MODEL // (2 * LANES)


V_SLABS = 128
V_GROUP = 16
V_ID_SPLIT = 4


def _peer_v_kernel(*refs, tt):
    id_refs = refs[:V_ID_SPLIT]
    w_ref, x1_ref, tab_ref, out_ref, slab_sc = refs[V_ID_SPLIT:]
    hk = P_HEADS * P_TOPK
    per = hk // V_ID_SPLIT
    tg = V_GROUP
    plane = tg + SUBLANES
    half = D_MODEL // 2

    def group(g, carry):
        t0 = pl.multiple_of(g * tg, tg)
        wg = w_ref[pl.ds(t0, tg), :]
        acc_lo = [jnp.zeros((tg, LANES), F32) for _ in range(V_ROW_WORDS)]
        acc_hi = [jnp.zeros((tg, LANES), F32) for _ in range(V_ROW_WORDS)]
        for n_slot in range(hk):
            q, k = divmod(n_slot, V_ID_SPLIT)
            m = k * per + q
            slab = slab_sc.at[n_slot % V_SLABS]
            for t in range(tg):
                r0 = pl.multiple_of(id_refs[k][t0 * per + (t * per + q)], V_ROW_WORDS)
                slab[pl.ds(t, V_ROW_WORDS, stride=plane), :] = tab_ref[pl.ds(r0, V_ROW_WORDS), :]
            wcol = wg[:, m:m + 1]
            for s in range(V_ROW_WORDS):
                words = slab[s * plane:s * plane + tg, :]
                acc_lo[s] = acc_lo[s] + wcol * lax.bitcast_convert_type(words << 16, F32)
                acc_hi[s] = acc_hi[s] + wcol * lax.bitcast_convert_type(words & jnp.int32(-65536), F32)
        for s in range(V_ROW_WORDS):
            lo = slice(s * LANES, (s + 1) * LANES)
            hi = slice(half + s * LANES, half + (s + 1) * LANES)
            out_ref[pl.ds(t0, tg), lo] = x1_ref[pl.ds(t0, tg), lo] + acc_lo[s]
            out_ref[pl.ds(t0, tg), hi] = x1_ref[pl.ds(t0, tg), hi] + acc_hi[s]
        return carry
    lax.fori_loop(0, tt // tg, group, 0)


def _peer_v(ids, hw, x1, v_packed, tt):
    n = ids.shape[0]
    hk = P_HEADS * P_TOPK
    per = hk // V_ID_SPLIT
    id_parts = [ids[:, k * per:(k + 1) * per].reshape(n * per) for k in range(V_ID_SPLIT)]
    return pl.pallas_call(
        functools.partial(_peer_v_kernel, tt=tt),
        grid=(n // tt,),
        in_specs=[pl.BlockSpec((tt * per,), lambda i: (i,), memory_space=pltpu.SMEM)] * V_ID_SPLIT + [
            pl.BlockSpec((tt, hk), lambda i: (i, 0)),
            pl.BlockSpec((tt, D_MODEL), lambda i: (i, 0)),
            pl.BlockSpec((N_EXPERTS * V_ROW_WORDS, LANES), lambda i: (0, 0),
                         pipeline_mode=pl.Buffered(1)),
        ],
        out_specs=pl.BlockSpec((tt, D_MODEL), lambda i: (i, 0)),
        out_shape=jax.ShapeDtypeStruct((n, D_MODEL), F32),
        scratch_shapes=[pltpu.VMEM((V_SLABS, V_ROW_WORDS * (V_GROUP + SUBLANES), LANES), I32)],
        compiler_params=_cparams(("arbitrary",)),
        name="peer_v",
    )(*id_parts, hw, x1, v_packed)


def _pack_v(v_tab):
    vb = lax.bitcast_convert_type(v_tab.astype(BF16), jnp.uint16).astype(jnp.uint32)
    half = D_MODEL // 2
    packed = vb[:, :half] | (vb[:, half:] << 16)
    return lax.bitcast_convert_type(packed, I32).reshape(N_EXPERTS * V_ROW_WORDS, LANES)


def _layer(x2, batch, seq, norm1_g, w_in, b_in, conv_w, m_norm_g, qn_g, kn_g, w_m_out, w_f_out,
           w_out, norm2_g, w_pq, sub_keys, u_tab, v_tab):
    o = _OFFS
    cols = lambda a: jnp.concatenate([a[..., o[0]:o[4]], a[..., o[6]:o[9]], a[..., o[10]:o[12]]], axis=-1)
    small = lambda a: jnp.concatenate([a[..., o[4]:o[6]], a[..., o[9]:o[10]]], axis=-1)
    w_main = cols(w_in).astype(BF16)
    b_main = cols(b_in)[None, :]
    ws = small(w_in)
    bs = small(b_in)
    pad = LANES - ws.shape[1]
    ws = jnp.pad(ws, ((0, 0), (0, pad))).astype(BF16)
    bs = jnp.pad(bs, (0, pad))

    tm_a = min(512, x2.shape[0])
    z, gates_col, gates_row = _inproj(x2, norm1_g[None, :], w_main, b_main, ws, ws.T,
                                      bs[None, :], bs[:, None], tm_a)
    ccol = _forget_cumsum(gates_row, batch, seq)
    chunk = min(256, seq)
    hm = _mlstm(z, gates_col, gates_row, conv_w, m_norm_g[None, :], batch, seq, chunk)
    tq = min(512, seq)
    hf = _fox(z, ccol, qn_g[None, :], kn_g[None, :], batch, seq, tq, 2)

    sk = sub_keys.reshape(2 * P_HEADS, P_KEYS, P_HALF).astype(BF16)
    tm_d = min(512, x2.shape[0])
    x1, xn, ids_i, ids_j, gate = _merge(hm, hf, z, x2, w_m_out.astype(BF16), w_f_out.astype(BF16),
                                        w_out.astype(BF16), norm2_g[None, :], w_pq.astype(BF16), sk,
                                        tm_d, 512)
    tm_e = min(1024, x2.shape[0])
    hw, ids = _peer_u(xn, u_tab.astype(BF16), ids_i, ids_j, gate, tm_e, 4096)
    return _peer_v(ids, hw, x1, _pack_v(v_tab), min(512, x1.shape[0]))


def kernel(x, norm1_g, w_in, b_in, conv_w, m_norm_g, qn_g, kn_g, w_m_out, w_f_out, w_out,
           norm2_g, w_pq, sub_keys, u_tab, v_tab):
    batch, seq, d = x.shape
    x2 = x.reshape(batch * seq, d)
    for l in range(w_in.shape[0]):
        x2 = _layer(x2, batch, seq, norm1_g[l], w_in[l], b_in[l], conv_w[l], m_norm_g[l], qn_g[l],
                    kn_g[l], w_m_out[l], w_f_out[l], w_out[l], norm2_g[l], w_pq[l], sub_keys[l],
                    u_tab[l], v_tab[l])
    return x2.reshape(batch, seq, d)
```
